```python
import math
import jax, jax.numpy as jnp
from jax import lax
import numpy as np

D_MODEL = 1024
BATCH = 8
SEQ = 8192
DEPTH = 1

CHUNK = 64
LN_EPS = 1e-5
DEEPNORM_ALPHA = (2.0 * DEPTH) ** 0.25
DEEPNORM_BETA = (8.0 * DEPTH) ** -0.25

ATT_HEADS = 8
ATT_HEAD_DIM = 64
ATT_WIDTH = ATT_HEADS * ATT_HEAD_DIM
ATT_LEFT_CHUNKS = 8
ATT_BAND = (ATT_LEFT_CHUNKS + 1) * CHUNK
MAX_REL = 128
N_REL = 2 * MAX_REL + 1

MLSTM_HEADS = 4
MLSTM_HEAD_DIM = 128
MLSTM_WIDTH = MLSTM_HEADS * MLSTM_HEAD_DIM
CONV_WIDTH = 4

IN_PROJ_WIDTH = 3 * ATT_WIDTH + 4 * MLSTM_WIDTH + 2 * MLSTM_HEADS
IN_SPLITS = (ATT_WIDTH, 2 * ATT_WIDTH, 3 * ATT_WIDTH,
             3 * ATT_WIDTH + 2 * MLSTM_WIDTH,
             3 * ATT_WIDTH + 3 * MLSTM_WIDTH,
             3 * ATT_WIDTH + 4 * MLSTM_WIDTH,
             3 * ATT_WIDTH + 4 * MLSTM_WIDTH + MLSTM_HEADS)

MEM_TOKENS = 256
XATT_HEADS = 4
XATT_HEAD_DIM = D_MODEL // XATT_HEADS

PEER_HEADS = 8
N_KEYS = 128
N_EXPERTS = N_KEYS * N_KEYS
PEER_TOPK = 16
PEER_KEY_DIM = 256
PEER_HALF = PEER_KEY_DIM // 2
PEER_TOKEN_BLOCK = 128

kernel_name = "hybrid_chunkattn_mlstm_peer_deepnorm"


def layer_norm(x, g, b):
    xf = x.astype(jnp.float32)
    mu = jnp.mean(xf, -1, keepdims=True)
    var = jnp.mean(jnp.square(xf - mu), -1, keepdims=True)
    return ((xf - mu) * lax.rsqrt(var + LN_EPS) * g.astype(jnp.float32) + b.astype(jnp.float32)).astype(x.dtype)


def headwise_norm(x, g):
    xf = x.astype(jnp.float32)
    mu = jnp.mean(xf, -1, keepdims=True)
    var = jnp.mean(jnp.square(xf - mu), -1, keepdims=True)
    return ((xf - mu) * lax.rsqrt(var + LN_EPS) * g.astype(jnp.float32)).astype(x.dtype)


def causal_depthwise_conv(x, w, b):
    c = x.shape[-1]
    y = lax.conv_general_dilated(x, w[:, None, :].astype(x.dtype), window_strides=(1,),
                                 padding=((CONV_WIDTH - 1, 0),),
                                 dimension_numbers=('NWC', 'WIO', 'NWC'),
                                 feature_group_count=c)
    return y + b.astype(x.dtype)


def chunked_rel_attention(q, k, v, rel_bias):
    B, S, H, Dh = q.shape
    n_chunks = S // CHUNK
    pad = ATT_LEFT_CHUNKS * CHUNK
    kp = jnp.pad(k, ((0, 0), (pad, 0), (0, 0), (0, 0)))
    vp = jnp.pad(v, ((0, 0), (pad, 0), (0, 0), (0, 0)))
    q_off = jnp.arange(CHUNK)[:, None]
    k_off = jnp.arange(ATT_BAND)[None, :] - pad
    rel = jnp.clip(k_off - q_off, -MAX_REL, MAX_REL) + MAX_REL
    bias = rel_bias[:, rel].astype(jnp.float32)
    scale = Dh ** -0.5

    def one_chunk(c):
        start = c * CHUNK
        qc = lax.dynamic_slice_in_dim(q, start, CHUNK, axis=1)
        kc = lax.dynamic_slice_in_dim(kp, start, ATT_BAND, axis=1)
        vc = lax.dynamic_slice_in_dim(vp, start, ATT_BAND, axis=1)
        s = jnp.einsum('bqhd,bkhd->bhqk', qc, kc, preferred_element_type=jnp.float32) * scale + bias
        valid = (start - pad + jnp.arange(ATT_BAND)) >= 0
        s = jnp.where(valid, s, -jnp.inf)
        p = jax.nn.softmax(s, axis=-1)
        return jnp.einsum('bhqk,bkhd->bqhd', p.astype(vc.dtype), vc)

    out = lax.map(one_chunk, jnp.arange(n_chunks))
    return jnp.moveaxis(out, 0, 1).reshape(B, S, H * Dh)


def mlstm_chunkwise(q, k, v, i_pre, f_pre):
    B, S, H, D = q.shape
    n_chunks = S // CHUNK
    f32 = jnp.float32

    def to_chunks(a):
        a = a.reshape(B, n_chunks, CHUNK, H, *a.shape[3:])
        return jnp.moveaxis(a, (1, 3), (0, 2))

    qs = to_chunks(q.astype(f32))
    ks = to_chunks(k.astype(f32) * (D ** -0.5))
    vs = to_chunks(v.astype(f32))
    log_i = to_chunks(i_pre.astype(f32))
    log_f = to_chunks(jax.nn.log_sigmoid(f_pre.astype(f32)))
    causal = jnp.tril(jnp.ones((CHUNK, CHUNK), dtype=bool))

    def step(carry, xs):
        C_prev, n_prev, m_prev = carry
        qc, kc, vc, ic, fc = xs
        b = jnp.cumsum(fc, axis=-1)
        log_d = b[..., :, None] - b[..., None, :] + ic[..., None, :]
        log_d = jnp.where(causal, log_d, -jnp.inf)
        inter = b + m_prev[..., None]
        m_t = jnp.maximum(inter, jnp.max(log_d, -1))
        d_mat = jnp.exp(log_d - m_t[..., None])
        w_inter = jnp.exp(inter - m_t)
        qk = jnp.einsum('bhtd,bhsd->bhts', qc, kc) * d_mat
        num = w_inter[..., None] * jnp.einsum('bhtd,bhde->bhte', qc, C_prev) + jnp.einsum('bhts,bhse->bhte', qk, vc)
        den = w_inter * jnp.einsum('bhtd,bhd->bht', qc, n_prev) + jnp.sum(qk, -1)
        h = num / jnp.maximum(jnp.abs(den), jnp.exp(-m_t))[..., None]
        b_last = b[..., -1]
        log_in = b_last[..., None] - b + ic
        m_new = jnp.maximum(b_last + m_prev, jnp.max(log_in, -1))
        w_prev = jnp.exp(b_last + m_prev - m_new)
        w_in = jnp.exp(log_in - m_new[..., None])
        C_new = w_prev[..., None, None] * C_prev + jnp.einsum('bhs,bhsd,bhse->bhde', w_in, kc, vc)
        n_new = w_prev[..., None] * n_prev + jnp.einsum('bhs,bhsd->bhd', w_in, kc)
        return (C_new, n_new, m_new), h

    init = (jnp.zeros((B, H, D, D), f32), jnp.zeros((B, H, D), f32), jnp.zeros((B, H), f32))
    _, h = lax.scan(step, init, (qs, ks, vs, log_i, log_f))
    h = jnp.moveaxis(h, (0, 2), (1, 3)).reshape(B, S, H, D)
    return h.astype(q.dtype)


def hybrid_mixer(h, w_in, conv_w, conv_b, i_bias, f_bias, norm_g, rel_bias, w_out):
    B, S, _ = h.shape
    proj = h @ w_in
    a_q, a_k, a_v, m_qk, m_v, m_o, m_i, m_f = jnp.split(proj, IN_SPLITS, axis=-1)
    hs = (B, S, ATT_HEADS, ATT_HEAD_DIM)
    att = chunked_rel_attention(a_q.reshape(hs), a_k.reshape(hs), a_v.reshape(hs), rel_bias)
    qk = jax.nn.silu(causal_depthwise_conv(m_qk, conv_w, conv_b))
    m_q, m_k = jnp.split(qk, 2, axis=-1)
    ms = (B, S, MLSTM_HEADS, MLSTM_HEAD_DIM)
    hm = mlstm_chunkwise(m_q.reshape(ms), m_k.reshape(ms), m_v.reshape(ms),
                         m_i + i_bias.astype(m_i.dtype), m_f + f_bias.astype(m_f.dtype))
    hm = headwise_norm(hm, norm_g.reshape(MLSTM_HEADS, MLSTM_HEAD_DIM)).reshape(B, S, MLSTM_WIDTH)
    hm = jax.nn.sigmoid(m_o) * hm
    return jnp.concatenate([att, hm], axis=-1) @ w_out


def memory_cross_attention(h, mem, w_q, w_kv, w_o):
    B, S, _ = h.shape
    M = mem.shape[1]
    q = (h @ w_q).reshape(B, S, XATT_HEADS, XATT_HEAD_DIM)
    k, v = jnp.split(mem @ w_kv, 2, axis=-1)
    k = k.reshape(B, M, XATT_HEADS, XATT_HEAD_DIM)
    v = v.reshape(B, M, XATT_HEADS, XATT_HEAD_DIM)
    s = jnp.einsum('bshd,bmhd->bhsm', q, k, preferred_element_type=jnp.float32) * (XATT_HEAD_DIM ** -0.5)
    p = jax.nn.softmax(s, axis=-1)
    o = jnp.einsum('bhsm,bmhd->bshd', p.astype(v.dtype), v).reshape(B, S, D_MODEL)
    return o @ w_o


def peer_ffn(x, w_query, sub_keys, expert_u, expert_v):
    B, S, D = x.shape
    xt = x.reshape(-1, PEER_TOKEN_BLOCK, D)

    def block(xb):
        T = xb.shape[0]
        q = (xb @ w_query).reshape(T, PEER_HEADS, 2, PEER_HALF)
        s = jnp.einsum('thpc,pnc->thpn', q, sub_keys, preferred_element_type=jnp.float32)
        top_s, top_i = lax.top_k(s, PEER_TOPK)
        cand = top_s[:, :, 0, :, None] + top_s[:, :, 1, None, :]
        best_s, best_j = lax.top_k(cand.reshape(T, PEER_HEADS, PEER_TOPK * PEER_TOPK), PEER_TOPK)
        i1 = jnp.take_along_axis(top_i[:, :, 0], best_j // PEER_TOPK, axis=-1)
        i2 = jnp.take_along_axis(top_i[:, :, 1], best_j % PEER_TOPK, axis=-1)
        idx = i1 * N_KEYS + i2
        g = jax.nn.softmax(best_s, axis=-1)
        u = expert_u[idx]
        a = jax.nn.gelu(jnp.einsum('thkd,td->thk', u, xb, preferred_element_type=jnp.float32), approximate=False)
        vv = expert_v[idx]
        out = jnp.einsum('thk,thkd->td', (g * a).astype(vv.dtype), vv)
        return out.astype(xb.dtype)

    return lax.map(block, xt).reshape(B, S, D)


def setup_inputs(seed: int = 0) -> dict:
    key = jax.random.key(seed)
    ks = jax.random.split(key, 26)

    def nrm(k, shape, scale):
        return jax.random.normal(k, shape, jnp.float32) * scale

    L, D = DEPTH, D_MODEL
    beta = DEEPNORM_BETA
    return {
        "x": nrm(ks[0], (BATCH, SEQ, D), 1.0),
        "mem": nrm(ks[1], (BATCH, MEM_TOKENS, D), 1.0),
        "ln_in_g": 1.0 + nrm(ks[2], (D,), 0.02),
        "ln_in_b": nrm(ks[3], (D,), 0.02),
        "w_in": nrm(ks[4], (L, D, IN_PROJ_WIDTH), D ** -0.5),
        "conv_w": nrm(ks[5], (L, CONV_WIDTH, 2 * MLSTM_WIDTH), CONV_WIDTH ** -0.5),
        "conv_b": nrm(ks[6], (L, 2 * MLSTM_WIDTH), 0.02),
        "mlstm_i_bias": nrm(ks[7], (L, MLSTM_HEADS), 0.1),
        "mlstm_f_bias": jnp.linspace(3.0, 6.0, MLSTM_HEADS, dtype=jnp.float32)[None, :] + nrm(ks[8], (L, MLSTM_HEADS), 0.1),
        "mlstm_norm_g": 1.0 + nrm(ks[9], (L, MLSTM_WIDTH), 0.02),
        "rel_bias": nrm(ks[10], (L, ATT_HEADS, N_REL), 0.5),
        "w_out": nrm(ks[11], (L, D, D), beta * D ** -0.5),
        "ln1_g": 1.0 + nrm(ks[12], (L, D), 0.02),
        "ln1_b": nrm(ks[13], (L, D), 0.02),
        "xattn_w_q": nrm(ks[14], (L, D, D), D ** -0.5),
        "xattn_w_kv": nrm(ks[15], (L, D, 2 * D), D ** -0.5),
        "xattn_w_o": nrm(ks[16], (L, D, D), beta * D ** -0.5),
        "ln2_g": 1.0 + nrm(ks[17], (L, D), 0.02),
        "ln2_b": nrm(ks[18], (L, D), 0.02),
        "peer_w_query": nrm(ks[19], (L, D, PEER_HEADS * PEER_KEY_DIM), D ** -0.5),
        "peer_sub_keys": nrm(ks[20], (L, 2, N_KEYS, PEER_HALF), PEER_HALF ** -0.5),
        "peer_u": nrm(ks[21], (L, N_EXPERTS, D), D ** -0.5),
        "peer_v": nrm(ks[22], (L, N_EXPERTS, D), beta * PEER_HEADS ** -0.5),
        "ln3_g": 1.0 + nrm(ks[23], (L, D), 0.02),
        "ln3_b": nrm(ks[24], (L, D), 0.02),
    }


def reference(x, mem, ln_in_g, ln_in_b, w_in, conv_w, conv_b, mlstm_i_bias, mlstm_f_bias,
              mlstm_norm_g, rel_bias, w_out, ln1_g, ln1_b, xattn_w_q, xattn_w_kv, xattn_w_o,
              ln2_g, ln2_b, peer_w_query, peer_sub_keys, peer_u, peer_v, ln3_g, ln3_b):
    a = DEEPNORM_ALPHA
    h = layer_norm(x, ln_in_g, ln_in_b)
    for l in range(DEPTH):
        y = hybrid_mixer(h, w_in[l], conv_w[l], conv_b[l], mlstm_i_bias[l], mlstm_f_bias[l],
                         mlstm_norm_g[l], rel_bias[l], w_out[l])
        h = layer_norm(a * h + y, ln1_g[l], ln1_b[l])
        y = memory_cross_attention(h, mem, xattn_w_q[l], xattn_w_kv[l], xattn_w_o[l])
        h = layer_norm(a * h + y, ln2_g[l], ln2_b[l])
        y = peer_ffn(h, peer_w_query[l], peer_sub_keys[l], peer_u[l], peer_v[l])
        h = layer_norm(a * h + y, ln3_g[l], ln3_b[l])
    return h
```

```python
import functools
import math

import jax
import jax.numpy as jnp
from jax import lax
from jax.experimental import pallas as pl
from jax.experimental.pallas import tpu as pltpu

F32, BF16, I32, U32 = jnp.float32, jnp.bfloat16, jnp.int32, jnp.uint32

DEPTH = 1
LN_EPS = 1e-5
DEEPNORM_ALPHA = (2.0 * DEPTH) ** 0.25
CHUNK = 64
ATT_HEADS, ATT_HEAD_DIM, ATT_LEFT_CHUNKS, MAX_REL = 8, 64, 8, 128
ATT_WIDTH = ATT_HEADS * ATT_HEAD_DIM
MLSTM_HEADS, MLSTM_HEAD_DIM, CONV_WIDTH = 4, 128, 4
MLSTM_WIDTH = MLSTM_HEADS * MLSTM_HEAD_DIM
XATT_HEADS = 4
PEER_HEADS, N_KEYS, PEER_TOPK = 8, 128, 16
PEER_PICKS = PEER_HEADS * PEER_TOPK

LANES = 128
SUBLANES = 8
VMEM_LIMIT_BYTES = 56 * 1024 * 1024

NEG_BIG = -1e30

ROW_BLOCK = 512
ATT_BLOCK = 256
ATT_KEY_BLOCKS = 3
MLSTM_BLOCK = 256
TOPK_BLOCK = 256
PEER_BLOCK = 128


def _params(*semantics):
    return pltpu.CompilerParams(dimension_semantics=semantics, vmem_limit_bytes=VMEM_LIMIT_BYTES)


def _layer_norm(x, g, b):
    mu = jnp.mean(x, -1, keepdims=True)
    xc = x - mu
    var = jnp.mean(xc * xc, -1, keepdims=True)
    return xc * lax.rsqrt(var + LN_EPS) * g + b


def _split2(x):
    hi = x.astype(BF16)
    lo = (x - hi.astype(F32)).astype(BF16)
    return hi, lo


def _split3(x):
    hi = x.astype(BF16)
    r = x - hi.astype(F32)
    mid = r.astype(BF16)
    lo = (r - mid.astype(F32)).astype(BF16)
    return hi, mid, lo


def _dot(a, b):
    return jnp.dot(a, b, preferred_element_type=F32)


def _dot_nt(a, b):
    return lax.dot_general(a, b, (((1,), (1,)), ((), ())), preferred_element_type=F32)


def _dot_tn(a, b):
    return lax.dot_general(a, b, (((0,), (0,)), ((), ())), preferred_element_type=F32)


def _full(shape):
    return pl.BlockSpec(shape, lambda *_: (0,) * len(shape))


def _ln_inproj_kernel(x_ref, g_ref, b_ref, wqkv_ref, wmqk_ref, wmv_ref, wmo_ref, wghi_ref, wglo_ref,
                      h_ref, qkv_ref, mqk_ref, mv_ref, mo_ref, gate_ref):
    h = _layer_norm(x_ref[...], g_ref[...], b_ref[...])
    h_ref[...] = h
    hb, hlo = _split2(h)
    qkv_ref[...] = _dot(hb, wqkv_ref[...]).astype(BF16)
    mqk_ref[...] = _dot(hb, wmqk_ref[...]).astype(BF16)
    mv_ref[...] = _dot(hb, wmv_ref[...]).astype(BF16)
    mo_ref[...] = _dot(hb, wmo_ref[...]).astype(BF16)
    gate_ref[...] = _dot(hb, wghi_ref[...]) + _dot(hlo, wghi_ref[...]) + _dot(hb, wglo_ref[...])


def _ln_inproj(x2, g, b, w_in):
    n, d = x2.shape
    a3 = 3 * ATT_WIDTH
    wqkv = w_in[:, :a3].astype(BF16)
    wmqk = w_in[:, a3:a3 + 2 * MLSTM_WIDTH].astype(BF16)
    wmv = w_in[:, a3 + 2 * MLSTM_WIDTH:a3 + 3 * MLSTM_WIDTH].astype(BF16)
    wmo = w_in[:, a3 + 3 * MLSTM_WIDTH:a3 + 4 * MLSTM_WIDTH].astype(BF16)
    wg = jnp.pad(w_in[:, a3 + 4 * MLSTM_WIDTH:], ((0, 0), (0, LANES - 2 * MLSTM_HEADS)))
    wghi = wg.astype(BF16)
    wglo = (wg - wghi.astype(F32)).astype(BF16)
    rows = lambda w: pl.BlockSpec((ROW_BLOCK, w), lambda i: (i, 0))
    return pl.pallas_call(
        _ln_inproj_kernel,
        grid=(n // ROW_BLOCK,),
        in_specs=[rows(d), _full((1, d)), _full((1, d)), _full(wqkv.shape), _full(wmqk.shape),
                  _full(wmv.shape), _full(wmo.shape), _full(wghi.shape), _full(wglo.shape)],
        out_specs=[rows(d), rows(a3), rows(2 * MLSTM_WIDTH), rows(MLSTM_WIDTH), rows(MLSTM_WIDTH), rows(LANES)],
        out_shape=[jax.ShapeDtypeStruct((n, d), F32), jax.ShapeDtypeStruct((n, a3), BF16),
                   jax.ShapeDtypeStruct((n, 2 * MLSTM_WIDTH), BF16), jax.ShapeDtypeStruct((n, MLSTM_WIDTH), BF16),
                   jax.ShapeDtypeStruct((n, MLSTM_WIDTH), BF16), jax.ShapeDtypeStruct((n, LANES), F32)],
        compiler_params=_params("arbitrary"),
        name="ln_inproj",
    )(x2, g.reshape(1, d), b.reshape(1, d), wqkv, wmqk, wmv, wmo, wghi, wglo)


def _attn_kernel(q_ref, k0_ref, k1_ref, k2_ref, v0_ref, v1_ref, v2_ref, bias_ref, o_ref):
    i = pl.program_id(1)
    nkeys = ATT_KEY_BLOCKS * ATT_BLOCK
    q = q_ref[0] * (ATT_HEAD_DIM ** -0.5)
    kcat = jnp.concatenate([k0_ref[0], k1_ref[0], k2_ref[0]], axis=0)
    vcat = jnp.concatenate([v0_ref[0], v1_ref[0], v2_ref[0]], axis=0)
    col = lax.broadcasted_iota(I32, (1, nkeys), 1)
    in_seq = col >= (ATT_KEY_BLOCKS - 1 - i) * ATT_BLOCK
    lane = lax.broadcasted_iota(I32, (1, LANES), 1)
    low = lane < ATT_HEAD_DIM
    outs = []
    for pair in range(ATT_HEADS // 2):
        sl = slice(pair * LANES, (pair + 1) * LANES)
        qp, kp, vp = q[:, sl], kcat[:, sl], vcat[:, sl]
        halves = []
        for half in range(2):
            keep = low if half == 0 else jnp.logical_not(low)
            qh = jnp.where(keep, qp, jnp.zeros_like(qp))
            s = _dot_nt(qh, kp) + bias_ref[2 * pair + half]
            s = jnp.where(in_seq, s, NEG_BIG)
            m = jnp.max(s, -1, keepdims=True)
            p = jnp.exp(s - m)
            l = jnp.sum(p, -1, keepdims=True)
            halves.append(_dot(p.astype(BF16), vp) / l)
        outs.append(jnp.where(low, halves[0], halves[1]))
    o_ref[0] = jnp.concatenate(outs, axis=-1).astype(BF16)


def _attn_bias(rel_bias):
    r = jnp.arange(ATT_BLOCK)[:, None]
    j = jnp.arange(ATT_KEY_BLOCKS * ATT_BLOCK)[None, :]
    rel = (j - (ATT_KEY_BLOCKS - 1) * ATT_BLOCK) - r
    qc = r // CHUNK
    kc = j // CHUNK
    allowed = (kc >= qc) & (kc <= qc + ATT_LEFT_CHUNKS)
    tab = rel_bias[:, jnp.clip(rel, -MAX_REL, MAX_REL) + MAX_REL].astype(F32)
    return jnp.where(allowed[None], tab, NEG_BIG)


def _attention(qkv, rel_bias, batch, seq):
    assert ATT_LEFT_CHUNKS * CHUNK == (ATT_KEY_BLOCKS - 1) * ATT_BLOCK
    qkv3 = qkv.reshape(batch, seq, 3 * ATT_WIDTH)
    bias = _attn_bias(rel_bias)
    blk = (1, ATT_BLOCK, ATT_WIDTH)

    def kv_spec(col, j):
        return pl.BlockSpec(blk, lambda b, i: (b, jnp.maximum(i - (ATT_KEY_BLOCKS - 1) + j, 0), col))

    return pl.pallas_call(
        _attn_kernel,
        grid=(batch, seq // ATT_BLOCK),
        in_specs=[pl.BlockSpec(blk, lambda b, i: (b, i, 0))]
        + [kv_spec(1, j) for j in range(ATT_KEY_BLOCKS)] + [kv_spec(2, j) for j in range(ATT_KEY_BLOCKS)]
        + [_full(bias.shape)],
        out_specs=pl.BlockSpec(blk, lambda b, i: (b, i, 0)),
        out_shape=jax.ShapeDtypeStruct((batch, seq, ATT_WIDTH), BF16),
        compiler_params=_params("arbitrary", "arbitrary"),
        name="attn",
    )(qkv3, qkv3, qkv3, qkv3, qkv3, qkv3, qkv3, bias)


def _mlstm_kernel(mqk_ref, mv_ref, mo_ref, gate_ref, convw_ref, convb_ref, gbias_ref, ng_ref, out_ref,
                  xpad, q_scr, k_scr, gate_scr, logf_scr, c_st, n_st, m_st):
    j = pl.program_id(1)
    rows = MLSTM_BLOCK
    d = MLSTM_HEAD_DIM

    @pl.when(j == 0)
    def _():
        xpad[0:SUBLANES, :] = jnp.zeros((SUBLANES, 2 * MLSTM_WIDTH), F32)
        c_st[...] = jnp.zeros_like(c_st)
        n_st[...] = jnp.zeros_like(n_st)
        m_st[...] = jnp.zeros_like(m_st)

    xpad[SUBLANES:SUBLANES + rows, :] = mqk_ref[0].astype(F32)
    acc = jnp.broadcast_to(convb_ref[...], (rows, 2 * MLSTM_WIDTH))
    for t in range(CONV_WIDTH):
        acc = acc + convw_ref[t:t + 1, :] * xpad[pl.ds(SUBLANES - (CONV_WIDTH - 1) + t, rows), :]
    xpad[0:SUBLANES, :] = xpad[rows:rows + SUBLANES, :]
    qk = acc * jax.nn.sigmoid(acc)
    q_scr[...] = qk[:, :MLSTM_WIDTH].astype(BF16)
    k_scr[...] = qk[:, MLSTM_WIDTH:] * (d ** -0.5)

    gates = gate_ref[0] + gbias_ref[...]
    gate_scr[...] = gates
    logf_scr[...] = jax.nn.log_sigmoid(gates)

    ri = lax.broadcasted_iota(I32, (CHUNK, CHUNK), 0)
    ci = lax.broadcasted_iota(I32, (CHUNK, CHUNK), 1)
    causal = ci <= ri
    tri = causal.astype(BF16)

    def chunk_body(c, carry):
        r0 = pl.multiple_of(c * CHUNK, CHUNK)
        g = gate_scr[pl.ds(r0, CHUNK), :]
        lf = logf_scr[pl.ds(r0, CHUNK), :]
        l1, l2, l3 = _split3(lf)
        bcol = _dot(tri, l1) + _dot(tri, l2) + _dot(tri, l3)
        g_t = g.T
        b_t = bcol.T
        for h in range(MLSTM_HEADS):
            hs = slice(h * d, (h + 1) * d)
            ic = g[:, h:h + 1]
            bc = bcol[:, MLSTM_HEADS + h:MLSTM_HEADS + h + 1]
            ir = g_t[h:h + 1, :]
            br = b_t[MLSTM_HEADS + h:MLSTM_HEADS + h + 1, :]
            m_prev = m_st[h:h + 1, 0:1]
            n_prev = n_st[h:h + 1, :]
            c_prev = c_st[h]
            qh = q_scr[pl.ds(r0, CHUNK), hs]
            kh = k_scr[pl.ds(r0, CHUNK), hs]
            vh = mv_ref[0, pl.ds(r0, CHUNK), hs]

            log_d = jnp.where(causal, bc - br + ir, NEG_BIG)
            inter = bc + m_prev
            m_t = jnp.maximum(inter, jnp.max(log_d, -1, keepdims=True))
            d_mat = jnp.exp(log_d - m_t)
            w_inter = jnp.exp(inter - m_t)
            qk_d = _dot_nt(qh, kh.astype(BF16)) * d_mat
            num = w_inter * _dot(qh, c_prev.astype(BF16)) + _dot(qk_d.astype(BF16), vh)
            den = (w_inter * jnp.sum(qh.astype(F32) * n_prev, -1, keepdims=True)
                   + jnp.sum(qk_d, -1, keepdims=True))
            hh = num / jnp.maximum(jnp.abs(den), jnp.exp(-m_t))

            b_last = bc[CHUNK - 1:CHUNK, :]
            log_in = b_last - bc + ic
            m_new = jnp.maximum(b_last + m_prev, jnp.max(log_in, 0, keepdims=True))
            w_prev = jnp.exp(b_last + m_prev - m_new)
            kw = kh * jnp.exp(log_in - m_new)
            c_st[h] = w_prev * c_prev + _dot_tn(kw.astype(BF16), vh)
            n_st[h:h + 1, :] = w_prev * n_prev + jnp.sum(kw, 0, keepdims=True)
            m_st[h:h + 1, :] = jnp.broadcast_to(m_new, (1, LANES))

            mu = jnp.mean(hh, -1, keepdims=True)
            hc = hh - mu
            var = jnp.mean(hc * hc, -1, keepdims=True)
            hn = hc * lax.rsqrt(var + LN_EPS) * ng_ref[:, hs]
            og = jax.nn.sigmoid(mo_ref[0, pl.ds(r0, CHUNK), hs].astype(F32))
            out_ref[0, pl.ds(r0, CHUNK), hs] = (og * hn).astype(BF16)
        return carry

    lax.fori_loop(0, rows // CHUNK, chunk_body, 0)


def _mlstm(mqk, mv, mo, gates, conv_w, conv_b, i_bias, f_bias, norm_g, batch, seq):
    w2 = 2 * MLSTM_WIDTH
    gbias = jnp.pad(jnp.concatenate([i_bias, f_bias]).astype(F32), (0, LANES - 2 * MLSTM_HEADS)).reshape(1, LANES)
    blk = lambda w: pl.BlockSpec((1, MLSTM_BLOCK, w), lambda b, i: (b, i, 0))
    return pl.pallas_call(
        _mlstm_kernel,
        grid=(batch, seq // MLSTM_BLOCK),
        in_specs=[blk(w2), blk(MLSTM_WIDTH), blk(MLSTM_WIDTH), blk(LANES),
                  _full((CONV_WIDTH, w2)), _full((1, w2)), _full((1, LANES)), _full((1, MLSTM_WIDTH))],
        out_specs=blk(MLSTM_WIDTH),
        out_shape=jax.ShapeDtypeStruct((batch, seq, MLSTM_WIDTH), BF16),
        scratch_shapes=[pltpu.VMEM((MLSTM_BLOCK + SUBLANES, w2), F32),
                        pltpu.VMEM((MLSTM_BLOCK, MLSTM_WIDTH), BF16),
                        pltpu.VMEM((MLSTM_BLOCK, MLSTM_WIDTH), F32),
                        pltpu.VMEM((MLSTM_BLOCK, LANES), F32),
                        pltpu.VMEM((MLSTM_BLOCK, LANES), F32),
                        pltpu.VMEM((MLSTM_HEADS, MLSTM_HEAD_DIM, MLSTM_HEAD_DIM), F32),
                        pltpu.VMEM((SUBLANES, MLSTM_HEAD_DIM), F32),
                        pltpu.VMEM((SUBLANES, LANES), F32)],
        compiler_params=_params("arbitrary", "arbitrary"),
        name="mlstm",
    )(mqk.reshape(batch, seq, w2), mv.reshape(batch, seq, MLSTM_WIDTH), mo.reshape(batch, seq, MLSTM_WIDTH),
      gates.reshape(batch, seq, LANES), conv_w.astype(F32), conv_b.reshape(1, w2).astype(F32), gbias,
      norm_g.reshape(1, MLSTM_WIDTH).astype(F32))


def _outproj_kernel(att_ref, hm_ref, h_ref, wa_ref, wm_ref, g_ref, b_ref, o_ref):
    y = _dot(att_ref[...], wa_ref[...]) + _dot(hm_ref[...], wm_ref[...])
    o_ref[...] = _layer_norm(DEEPNORM_ALPHA * h_ref[...] + y, g_ref[...], b_ref[...])


def _outproj(att, hm, h, w_out, g, b):
    n, d = h.shape
    wa = w_out[:ATT_WIDTH].astype(BF16)
    wm = w_out[ATT_WIDTH:].astype(BF16)
    rows = lambda w: pl.BlockSpec((ROW_BLOCK, w), lambda i: (i, 0))
    return pl.pallas_call(
        _outproj_kernel,
        grid=(n // ROW_BLOCK,),
        in_specs=[rows(ATT_WIDTH), rows(MLSTM_WIDTH), rows(d), _full(wa.shape), _full(wm.shape),
                  _full((1, d)), _full((1, d))],
        out_specs=rows(d),
        out_shape=jax.ShapeDtypeStruct((n, d), F32),
        compiler_params=_params("arbitrary"),
        name="outproj",
    )(att, hm, h, wa, wm, g.reshape(1, d), b.reshape(1, d))


def _kvproj_kernel(mem_ref, w_ref, k_ref, v_ref):
    kv = _dot(mem_ref[...].astype(BF16), w_ref[...])
    d = k_ref.shape[-1]
    k_ref[...] = kv[:, :d].astype(BF16)
    v_ref[...] = kv[:, d:].astype(BF16)


def _kvproj(mem2, w_kv):
    n, d = mem2.shape
    w = w_kv.astype(BF16)
    blk = min(ROW_BLOCK, n)
    rows = pl.BlockSpec((blk, d), lambda i: (i, 0))
    return pl.pallas_call(
        _kvproj_kernel,
        grid=(n // blk,),
        in_specs=[rows, _full(w.shape)],
        out_specs=[rows, rows],
        out_shape=[jax.ShapeDtypeStruct((n, d), BF16)] * 2,
        compiler_params=_params("arbitrary"),
        name="kvproj",
    )(mem2, w)


def _xattn_kernel(h_ref, k_ref, v_ref, wq_ref, wo_ref, g_ref, b_ref, o_ref):
    h = h_ref[0]
    d = h.shape[-1]
    dh = d // XATT_HEADS
    q = (_dot(h.astype(BF16), wq_ref[...]) * (dh ** -0.5)).astype(BF16)
    outs = []
    for hd in range(XATT_HEADS):
        sl = slice(hd * dh, (hd + 1) * dh)
        s = _dot_nt(q[:, sl], k_ref[0, :, sl])
        m = jnp.max(s, -1, keepdims=True)
        p = jnp.exp(s - m)
        l = jnp.sum(p, -1, keepdims=True)
        outs.append((_dot(p.astype(BF16), v_ref[0, :, sl]) / l).astype(BF16))
    y = _dot(jnp.concatenate(outs, axis=-1), wo_ref[...])
    o_ref[0] = _layer_norm(DEEPNORM_ALPHA * h + y, g_ref[...], b_ref[...])


def _xattn(h3, k3, v3, w_q, w_o, g, b):
    batch, seq, d = h3.shape
    m = k3.shape[1]
    wq = w_q.astype(BF16)
    wo = w_o.astype(BF16)
    blk = pl.BlockSpec((1, ROW_BLOCK, d), lambda bb, i: (bb, i, 0))
    mem = pl.BlockSpec((1, m, d), lambda bb, i: (bb, 0, 0))
    return pl.pallas_call(
        _xattn_kernel,
        grid=(batch, seq // ROW_BLOCK),
        in_specs=[blk, mem, mem, _full(wq.shape), _full(wo.shape), _full((1, d)), _full((1, d))],
        out_specs=blk,
        out_shape=jax.ShapeDtypeStruct((batch, seq, d), F32),
        compiler_params=_params("arbitrary", "arbitrary"),
        name="xattn",
    )(h3, k3, v3, wq, wo, g.reshape(1, d), b.reshape(1, d))


def _topk_rows(s, k, big):
    iota = lax.broadcasted_iota(I32, s.shape, 0)
    vals, idxs = [], []
    for _ in range(k):
        m = jnp.max(s, axis=0, keepdims=True)
        sel = jnp.min(jnp.where(s == m, iota, big), axis=0, keepdims=True)
        vals.append(m)
        idxs.append(sel)
        s = jnp.where(iota == sel, -jnp.inf, s)
    return vals, idxs


def _stack_rows(rows_list):
    k = len(rows_list)
    t = rows_list[0].shape[-1]
    iota = lax.broadcasted_iota(I32, (k, t), 0)
    out = jnp.broadcast_to(rows_list[0], (k, t))
    for r in range(1, k):
        out = jnp.where(iota == r, jnp.broadcast_to(rows_list[r], (k, t)), out)
    return out


def _peer_topk_kernel(h_ref, wq_ref, khi_ref, klo_ref, idx_ref, g_ref):
    t = h_ref.shape[0]
    kk = PEER_TOPK
    q = _dot(h_ref[...].astype(BF16), wq_ref[...])
    half8 = kk // 2
    sub = lax.broadcasted_iota(I32, (half8, t), 0)
    for hh in range(PEER_HEADS):
        tops = []
        for p in range(2):
            c0 = (hh * 2 + p) * N_KEYS
            qhi, qlo = _split2(q[:, c0:c0 + N_KEYS])
            s = _dot_nt(khi_ref[p], qhi) + _dot_nt(khi_ref[p], qlo) + _dot_nt(klo_ref[p], qhi)
            vals, idxs = _topk_rows(s, kk, N_KEYS)
            tops.append((_stack_rows(vals), _stack_rows(idxs)))
        (s0, i0), (s1, i1) = tops
        e0 = i0 * N_KEYS
        cand, cexp, cflat = [], [], []
        for b in range(half8):
            cand.append(s0[:half8] + s1[b:b + 1])
            cexp.append(e0[:half8] + i1[b:b + 1])
            cflat.append(sub * kk + b)
        cand.append(s0[half8:] + s1[0:1])
        cexp.append(e0[half8:] + i1[0:1])
        cflat.append((sub + half8) * kk)
        cand.append(s0[0:1] + s1[half8:])
        cexp.append(e0[0:1] + i1[half8:])
        cflat.append(sub + half8)
        cand = jnp.concatenate(cand, axis=0)
        cexp = jnp.concatenate(cexp, axis=0)
        cflat = jnp.concatenate(cflat, axis=0)
        best_s, best_e = [], []
        for _ in range(kk):
            m = jnp.max(cand, axis=0, keepdims=True)
            jsel = jnp.min(jnp.where(cand == m, cflat, kk * kk), axis=0, keepdims=True)
            hit = cflat == jsel
            best_s.append(m)
            best_e.append(jnp.max(jnp.where(hit, cexp, 0), axis=0, keepdims=True))
            cand = jnp.where(hit, -jnp.inf, cand)
        bs = _stack_rows(best_s)
        ex = jnp.exp(bs - bs[0:1])
        g_ref[hh * kk:(hh + 1) * kk, :] = ex / jnp.sum(ex, axis=0, keepdims=True)
        idx_ref[hh * kk:(hh + 1) * kk, :] = _stack_rows(best_e)


def _peer_topk(h2, w_query, sub_keys):
    n, d = h2.shape
    wq = w_query.astype(BF16)
    khi = sub_keys.astype(BF16)
    klo = (sub_keys - khi.astype(F32)).astype(BF16)
    cols = pl.BlockSpec((PEER_PICKS, TOPK_BLOCK), lambda i: (0, i))
    return pl.pallas_call(
        _peer_topk_kernel,
        grid=(n // TOPK_BLOCK,),
        in_specs=[pl.BlockSpec((TOPK_BLOCK, d), lambda i: (i, 0)), _full(wq.shape), _full(khi.shape), _full(klo.shape)],
        out_specs=[cols, cols],
        out_shape=[jax.ShapeDtypeStruct((PEER_PICKS, n), I32), jax.ShapeDtypeStruct((PEER_PICKS, n), F32)],
        compiler_params=_params("arbitrary"),
        name="peer_topk",
    )(h2, wq, khi, klo)


def _pack_table(tab):
    e, d = tab.shape
    bits = lax.bitcast_convert_type(tab.astype(BF16), jnp.uint16).astype(U32)
    packed = (bits[:e // 2] << 16) | bits[e // 2:]
    return packed.reshape(e // 2, d // LANES, LANES)


def _fetch_row(tab_ref, e, half):
    w = tab_ref[e & (half - 1)]
    sh = ((e >> (half.bit_length() - 1)) << 4).astype(U32)
    return pltpu.bitcast((w << sh) & jnp.uint32(0xFFFF0000), F32)


def _peer_u_kernel(idx_ref, x_ref, g_ref, tab_ref, mask_ref, coef_ref, a_scr):
    half = tab_ref.shape[0]
    ones = jnp.ones((LANES, LANES), BF16)

    def token(t, carry):
        x = x_ref[t]
        acc = jnp.zeros((LANES, LANES), F32)
        for hh in range(PEER_HEADS):
            prods = [_fetch_row(tab_ref, idx_ref[hh * PEER_TOPK + k, t], half) * x for k in range(PEER_TOPK)]
            r = _dot(jnp.concatenate(prods, axis=0).astype(BF16), ones)
            acc = acc + r * mask_ref[hh]
        a_scr[pl.ds(t, 1), :] = jnp.sum(acc, axis=0, keepdims=True)
        return carry

    lax.fori_loop(0, x_ref.shape[0], token, 0)
    a_t = a_scr[...].T
    gelu = 0.5 * a_t * (1.0 + lax.erf(a_t * math.sqrt(0.5)))
    coef_ref[...] = g_ref[...] * gelu


def _peer_v_kernel(idx_ref, coef_ref, x_ref, tab_ref, g_ref, b_ref, o_ref, y_scr):
    half = tab_ref.shape[0]

    def token(t, carry):
        acc = jnp.zeros((SUBLANES, LANES), F32)
        for k in range(PEER_PICKS):
            acc = acc + coef_ref[k, t] * _fetch_row(tab_ref, idx_ref[k, t], half)
        y_scr[t] = acc
        return carry

    lax.fori_loop(0, x_ref.shape[0], token, 0)
    z = DEEPNORM_ALPHA * x_ref[...] + y_scr[...]
    cnt = z.shape[1] * z.shape[2]
    mu = jnp.sum(jnp.sum(z, axis=2, keepdims=True), axis=1, keepdims=True) / cnt
    zc = z - mu
    var = jnp.sum(jnp.sum(zc * zc, axis=2, keepdims=True), axis=1, keepdims=True) / cnt
    o_ref[...] = zc * lax.rsqrt(var + LN_EPS) * g_ref[...] + b_ref[...]


def _peer_masks():
    r = jnp.arange(LANES)[:, None]
    j = jnp.arange(LANES)[None, :]
    return jnp.stack([(j == hh * PEER_TOPK + r // SUBLANES) for hh in range(PEER_HEADS)]).astype(F32)


def _peer_ffn(h2, idx_t, g_t, expert_u, expert_v, ln_g, ln_b):
    n, d = h2.shape
    rows8 = d // LANES
    assert rows8 == SUBLANES and PEER_PICKS == LANES
    x3 = h2.reshape(n, rows8, LANES)
    u_pk = _pack_table(expert_u)
    v_pk = _pack_table(expert_v)
    smem_cols = pl.BlockSpec((PEER_PICKS, PEER_BLOCK), lambda i: (0, i), memory_space=pltpu.SMEM)
    vmem_cols = pl.BlockSpec((PEER_PICKS, PEER_BLOCK), lambda i: (0, i))
    tok = pl.BlockSpec((PEER_BLOCK, rows8, LANES), lambda i: (i, 0, 0))
    table = pl.BlockSpec(u_pk.shape, lambda i: (0, 0, 0), pipeline_mode=pl.Buffered(1))
    masks = _peer_masks()
    coef_t = pl.pallas_call(
        _peer_u_kernel,
        grid=(n // PEER_BLOCK,),
        in_specs=[smem_cols, tok, vmem_cols, table, _full(masks.shape)],
        out_specs=vmem_cols,
        out_shape=jax.ShapeDtypeStruct((PEER_PICKS, n), F32),
        scratch_shapes=[pltpu.VMEM((PEER_BLOCK, PEER_PICKS), F32)],
        compiler_params=_params("arbitrary"),
        name="peer_u",
    )(idx_t, x3, g_t, u_pk, masks)
    out = pl.pallas_call(
        _peer_v_kernel,
        grid=(n // PEER_BLOCK,),
        in_specs=[smem_cols, smem_cols, tok, table, _full((rows8, LANES)), _full((rows8, LANES))],
        out_specs=tok,
        out_shape=jax.ShapeDtypeStruct((n, rows8, LANES), F32),
        scratch_shapes=[pltpu.VMEM((PEER_BLOCK, rows8, LANES), F32)],
        compiler_params=_params("arbitrary"),
        name="peer_v",
    )(idx_t, coef_t, x3, v_pk, ln_g.reshape(rows8, LANES), ln_b.reshape(rows8, LANES))
    return out.reshape(n, d)


def kernel(x, mem, ln_in_g, ln_in_b, w_in, conv_w, conv_b, mlstm_i_bias, mlstm_f_bias, mlstm_norm_g, rel_bias, w_out, ln1_g, ln1_b, xattn_w_q, xattn_w_kv, xattn_w_o, ln2_g, ln2_b, peer_w_query, peer_sub_keys, peer_u, peer_v, ln3_g, ln3_b):
    batch, seq, d = x.shape
    n = batch * seq
    assert w_in.shape[0] == DEPTH
    h, qkv, mqk, mv, mo, gates = _ln_inproj(x.reshape(n, d), ln_in_g, ln_in_b, w_in[0])
    for l in range(DEPTH):
        if l > 0:
            raise NotImplementedError("input projection of deeper layers")
        att = _attention(qkv, rel_bias[l], batch, seq)
        hm = _mlstm(mqk, mv, mo, gates, conv_w[l], conv_b[l], mlstm_i_bias[l], mlstm_f_bias[l], mlstm_norm_g[l],
                    batch, seq)
        h = _outproj(att.reshape(n, ATT_WIDTH), hm.reshape(n, MLSTM_WIDTH), h, w_out[l], ln1_g[l], ln1_b[l])
        k2, v2 = _kvproj(mem.reshape(-1, d), xattn_w_kv[l])
        m = mem.shape[1]
        h = _xattn(h.reshape(batch, seq, d), k2.reshape(batch, m, d), v2.reshape(batch, m, d),
                   xattn_w_q[l], xattn_w_o[l], ln2_g[l], ln2_b[l]).reshape(n, d)
        idx_t, g_t = _peer_topk(h, peer_w_query[l], peer_sub_keys[l])
        h = _peer_ffn(h, idx_t, g_t, peer_u[l], peer_v[l], ln3_g[l], ln3_b[l])
    return h.reshape(batch, seq, d)
```

```python
import functools
import math

import jax
import jax.numpy as jnp
from jax import lax
from jax.experimental import pallas as pl
from jax.experimental.pallas import tpu as pltpu

F32, BF16, I32, U32 = jnp.float32, jnp.bfloat16, jnp.int32, jnp.uint32

DEPTH = 1
LN_EPS = 1e-5
DEEPNORM_ALPHA = (2.0 * DEPTH) ** 0.25
CHUNK = 64
ATT_HEADS, ATT_HEAD_DIM, ATT_LEFT_CHUNKS, MAX_REL = 8, 64, 8, 128
ATT_WIDTH = ATT_HEADS * ATT_HEAD_DIM
MLSTM_HEADS, MLSTM_HEAD_DIM, CONV_WIDTH = 4, 128, 4
MLSTM_WIDTH = MLSTM_HEADS * MLSTM_HEAD_DIM
XATT_HEADS = 4
PEER_HEADS, N_KEYS, PEER_TOPK = 8, 128, 16
PEER_PICKS = PEER_HEADS * PEER_TOPK
PEER_HALF_EXPERTS = N_KEYS * N_KEYS // 2
U_TILE_PICKS = 16
PEER_UNROLL = 4

LANES = 128
SUBLANES = 8
VMEM_LIMIT_BYTES = 56 * 1024 * 1024

NEG_BIG = -1e30

ROW_BLOCK = 512
ATT_BLOCK = 256
ATT_KEY_BLOCKS = 3
MLSTM_BLOCK = 256
TOPK_BLOCK = 256
PEER_BLOCK = 64


def _params(*semantics):
    return pltpu.CompilerParams(dimension_semantics=semantics, vmem_limit_bytes=VMEM_LIMIT_BYTES)


def _layer_norm(x, g, b):
    mu = jnp.mean(x, -1, keepdims=True)
    xc = x - mu
    var = jnp.mean(xc * xc, -1, keepdims=True)
    return xc * lax.rsqrt(var + LN_EPS) * g + b


def _split2(x):
    hi = x.astype(BF16)
    lo = (x - hi.astype(F32)).astype(BF16)
    return hi, lo


def _split3(x):
    hi = x.astype(BF16)
    r = x - hi.astype(F32)
    mid = r.astype(BF16)
    lo = (r - mid.astype(F32)).astype(BF16)
    return hi, mid, lo


def _dot(a, b):
    return jnp.dot(a, b, preferred_element_type=F32)


def _dot_nt(a, b):
    return lax.dot_general(a, b, (((1,), (1,)), ((), ())), preferred_element_type=F32)


def _dot_tn(a, b):
    return lax.dot_general(a, b, (((0,), (0,)), ((), ())), preferred_element_type=F32)


def _full(shape):
    return pl.BlockSpec(shape, lambda *_: (0,) * len(shape))


def _ln_inproj_kernel(x_ref, g_ref, b_ref, wqkv_ref, wmqk_ref, wmv_ref, wmo_ref, wghi_ref, wglo_ref,
                      h_ref, qkv_ref, mqk_ref, mv_ref, mo_ref, gate_ref):
    h = _layer_norm(x_ref[...], g_ref[...], b_ref[...])
    h_ref[...] = h
    hb, hlo = _split2(h)
    qkv_ref[...] = _dot(hb, wqkv_ref[...]).astype(BF16)
    mqk_ref[...] = _dot(hb, wmqk_ref[...]).astype(BF16)
    mv_ref[...] = _dot(hb, wmv_ref[...]).astype(BF16)
    mo_ref[...] = _dot(hb, wmo_ref[...]).astype(BF16)
    gate_ref[...] = _dot(hb, wghi_ref[...]) + _dot(hlo, wghi_ref[...]) + _dot(hb, wglo_ref[...])


def _ln_inproj(x2, g, b, w_in):
    n, d = x2.shape
    a3 = 3 * ATT_WIDTH
    wqkv = w_in[:, :a3].astype(BF16)
    wmqk = w_in[:, a3:a3 + 2 * MLSTM_WIDTH].astype(BF16)
    wmv = w_in[:, a3 + 2 * MLSTM_WIDTH:a3 + 3 * MLSTM_WIDTH].astype(BF16)
    wmo = w_in[:, a3 + 3 * MLSTM_WIDTH:a3 + 4 * MLSTM_WIDTH].astype(BF16)
    wg = jnp.pad(w_in[:, a3 + 4 * MLSTM_WIDTH:], ((0, 0), (0, LANES - 2 * MLSTM_HEADS)))
    wghi = wg.astype(BF16)
    wglo = (wg - wghi.astype(F32)).astype(BF16)
    rows = lambda w: pl.BlockSpec((ROW_BLOCK, w), lambda i: (i, 0))
    return pl.pallas_call(
        _ln_inproj_kernel,
        grid=(n // ROW_BLOCK,),
        in_specs=[rows(d), _full((1, d)), _full((1, d)), _full(wqkv.shape), _full(wmqk.shape),
                  _full(wmv.shape), _full(wmo.shape), _full(wghi.shape), _full(wglo.shape)],
        out_specs=[rows(d), rows(a3), rows(2 * MLSTM_WIDTH), rows(MLSTM_WIDTH), rows(MLSTM_WIDTH), rows(LANES)],
        out_shape=[jax.ShapeDtypeStruct((n, d), F32), jax.ShapeDtypeStruct((n, a3), BF16),
                   jax.ShapeDtypeStruct((n, 2 * MLSTM_WIDTH), BF16), jax.ShapeDtypeStruct((n, MLSTM_WIDTH), BF16),
                   jax.ShapeDtypeStruct((n, MLSTM_WIDTH), BF16), jax.ShapeDtypeStruct((n, LANES), F32)],
        compiler_params=_params("arbitrary"),
        name="ln_inproj",
    )(x2, g.reshape(1, d), b.reshape(1, d), wqkv, wmqk, wmv, wmo, wghi, wglo)


def _attn_kernel(q_ref, k0_ref, k1_ref, k2_ref, v0_ref, v1_ref, v2_ref, bias_ref, o_ref):
    i = pl.program_id(1)
    nkeys = ATT_KEY_BLOCKS * ATT_BLOCK
    q = q_ref[0] * (ATT_HEAD_DIM ** -0.5)
    kcat = jnp.concatenate([k0_ref[0], k1_ref[0], k2_ref[0]], axis=0)
    vcat = jnp.concatenate([v0_ref[0], v1_ref[0], v2_ref[0]], axis=0)
    col = lax.broadcasted_iota(I32, (1, nkeys), 1)
    in_seq = col >= (ATT_KEY_BLOCKS - 1 - i) * ATT_BLOCK
    lane = lax.broadcasted_iota(I32, (1, LANES), 1)
    low = lane < ATT_HEAD_DIM
    outs = []
    for pair in range(ATT_HEADS // 2):
        sl = slice(pair * LANES, (pair + 1) * LANES)
        qp, kp, vp = q[:, sl], kcat[:, sl], vcat[:, sl]
        halves = []
        for half in range(2):
            keep = low if half == 0 else jnp.logical_not(low)
            qh = jnp.where(keep, qp, jnp.zeros_like(qp))
            s = _dot_nt(qh, kp) + bias_ref[2 * pair + half]
            s = jnp.where(in_seq, s, NEG_BIG)
            m = jnp.max(s, -1, keepdims=True)
            p = jnp.exp(s - m)
            l = jnp.sum(p, -1, keepdims=True)
            halves.append(_dot(p.astype(BF16), vp) / l)
        outs.append(jnp.where(low, halves[0], halves[1]))
    o_ref[0] = jnp.concatenate(outs, axis=-1).astype(BF16)


def _attn_bias(rel_bias):
    nq, nk = ATT_BLOCK, ATT_KEY_BLOCKS * ATT_BLOCK
    r = jnp.arange(nq)[:, None]
    j = jnp.arange(nk)[None, :]
    allowed = (j // CHUNK >= r // CHUNK) & (j // CHUNK <= r // CHUNK + ATT_LEFT_CHUNKS)
    period = nq + nk - 1
    i = jnp.arange(period)
    d = jnp.where(i < nk, i, i - period)
    rel = d - (ATT_KEY_BLOCKS - 1) * ATT_BLOCK
    line = rel_bias[:, jnp.clip(rel, -MAX_REL, MAX_REL) + MAX_REL].astype(F32)
    rep = jnp.tile(line, (1, nq + 1))[:, :nq * (period - 1)]
    tab = rep.reshape(-1, nq, period - 1)[:, :, :nk]
    return jnp.where(allowed[None], tab, NEG_BIG)


def _attention(qkv, rel_bias, batch, seq):
    assert ATT_LEFT_CHUNKS * CHUNK == (ATT_KEY_BLOCKS - 1) * ATT_BLOCK
    qkv3 = qkv.reshape(batch, seq, 3 * ATT_WIDTH)
    bias = _attn_bias(rel_bias)
    blk = (1, ATT_BLOCK, ATT_WIDTH)

    def kv_spec(col, j):
        return pl.BlockSpec(blk, lambda b, i: (b, jnp.maximum(i - (ATT_KEY_BLOCKS - 1) + j, 0), col))

    return pl.pallas_call(
        _attn_kernel,
        grid=(batch, seq // ATT_BLOCK),
        in_specs=[pl.BlockSpec(blk, lambda b, i: (b, i, 0))]
        + [kv_spec(1, j) for j in range(ATT_KEY_BLOCKS)] + [kv_spec(2, j) for j in range(ATT_KEY_BLOCKS)]
        + [_full(bias.shape)],
        out_specs=pl.BlockSpec(blk, lambda b, i: (b, i, 0)),
        out_shape=jax.ShapeDtypeStruct((batch, seq, ATT_WIDTH), BF16),
        compiler_params=_params("arbitrary", "arbitrary"),
        name="attn",
    )(qkv3, qkv3, qkv3, qkv3, qkv3, qkv3, qkv3, bias)


def _mlstm_kernel(mqk_ref, mv_ref, mo_ref, gate_ref, convw_ref, convb_ref, gbias_ref, ng_ref, out_ref,
                  xpad, q_scr, k_scr, gate_scr, logf_scr, c_st, n_st, m_st):
    j = pl.program_id(1)
    rows = MLSTM_BLOCK
    d = MLSTM_HEAD_DIM

    @pl.when(j == 0)
    def _():
        xpad[0:SUBLANES, :] = jnp.zeros((SUBLANES, 2 * MLSTM_WIDTH), F32)
        c_st[...] = jnp.zeros_like(c_st)
        n_st[...] = jnp.zeros_like(n_st)
        m_st[...] = jnp.zeros_like(m_st)

    xpad[SUBLANES:SUBLANES + rows, :] = mqk_ref[0].astype(F32)
    acc = jnp.broadcast_to(convb_ref[...], (rows, 2 * MLSTM_WIDTH))
    for t in range(CONV_WIDTH):
        acc = acc + convw_ref[t:t + 1, :] * xpad[pl.ds(SUBLANES - (CONV_WIDTH - 1) + t, rows), :]
    xpad[0:SUBLANES, :] = xpad[rows:rows + SUBLANES, :]
    qk = acc * jax.nn.sigmoid(acc)
    q_scr[...] = qk[:, :MLSTM_WIDTH].astype(BF16)
    k_scr[...] = qk[:, MLSTM_WIDTH:] * (d ** -0.5)

    gates = gate_ref[0] + gbias_ref[...]
    gate_scr[...] = gates
    logf_scr[...] = jax.nn.log_sigmoid(gates)

    ri = lax.broadcasted_iota(I32, (CHUNK, CHUNK), 0)
    ci = lax.broadcasted_iota(I32, (CHUNK, CHUNK), 1)
    causal = ci <= ri
    tri = causal.astype(BF16)

    def chunk_body(c, carry):
        r0 = pl.multiple_of(c * CHUNK, CHUNK)
        g = gate_scr[pl.ds(r0, CHUNK), :]
        lf = logf_scr[pl.ds(r0, CHUNK), :]
        l1, l2, l3 = _split3(lf)
        bcol = _dot(tri, l1) + _dot(tri, l2) + _dot(tri, l3)
        g_t = g.T
        b_t = bcol.T
        for h in range(MLSTM_HEADS):
            hs = slice(h * d, (h + 1) * d)
            ic = g[:, h:h + 1]
            bc = bcol[:, MLSTM_HEADS + h:MLSTM_HEADS + h + 1]
            ir = g_t[h:h + 1, :]
            br = b_t[MLSTM_HEADS + h:MLSTM_HEADS + h + 1, :]
            m_prev = m_st[h:h + 1, 0:1]
            n_prev = n_st[h:h + 1, :]
            c_prev = c_st[h]
            qh = q_scr[pl.ds(r0, CHUNK), hs]
            kh = k_scr[pl.ds(r0, CHUNK), hs]
            vh = mv_ref[0, pl.ds(r0, CHUNK), hs]

            log_d = jnp.where(causal, bc - br + ir, NEG_BIG)
            inter = bc + m_prev
            m_t = jnp.maximum(inter, jnp.max(log_d, -1, keepdims=True))
            d_mat = jnp.exp(log_d - m_t)
            w_inter = jnp.exp(inter - m_t)
            qk_d = _dot_nt(qh, kh.astype(BF16)) * d_mat
            num = w_inter * _dot(qh, c_prev.astype(BF16)) + _dot(qk_d.astype(BF16), vh)
            den = (w_inter * jnp.sum(qh.astype(F32) * n_prev, -1, keepdims=True)
                   + jnp.sum(qk_d, -1, keepdims=True))
            hh = num / jnp.maximum(jnp.abs(den), jnp.exp(-m_t))

            b_last = bc[CHUNK - 1:CHUNK, :]
            log_in = b_last - bc + ic
            m_new = jnp.maximum(b_last + m_prev, jnp.max(log_in, 0, keepdims=True))
            w_prev = jnp.exp(b_last + m_prev - m_new)
            kw = kh * jnp.exp(log_in - m_new)
            c_st[h] = w_prev * c_prev + _dot_tn(kw.astype(BF16), vh)
            n_st[h:h + 1, :] = w_prev * n_prev + jnp.sum(kw, 0, keepdims=True)
            m_st[h:h + 1, :] = jnp.broadcast_to(m_new, (1, LANES))

            mu = jnp.mean(hh, -1, keepdims=True)
            hc = hh - mu
            var = jnp.mean(hc * hc, -1, keepdims=True)
            hn = hc * lax.rsqrt(var + LN_EPS) * ng_ref[:, hs]
            og = jax.nn.sigmoid(mo_ref[0, pl.ds(r0, CHUNK), hs].astype(F32))
            out_ref[0, pl.ds(r0, CHUNK), hs] = (og * hn).astype(BF16)
        return carry

    lax.fori_loop(0, rows // CHUNK, chunk_body, 0)


def _mlstm(mqk, mv, mo, gates, conv_w, conv_b, i_bias, f_bias, norm_g, batch, seq):
    w2 = 2 * MLSTM_WIDTH
    gbias = jnp.pad(jnp.concatenate([i_bias, f_bias]).astype(F32), (0, LANES - 2 * MLSTM_HEADS)).reshape(1, LANES)
    blk = lambda w: pl.BlockSpec((1, MLSTM_BLOCK, w), lambda b, i: (b, i, 0))
    return pl.pallas_call(
        _mlstm_kernel,
        grid=(batch, seq // MLSTM_BLOCK),
        in_specs=[blk(w2), blk(MLSTM_WIDTH), blk(MLSTM_WIDTH), blk(LANES),
                  _full((CONV_WIDTH, w2)), _full((1, w2)), _full((1, LANES)), _full((1, MLSTM_WIDTH))],
        out_specs=blk(MLSTM_WIDTH),
        out_shape=jax.ShapeDtypeStruct((batch, seq, MLSTM_WIDTH), BF16),
        scratch_shapes=[pltpu.VMEM((MLSTM_BLOCK + SUBLANES, w2), F32),
                        pltpu.VMEM((MLSTM_BLOCK, MLSTM_WIDTH), BF16),
                        pltpu.VMEM((MLSTM_BLOCK, MLSTM_WIDTH), F32),
                        pltpu.VMEM((MLSTM_BLOCK, LANES), F32),
                        pltpu.VMEM((MLSTM_BLOCK, LANES), F32),
                        pltpu.VMEM((MLSTM_HEADS, MLSTM_HEAD_DIM, MLSTM_HEAD_DIM), F32),
                        pltpu.VMEM((SUBLANES, MLSTM_HEAD_DIM), F32),
                        pltpu.VMEM((SUBLANES, LANES), F32)],
        compiler_params=_params("arbitrary", "arbitrary"),
        name="mlstm",
    )(mqk.reshape(batch, seq, w2), mv.reshape(batch, seq, MLSTM_WIDTH), mo.reshape(batch, seq, MLSTM_WIDTH),
      gates.reshape(batch, seq, LANES), conv_w.astype(F32), conv_b.reshape(1, w2).astype(F32), gbias,
      norm_g.reshape(1, MLSTM_WIDTH).astype(F32))


def _outproj_kernel(att_ref, hm_ref, h_ref, wa_ref, wm_ref, g_ref, b_ref, o_ref):
    y = _dot(att_ref[...], wa_ref[...]) + _dot(hm_ref[...], wm_ref[...])
    o_ref[...] = _layer_norm(DEEPNORM_ALPHA * h_ref[...] + y, g_ref[...], b_ref[...])


def _outproj(att, hm, h, w_out, g, b):
    n, d = h.shape
    wa = w_out[:ATT_WIDTH].astype(BF16)
    wm = w_out[ATT_WIDTH:].astype(BF16)
    rows = lambda w: pl.BlockSpec((ROW_BLOCK, w), lambda i: (i, 0))
    return pl.pallas_call(
        _outproj_kernel,
        grid=(n // ROW_BLOCK,),
        in_specs=[rows(ATT_WIDTH), rows(MLSTM_WIDTH), rows(d), _full(wa.shape), _full(wm.shape),
                  _full((1, d)), _full((1, d))],
        out_specs=rows(d),
        out_shape=jax.ShapeDtypeStruct((n, d), F32),
        compiler_params=_params("arbitrary"),
        name="outproj",
    )(att, hm, h, wa, wm, g.reshape(1, d), b.reshape(1, d))


def _kvproj_kernel(mem_ref, w_ref, k_ref, v_ref):
    kv = _dot(mem_ref[...].astype(BF16), w_ref[...])
    d = k_ref.shape[-1]
    k_ref[...] = kv[:, :d].astype(BF16)
    v_ref[...] = kv[:, d:].astype(BF16)


def _kvproj(mem2, w_kv):
    n, d = mem2.shape
    w = w_kv.astype(BF16)
    blk = min(ROW_BLOCK, n)
    rows = pl.BlockSpec((blk, d), lambda i: (i, 0))
    return pl.pallas_call(
        _kvproj_kernel,
        grid=(n // blk,),
        in_specs=[rows, _full(w.shape)],
        out_specs=[rows, rows],
        out_shape=[jax.ShapeDtypeStruct((n, d), BF16)] * 2,
        compiler_params=_params("arbitrary"),
        name="kvproj",
    )(mem2, w)


def _xattn_kernel(h_ref, k_ref, v_ref, wq_ref, wo_ref, g_ref, b_ref, o_ref):
    h = h_ref[0]
    d = h.shape[-1]
    dh = d // XATT_HEADS
    q = (_dot(h.astype(BF16), wq_ref[...]) * (dh ** -0.5)).astype(BF16)
    outs = []
    for hd in range(XATT_HEADS):
        sl = slice(hd * dh, (hd + 1) * dh)
        s = _dot_nt(q[:, sl], k_ref[0, :, sl])
        m = jnp.max(s, -1, keepdims=True)
        p = jnp.exp(s - m)
        l = jnp.sum(p, -1, keepdims=True)
        outs.append((_dot(p.astype(BF16), v_ref[0, :, sl]) / l).astype(BF16))
    y = _dot(jnp.concatenate(outs, axis=-1), wo_ref[...])
    o_ref[0] = _layer_norm(DEEPNORM_ALPHA * h + y, g_ref[...], b_ref[...])


def _xattn(h3, k3, v3, w_q, w_o, g, b):
    batch, seq, d = h3.shape
    m = k3.shape[1]
    wq = w_q.astype(BF16)
    wo = w_o.astype(BF16)
    blk = pl.BlockSpec((1, ROW_BLOCK, d), lambda bb, i: (bb, i, 0))
    mem = pl.BlockSpec((1, m, d), lambda bb, i: (bb, 0, 0))
    return pl.pallas_call(
        _xattn_kernel,
        grid=(batch, seq // ROW_BLOCK),
        in_specs=[blk, mem, mem, _full(wq.shape), _full(wo.shape), _full((1, d)), _full((1, d))],
        out_specs=blk,
        out_shape=jax.ShapeDtypeStruct((batch, seq, d), F32),
        compiler_params=_params("arbitrary", "arbitrary"),
        name="xattn",
    )(h3, k3, v3, wq, wo, g.reshape(1, d), b.reshape(1, d))


def _topk_rows(s, k, big):
    iota = lax.broadcasted_iota(I32, s.shape, 0)
    vals, idxs = [], []
    for _ in range(k):
        m = jnp.max(s, axis=0, keepdims=True)
        sel = jnp.min(jnp.where(s == m, iota, big), axis=0, keepdims=True)
        vals.append(m)
        idxs.append(sel)
        s = jnp.where(iota == sel, -jnp.inf, s)
    return vals, idxs


def _stack_rows(rows_list):
    k = len(rows_list)
    t = rows_list[0].shape[-1]
    iota = lax.broadcasted_iota(I32, (k, t), 0)
    out = jnp.broadcast_to(rows_list[0], (k, t))
    for r in range(1, k):
        out = jnp.where(iota == r, jnp.broadcast_to(rows_list[r], (k, t)), out)
    return out


def _peer_topk_kernel(h_ref, wq_ref, khi_ref, klo_ref, row_ref, hbit_ref, g_ref, e_scr, g_scr):
    t = h_ref.shape[0]
    kk = PEER_TOPK
    q = _dot(h_ref[...].astype(BF16), wq_ref[...])
    half8 = kk // 2
    sub = lax.broadcasted_iota(I32, (half8, t), 0)
    for hh in range(PEER_HEADS):
        tops = []
        for p in range(2):
            c0 = (hh * 2 + p) * N_KEYS
            qhi, qlo = _split2(q[:, c0:c0 + N_KEYS])
            s = _dot_nt(khi_ref[p], qhi) + _dot_nt(khi_ref[p], qlo) + _dot_nt(klo_ref[p], qhi)
            vals, idxs = _topk_rows(s, kk, N_KEYS)
            tops.append((_stack_rows(vals), _stack_rows(idxs)))
        (s0, i0), (s1, i1) = tops
        e0 = i0 * N_KEYS
        cand, cexp, cflat = [], [], []
        for b in range(half8):
            cand.append(s0[:half8] + s1[b:b + 1])
            cexp.append(e0[:half8] + i1[b:b + 1])
            cflat.append(sub * kk + b)
        cand.append(s0[half8:] + s1[0:1])
        cexp.append(e0[half8:] + i1[0:1])
        cflat.append((sub + half8) * kk)
        cand.append(s0[0:1] + s1[half8:])
        cexp.append(e0[0:1] + i1[half8:])
        cflat.append(sub + half8)
        cand = jnp.concatenate(cand, axis=0)
        cexp = jnp.concatenate(cexp, axis=0)
        cflat = jnp.concatenate(cflat, axis=0)
        best_s, best_e = [], []
        for _ in range(kk):
            m = jnp.max(cand, axis=0, keepdims=True)
            jsel = jnp.min(jnp.where(cand == m, cflat, kk * kk), axis=0, keepdims=True)
            hit = cflat == jsel
            best_s.append(m)
            best_e.append(jnp.max(jnp.where(hit, cexp, 0), axis=0, keepdims=True))
            cand = jnp.where(hit, -jnp.inf, cand)
        bs = _stack_rows(best_s)
        ex = jnp.exp(bs - bs[0:1])
        g_scr[hh * kk:(hh + 1) * kk, :] = ex / jnp.sum(ex, axis=0, keepdims=True)
        e_scr[hh * kk:(hh + 1) * kk, :] = _stack_rows(best_e)
    e = e_scr[...]
    hb = e >> (PEER_HALF_EXPERTS.bit_length() - 1)
    row_ref[...] = (e & (PEER_HALF_EXPERTS - 1)).T
    hbit_ref[...] = hb.astype(F32).T
    g_ref[...] = g_scr[...].T


def _peer_topk(h2, w_query, sub_keys):
    n, d = h2.shape
    wq = w_query.astype(BF16)
    khi = sub_keys.astype(BF16)
    klo = (sub_keys - khi.astype(F32)).astype(BF16)
    rows = pl.BlockSpec((TOPK_BLOCK, PEER_PICKS), lambda i: (i, 0))
    return pl.pallas_call(
        _peer_topk_kernel,
        grid=(n // TOPK_BLOCK,),
        in_specs=[pl.BlockSpec((TOPK_BLOCK, d), lambda i: (i, 0)), _full(wq.shape), _full(khi.shape), _full(klo.shape)],
        out_specs=[rows, rows, rows],
        out_shape=[jax.ShapeDtypeStruct((n, PEER_PICKS), I32),
                   jax.ShapeDtypeStruct((n, PEER_PICKS), F32), jax.ShapeDtypeStruct((n, PEER_PICKS), F32)],
        scratch_shapes=[pltpu.VMEM((PEER_PICKS, TOPK_BLOCK), I32), pltpu.VMEM((PEER_PICKS, TOPK_BLOCK), F32)],
        compiler_params=_params("arbitrary"),
        name="peer_topk",
    )(h2, wq, khi, klo)


def _pack_table(tab):
    e, d = tab.shape
    bits = lax.bitcast_convert_type(tab.astype(BF16), jnp.uint16).astype(U32)
    packed = bits[:e // 2] | (bits[e // 2:] << 16)
    return packed.reshape(e // 2, d // LANES, LANES)


def _peer_u_consts():
    k = jnp.arange(2 * LANES)
    col = jnp.arange(2 * PEER_PICKS)
    row = jnp.arange(PEER_PICKS)
    fold = ((k[None, :] // 16 == jnp.arange(2 * U_TILE_PICKS)[:, None] // 2)
            & (k[None, :] % 2 == jnp.arange(2 * U_TILE_PICKS)[:, None] % 2)).astype(BF16)
    lanesum = (((col[None, :] // 2) % 2) == (k[:, None] // LANES)).astype(BF16)
    place = ((col[None, :] // 64 == row[:, None] // 32) & ((col[None, :] % 64) // 4 == (row[:, None] % 32) // 2)
             & (col[None, :] % 2 == row[:, None] % 2)).astype(F32)
    dup = ((col[None, :] // 2) == jnp.arange(PEER_PICKS)[:, None]).astype(BF16)
    return fold, lanesum, place, dup


def _peer_u_kernel(row_ref, x_ref, g_ref, hbit_ref, tab_ref, fold_ref, lanesum_ref, place_ref, dup_ref,
                   coefh_ref, res_scr):
    tb = x_ref.shape[0]
    ntiles = PEER_PICKS // (2 * U_TILE_PICKS)

    def tokens(i, carry):
        for u in range(PEER_UNROLL):
            t = i * PEER_UNROLL + u
            x = x_ref[t]
            xbits = pltpu.bitcast(x.astype(BF16).astype(F32), U32)
            xw = pltpu.bitcast(xbits | (xbits >> 16), BF16)
            for j in range(ntiles):
                cols = []
                for ab in range(2):
                    prods = []
                    for mm in range(U_TILE_PICKS):
                        c = j * 2 * U_TILE_PICKS + 2 * mm + ab
                        prods.append(pltpu.bitcast(tab_ref[row_ref[t, c]], BF16) * xw)
                    cols.append(jnp.concatenate(prods, axis=0))
                res_scr[t, j] = _dot(fold_ref[...], jnp.concatenate(cols, axis=1))
        return carry

    lax.fori_loop(0, tb // PEER_UNROLL, tokens, 0)

    r = res_scr[...].reshape(tb * PEER_PICKS, 2 * LANES).astype(BF16)
    s = _dot(r, lanesum_ref[...])
    a2 = jnp.sum(s.reshape(tb, PEER_PICKS, 2 * PEER_PICKS) * place_ref[...], axis=1)
    ghi, glo = _split2(g_ref[...])
    g2 = _dot(ghi, dup_ref[...]) + _dot(glo, dup_ref[...])
    hbit2 = _dot(hbit_ref[...].astype(BF16), dup_ref[...])
    parity = (lax.broadcasted_iota(I32, a2.shape, 1) % 2).astype(F32)
    coef = g2 * (0.5 * a2 * (1.0 + lax.erf(a2 * math.sqrt(0.5))))
    coefh_ref[...] = jnp.where(hbit2 == parity, coef, 0.0).astype(BF16)


def _peer_v_consts():
    col = jnp.arange(2 * PEER_PICKS * SUBLANES)
    ab, p, r = col // (PEER_PICKS * SUBLANES), (col // 16) % (PEER_PICKS // 2), col % 16
    src = jnp.arange(2 * PEER_PICKS)
    expand = ((src[:, None] // 2 == (2 * p + ab)[None, :]) & (src[:, None] % 2 == (r % 2)[None, :])).astype(BF16)
    kk = jnp.arange(PEER_PICKS * SUBLANES)
    diag = ((kk[None, :] % 16) // 2 == (jnp.arange(16) % SUBLANES)[:, None]).astype(F32)
    return expand, diag


def _peer_v_kernel(row_ref, coefh_ref, x_ref, tab_ref, expand_ref, diag_ref, g_ref, b_ref, o_ref, lrow_scr, y_scr):
    tb = x_ref.shape[0]
    kdim = PEER_PICKS * SUBLANES
    lrow_scr[...] = _dot(coefh_ref[...], expand_ref[...])
    diag = diag_ref[...]

    def tokens(i, carry):
        for u in range(PEER_UNROLL):
            t = i * PEER_UNROLL + u
            even = jnp.broadcast_to(lrow_scr[pl.ds(t, 1), 0:kdim], (SUBLANES, kdim))
            odd = jnp.broadcast_to(lrow_scr[pl.ds(t, 1), kdim:2 * kdim], (SUBLANES, kdim))
            lhs = (jnp.concatenate([even, odd], axis=0) * diag).astype(BF16)
            tiles = []
            for p in range(PEER_PICKS // 2):
                wa = pltpu.bitcast(tab_ref[row_ref[t, 2 * p]], BF16)
                wb = pltpu.bitcast(tab_ref[row_ref[t, 2 * p + 1]], BF16)
                tiles.append(jnp.concatenate([wa, wb], axis=1))
            res = _dot(lhs, jnp.concatenate(tiles, axis=0))
            y_scr[t] = res[0:SUBLANES, 0:LANES] + res[SUBLANES:, LANES:]
        return carry

    lax.fori_loop(0, tb // PEER_UNROLL, tokens, 0)
    z = DEEPNORM_ALPHA * x_ref[...] + y_scr[...]
    cnt = z.shape[1] * z.shape[2]
    mu = jnp.sum(jnp.sum(z, axis=2, keepdims=True), axis=1, keepdims=True) / cnt
    zc = z - mu
    var = jnp.sum(jnp.sum(zc * zc, axis=2, keepdims=True), axis=1, keepdims=True) / cnt
    o_ref[...] = zc * lax.rsqrt(var + LN_EPS) * g_ref[...] + b_ref[...]


def _peer_ffn(h2, row, hbit, gate, expert_u, expert_v, ln_g, ln_b):
    n, d = h2.shape
    rows8 = d // LANES
    assert rows8 == SUBLANES and PEER_PICKS == LANES and expert_u.shape[0] == 2 * PEER_HALF_EXPERTS
    x3 = h2.reshape(n, rows8, LANES)
    u_pk = _pack_table(expert_u)
    v_pk = _pack_table(expert_v)
    smem = pl.BlockSpec((PEER_BLOCK, PEER_PICKS), lambda i: (i, 0), memory_space=pltpu.SMEM)
    picks = pl.BlockSpec((PEER_BLOCK, PEER_PICKS), lambda i: (i, 0))
    picks2 = pl.BlockSpec((PEER_BLOCK, 2 * PEER_PICKS), lambda i: (i, 0))
    tok = pl.BlockSpec((PEER_BLOCK, rows8, LANES), lambda i: (i, 0, 0))
    table = pl.BlockSpec(u_pk.shape, lambda i: (0, 0, 0), pipeline_mode=pl.Buffered(1))
    u_consts = _peer_u_consts()
    coefh = pl.pallas_call(
        _peer_u_kernel,
        grid=(n // PEER_BLOCK,),
        in_specs=[smem, tok, picks, picks, table] + [_full(c.shape) for c in u_consts],
        out_specs=picks2,
        out_shape=jax.ShapeDtypeStruct((n, 2 * PEER_PICKS), BF16),
        scratch_shapes=[pltpu.VMEM((PEER_BLOCK, PEER_PICKS // (2 * U_TILE_PICKS), 2 * U_TILE_PICKS, 2 * LANES), F32)],
        compiler_params=_params("arbitrary"),
        name="peer_u",
    )(row, x3, gate, hbit, u_pk, *u_consts)
    v_consts = _peer_v_consts()
    out = pl.pallas_call(
        _peer_v_kernel,
        grid=(n // PEER_BLOCK,),
        in_specs=[smem, picks2, tok, table] + [_full(c.shape) for c in v_consts]
        + [_full((rows8, LANES)), _full((rows8, LANES))],
        out_specs=tok,
        out_shape=jax.ShapeDtypeStruct((n, rows8, LANES), F32),
        scratch_shapes=[pltpu.VMEM((PEER_BLOCK, 2 * PEER_PICKS * SUBLANES), F32),
                        pltpu.VMEM((PEER_BLOCK, rows8, LANES), F32)],
        compiler_params=_params("arbitrary"),
        name="peer_v",
    )(row, coefh, x3, v_pk, *v_consts, ln_g.reshape(rows8, LANES), ln_b.reshape(rows8, LANES))
    return out.reshape(n, d)


def kernel(x, mem, ln_in_g, ln_in_b, w_in, conv_w, conv_b, mlstm_i_bias, mlstm_f_bias, mlstm_norm_g, rel_bias, w_out, ln1_g, ln1_b, xattn_w_q, xattn_w_kv, xattn_w_o, ln2_g, ln2_b, peer_w_query, peer_sub_keys, peer_u, peer_v, ln3_g, ln3_b):
    batch, seq, d = x.shape
    n = batch * seq
    assert w_in.shape[0] == DEPTH
    h, qkv, mqk, mv, mo, gates = _ln_inproj(x.reshape(n, d), ln_in_g, ln_in_b, w_in[0])
    for l in range(DEPTH):
        if l > 0:
            raise NotImplementedError("input projection of deeper layers")
        att = _attention(qkv, rel_bias[l], batch, seq)
        hm = _mlstm(mqk, mv, mo, gates, conv_w[l], conv_b[l], mlstm_i_bias[l], mlstm_f_bias[l], mlstm_norm_g[l],
                    batch, seq)
        h = _outproj(att.reshape(n, ATT_WIDTH), hm.reshape(n, MLSTM_WIDTH), h, w_out[l], ln1_g[l], ln1_b[l])
        k2, v2 = _kvproj(mem.reshape(-1, d), xattn_w_kv[l])
        m = mem.shape[1]
        h = _xattn(h.reshape(batch, seq, d), k2.reshape(batch, m, d), v2.reshape(batch, m, d),
                   xattn_w_q[l], xattn_w_o[l], ln2_g[l], ln2_b[l]).reshape(n, d)
        row, hbit, gate = _peer_topk(h, peer_w_query[l], peer_sub_keys[l])
        h = _peer_ffn(h, row, hbit, gate, peer_u[l], peer_v[l], ln3_g[l], ln3_b[l])
    return h.reshape(batch, seq, d)
```

```python
import functools
import math

import jax
import jax.numpy as jnp
from jax import lax
from jax.experimental import pallas as pl
from jax.experimental.pallas import tpu as pltpu

F32, BF16, I32, U32 = jnp.float32, jnp.bfloat16, jnp.int32, jnp.uint32

DEPTH = 1
LN_EPS = 1e-5
DEEPNORM_ALPHA = (2.0 * DEPTH) ** 0.25
CHUNK = 64
ATT_HEADS, ATT_HEAD_DIM, ATT_LEFT_CHUNKS, MAX_REL = 8, 64, 8, 128
ATT_WIDTH = ATT_HEADS * ATT_HEAD_DIM
MLSTM_HEADS, MLSTM_HEAD_DIM, CONV_WIDTH = 4, 128, 4
MLSTM_WIDTH = MLSTM_HEADS * MLSTM_HEAD_DIM
XATT_HEADS = 4
PEER_HEADS, N_KEYS, PEER_TOPK = 8, 128, 16
PEER_PICKS = PEER_HEADS * PEER_TOPK
PEER_HALF_EXPERTS = N_KEYS * N_KEYS // 2
U_TILE_PICKS = 16
PEER_UNROLL = 8

LANES = 128
SUBLANES = 8
VMEM_LIMIT_BYTES = 56 * 1024 * 1024

NEG_BIG = -1e30

ROW_BLOCK = 512
ATT_BLOCK = 256
ATT_KEY_BLOCKS = 3
MLSTM_BLOCK = 256
TOPK_BLOCK = 256
PEER_BLOCK = 64


def _params(*semantics):
    return pltpu.CompilerParams(dimension_semantics=semantics, vmem_limit_bytes=VMEM_LIMIT_BYTES)


def _layer_norm(x, g, b):
    mu = jnp.mean(x, -1, keepdims=True)
    xc = x - mu
    var = jnp.mean(xc * xc, -1, keepdims=True)
    return xc * lax.rsqrt(var + LN_EPS) * g + b


def _split2(x):
    hi = x.astype(BF16)
    lo = (x - hi.astype(F32)).astype(BF16)
    return hi, lo


def _split3(x):
    hi = x.astype(BF16)
    r = x - hi.astype(F32)
    mid = r.astype(BF16)
    lo = (r - mid.astype(F32)).astype(BF16)
    return hi, mid, lo


def _dot(a, b):
    return jnp.dot(a, b, preferred_element_type=F32)


def _dot_nt(a, b):
    return lax.dot_general(a, b, (((1,), (1,)), ((), ())), preferred_element_type=F32)


def _dot_tn(a, b):
    return lax.dot_general(a, b, (((0,), (0,)), ((), ())), preferred_element_type=F32)


def _full(shape):
    return pl.BlockSpec(shape, lambda *_: (0,) * len(shape))


def _ln_inproj_kernel(x_ref, g_ref, b_ref, wqkv_ref, wmqk_ref, wmv_ref, wmo_ref, wghi_ref, wglo_ref,
                      h_ref, qkv_ref, mqk_ref, mv_ref, mo_ref, gate_ref):
    h = _layer_norm(x_ref[...], g_ref[...], b_ref[...])
    h_ref[...] = h
    hb, hlo = _split2(h)
    qkv_ref[...] = _dot(hb, wqkv_ref[...]).astype(BF16)
    mqk_ref[...] = _dot(hb, wmqk_ref[...]).astype(BF16)
    mv_ref[...] = _dot(hb, wmv_ref[...]).astype(BF16)
    mo_ref[...] = _dot(hb, wmo_ref[...]).astype(BF16)
    gate_ref[...] = _dot(hb, wghi_ref[...]) + _dot(hlo, wghi_ref[...]) + _dot(hb, wglo_ref[...])


def _ln_inproj(x2, g, b, w_in):
    n, d = x2.shape
    a3 = 3 * ATT_WIDTH
    wqkv = w_in[:, :a3].astype(BF16)
    wmqk = w_in[:, a3:a3 + 2 * MLSTM_WIDTH].astype(BF16)
    wmv = w_in[:, a3 + 2 * MLSTM_WIDTH:a3 + 3 * MLSTM_WIDTH].astype(BF16)
    wmo = w_in[:, a3 + 3 * MLSTM_WIDTH:a3 + 4 * MLSTM_WIDTH].astype(BF16)
    wg = jnp.pad(w_in[:, a3 + 4 * MLSTM_WIDTH:], ((0, 0), (0, LANES - 2 * MLSTM_HEADS)))
    wghi = wg.astype(BF16)
    wglo = (wg - wghi.astype(F32)).astype(BF16)
    rows = lambda w: pl.BlockSpec((ROW_BLOCK, w), lambda i: (i, 0))
    return pl.pallas_call(
        _ln_inproj_kernel,
        grid=(n // ROW_BLOCK,),
        in_specs=[rows(d), _full((1, d)), _full((1, d)), _full(wqkv.shape), _full(wmqk.shape),
                  _full(wmv.shape), _full(wmo.shape), _full(wghi.shape), _full(wglo.shape)],
        out_specs=[rows(d), rows(a3), rows(2 * MLSTM_WIDTH), rows(MLSTM_WIDTH), rows(MLSTM_WIDTH), rows(LANES)],
        out_shape=[jax.ShapeDtypeStruct((n, d), F32), jax.ShapeDtypeStruct((n, a3), BF16),
                   jax.ShapeDtypeStruct((n, 2 * MLSTM_WIDTH), BF16), jax.ShapeDtypeStruct((n, MLSTM_WIDTH), BF16),
                   jax.ShapeDtypeStruct((n, MLSTM_WIDTH), BF16), jax.ShapeDtypeStruct((n, LANES), F32)],
        compiler_params=_params("arbitrary"),
        name="ln_inproj",
    )(x2, g.reshape(1, d), b.reshape(1, d), wqkv, wmqk, wmv, wmo, wghi, wglo)


def _attn_kernel(q_ref, k0_ref, k1_ref, k2_ref, v0_ref, v1_ref, v2_ref, bias_ref, o_ref):
    i = pl.program_id(1)
    nkeys = ATT_KEY_BLOCKS * ATT_BLOCK
    q = q_ref[0] * (ATT_HEAD_DIM ** -0.5)
    kcat = jnp.concatenate([k0_ref[0], k1_ref[0], k2_ref[0]], axis=0)
    vcat = jnp.concatenate([v0_ref[0], v1_ref[0], v2_ref[0]], axis=0)
    col = lax.broadcasted_iota(I32, (1, nkeys), 1)
    in_seq = col >= (ATT_KEY_BLOCKS - 1 - i) * ATT_BLOCK
    lane = lax.broadcasted_iota(I32, (1, LANES), 1)
    low = lane < ATT_HEAD_DIM
    outs = []
    for pair in range(ATT_HEADS // 2):
        sl = slice(pair * LANES, (pair + 1) * LANES)
        qp, kp, vp = q[:, sl], kcat[:, sl], vcat[:, sl]
        halves = []
        for half in range(2):
            keep = low if half == 0 else jnp.logical_not(low)
            qh = jnp.where(keep, qp, jnp.zeros_like(qp))
            s = _dot_nt(qh, kp) + bias_ref[2 * pair + half]
            s = jnp.where(in_seq, s, NEG_BIG)
            m = jnp.max(s, -1, keepdims=True)
            p = jnp.exp(s - m)
            l = jnp.sum(p, -1, keepdims=True)
            halves.append(_dot(p.astype(BF16), vp) / l)
        outs.append(jnp.where(low, halves[0], halves[1]))
    o_ref[0] = jnp.concatenate(outs, axis=-1).astype(BF16)


def _attn_bias(rel_bias):
    nq, nk = ATT_BLOCK, ATT_KEY_BLOCKS * ATT_BLOCK
    r = jnp.arange(nq)[:, None]
    j = jnp.arange(nk)[None, :]
    allowed = (j // CHUNK >= r // CHUNK) & (j // CHUNK <= r // CHUNK + ATT_LEFT_CHUNKS)
    period = nq + nk - 1
    i = jnp.arange(period)
    d = jnp.where(i < nk, i, i - period)
    rel = d - (ATT_KEY_BLOCKS - 1) * ATT_BLOCK
    line = rel_bias[:, jnp.clip(rel, -MAX_REL, MAX_REL) + MAX_REL].astype(F32)
    rep = jnp.tile(line, (1, nq + 1))[:, :nq * (period - 1)]
    tab = rep.reshape(-1, nq, period - 1)[:, :, :nk]
    return jnp.where(allowed[None], tab, NEG_BIG)


def _attention(qkv, rel_bias, batch, seq):
    assert ATT_LEFT_CHUNKS * CHUNK == (ATT_KEY_BLOCKS - 1) * ATT_BLOCK
    qkv3 = qkv.reshape(batch, seq, 3 * ATT_WIDTH)
    bias = _attn_bias(rel_bias)
    blk = (1, ATT_BLOCK, ATT_WIDTH)

    def kv_spec(col, j):
        return pl.BlockSpec(blk, lambda b, i: (b, jnp.maximum(i - (ATT_KEY_BLOCKS - 1) + j, 0), col))

    return pl.pallas_call(
        _attn_kernel,
        grid=(batch, seq // ATT_BLOCK),
        in_specs=[pl.BlockSpec(blk, lambda b, i: (b, i, 0))]
        + [kv_spec(1, j) for j in range(ATT_KEY_BLOCKS)] + [kv_spec(2, j) for j in range(ATT_KEY_BLOCKS)]
        + [_full(bias.shape)],
        out_specs=pl.BlockSpec(blk, lambda b, i: (b, i, 0)),
        out_shape=jax.ShapeDtypeStruct((batch, seq, ATT_WIDTH), BF16),
        compiler_params=_params("arbitrary", "arbitrary"),
        name="attn",
    )(qkv3, qkv3, qkv3, qkv3, qkv3, qkv3, qkv3, bias)


def _mlstm_kernel(mqk_ref, mv_ref, mo_ref, gate_ref, convw_ref, convb_ref, gbias_ref, ng_ref, out_ref,
                  xpad, q_scr, k_scr, gate_scr, logf_scr, c_st, n_st, m_st):
    j = pl.program_id(1)
    rows = MLSTM_BLOCK
    d = MLSTM_HEAD_DIM

    @pl.when(j == 0)
    def _():
        xpad[0:SUBLANES, :] = jnp.zeros((SUBLANES, 2 * MLSTM_WIDTH), F32)
        c_st[...] = jnp.zeros_like(c_st)
        n_st[...] = jnp.zeros_like(n_st)
        m_st[...] = jnp.zeros_like(m_st)

    xpad[SUBLANES:SUBLANES + rows, :] = mqk_ref[0].astype(F32)
    acc = jnp.broadcast_to(convb_ref[...], (rows, 2 * MLSTM_WIDTH))
    for t in range(CONV_WIDTH):
        acc = acc + convw_ref[t:t + 1, :] * xpad[pl.ds(SUBLANES - (CONV_WIDTH - 1) + t, rows), :]
    xpad[0:SUBLANES, :] = xpad[rows:rows + SUBLANES, :]
    qk = acc * jax.nn.sigmoid(acc)
    q_scr[...] = qk[:, :MLSTM_WIDTH].astype(BF16)
    k_scr[...] = qk[:, MLSTM_WIDTH:] * (d ** -0.5)

    gates = gate_ref[0] + gbias_ref[...]
    gate_scr[...] = gates
    logf_scr[...] = jax.nn.log_sigmoid(gates)

    ri = lax.broadcasted_iota(I32, (CHUNK, CHUNK), 0)
    ci = lax.broadcasted_iota(I32, (CHUNK, CHUNK), 1)
    causal = ci <= ri
    tri = causal.astype(BF16)

    def chunk_body(c, carry):
        r0 = pl.multiple_of(c * CHUNK, CHUNK)
        g = gate_scr[pl.ds(r0, CHUNK), :]
        lf = logf_scr[pl.ds(r0, CHUNK), :]
        l1, l2, l3 = _split3(lf)
        bcol = _dot(tri, l1) + _dot(tri, l2) + _dot(tri, l3)
        g_t = g.T
        b_t = bcol.T
        for h in range(MLSTM_HEADS):
            hs = slice(h * d, (h + 1) * d)
            ic = g[:, h:h + 1]
            bc = bcol[:, MLSTM_HEADS + h:MLSTM_HEADS + h + 1]
            ir = g_t[h:h + 1, :]
            br = b_t[MLSTM_HEADS + h:MLSTM_HEADS + h + 1, :]
            m_prev = m_st[h:h + 1, 0:1]
            n_prev = n_st[h:h + 1, :]
            c_prev = c_st[h]
            qh = q_scr[pl.ds(r0, CHUNK), hs]
            kh = k_scr[pl.ds(r0, CHUNK), hs]
            vh = mv_ref[0, pl.ds(r0, CHUNK), hs]

            log_d = jnp.where(causal, bc - br + ir, NEG_BIG)
            inter = bc + m_prev
            m_t = jnp.maximum(inter, jnp.max(log_d, -1, keepdims=True))
            d_mat = jnp.exp(log_d - m_t)
            w_inter = jnp.exp(inter - m_t)
            qk_d = _dot_nt(qh, kh.astype(BF16)) * d_mat
            num = w_inter * _dot(qh, c_prev.astype(BF16)) + _dot(qk_d.astype(BF16), vh)
            den = (w_inter * jnp.sum(qh.astype(F32) * n_prev, -1, keepdims=True)
                   + jnp.sum(qk_d, -1, keepdims=True))
            hh = num / jnp.maximum(jnp.abs(den), jnp.exp(-m_t))

            b_last = bc[CHUNK - 1:CHUNK, :]
            log_in = b_last - bc + ic
            m_new = jnp.maximum(b_last + m_prev, jnp.max(log_in, 0, keepdims=True))
            w_prev = jnp.exp(b_last + m_prev - m_new)
            kw = kh * jnp.exp(log_in - m_new)
            c_st[h] = w_prev * c_prev + _dot_tn(kw.astype(BF16), vh)
            n_st[h:h + 1, :] = w_prev * n_prev + jnp.sum(kw, 0, keepdims=True)
            m_st[h:h + 1, :] = jnp.broadcast_to(m_new, (1, LANES))

            mu = jnp.mean(hh, -1, keepdims=True)
            hc = hh - mu
            var = jnp.mean(hc * hc, -1, keepdims=True)
            hn = hc * lax.rsqrt(var + LN_EPS) * ng_ref[:, hs]
            og = jax.nn.sigmoid(mo_ref[0, pl.ds(r0, CHUNK), hs].astype(F32))
            out_ref[0, pl.ds(r0, CHUNK), hs] = (og * hn).astype(BF16)
        return carry

    lax.fori_loop(0, rows // CHUNK, chunk_body, 0)


def _mlstm(mqk, mv, mo, gates, conv_w, conv_b, i_bias, f_bias, norm_g, batch, seq):
    w2 = 2 * MLSTM_WIDTH
    gbias = jnp.pad(jnp.concatenate([i_bias, f_bias]).astype(F32), (0, LANES - 2 * MLSTM_HEADS)).reshape(1, LANES)
    blk = lambda w: pl.BlockSpec((1, MLSTM_BLOCK, w), lambda b, i: (b, i, 0))
    return pl.pallas_call(
        _mlstm_kernel,
        grid=(batch, seq // MLSTM_BLOCK),
        in_specs=[blk(w2), blk(MLSTM_WIDTH), blk(MLSTM_WIDTH), blk(LANES),
                  _full((CONV_WIDTH, w2)), _full((1, w2)), _full((1, LANES)), _full((1, MLSTM_WIDTH))],
        out_specs=blk(MLSTM_WIDTH),
        out_shape=jax.ShapeDtypeStruct((batch, seq, MLSTM_WIDTH), BF16),
        scratch_shapes=[pltpu.VMEM((MLSTM_BLOCK + SUBLANES, w2), F32),
                        pltpu.VMEM((MLSTM_BLOCK, MLSTM_WIDTH), BF16),
                        pltpu.VMEM((MLSTM_BLOCK, MLSTM_WIDTH), F32),
                        pltpu.VMEM((MLSTM_BLOCK, LANES), F32),
                        pltpu.VMEM((MLSTM_BLOCK, LANES), F32),
                        pltpu.VMEM((MLSTM_HEADS, MLSTM_HEAD_DIM, MLSTM_HEAD_DIM), F32),
                        pltpu.VMEM((SUBLANES, MLSTM_HEAD_DIM), F32),
                        pltpu.VMEM((SUBLANES, LANES), F32)],
        compiler_params=_params("arbitrary", "arbitrary"),
        name="mlstm",
    )(mqk.reshape(batch, seq, w2), mv.reshape(batch, seq, MLSTM_WIDTH), mo.reshape(batch, seq, MLSTM_WIDTH),
      gates.reshape(batch, seq, LANES), conv_w.astype(F32), conv_b.reshape(1, w2).astype(F32), gbias,
      norm_g.reshape(1, MLSTM_WIDTH).astype(F32))


def _outproj_kernel(att_ref, hm_ref, h_ref, wa_ref, wm_ref, g_ref, b_ref, o_ref):
    y = _dot(att_ref[...], wa_ref[...]) + _dot(hm_ref[...], wm_ref[...])
    o_ref[...] = _layer_norm(DEEPNORM_ALPHA * h_ref[...] + y, g_ref[...], b_ref[...])


def _outproj(att, hm, h, w_out, g, b):
    n, d = h.shape
    wa = w_out[:ATT_WIDTH].astype(BF16)
    wm = w_out[ATT_WIDTH:].astype(BF16)
    rows = lambda w: pl.BlockSpec((ROW_BLOCK, w), lambda i: (i, 0))
    return pl.pallas_call(
        _outproj_kernel,
        grid=(n // ROW_BLOCK,),
        in_specs=[rows(ATT_WIDTH), rows(MLSTM_WIDTH), rows(d), _full(wa.shape), _full(wm.shape),
                  _full((1, d)), _full((1, d))],
        out_specs=rows(d),
        out_shape=jax.ShapeDtypeStruct((n, d), F32),
        compiler_params=_params("arbitrary"),
        name="outproj",
    )(att, hm, h, wa, wm, g.reshape(1, d), b.reshape(1, d))


def _kvproj_kernel(mem_ref, w_ref, k_ref, v_ref):
    kv = _dot(mem_ref[...].astype(BF16), w_ref[...])
    d = k_ref.shape[-1]
    k_ref[...] = kv[:, :d].astype(BF16)
    v_ref[...] = kv[:, d:].astype(BF16)


def _kvproj(mem2, w_kv):
    n, d = mem2.shape
    w = w_kv.astype(BF16)
    blk = min(ROW_BLOCK, n)
    rows = pl.BlockSpec((blk, d), lambda i: (i, 0))
    return pl.pallas_call(
        _kvproj_kernel,
        grid=(n // blk,),
        in_specs=[rows, _full(w.shape)],
        out_specs=[rows, rows],
        out_shape=[jax.ShapeDtypeStruct((n, d), BF16)] * 2,
        compiler_params=_params("arbitrary"),
        name="kvproj",
    )(mem2, w)


def _xattn_kernel(h_ref, k_ref, v_ref, wq_ref, wo_ref, g_ref, b_ref, o_ref):
    h = h_ref[0]
    d = h.shape[-1]
    dh = d // XATT_HEADS
    q = (_dot(h.astype(BF16), wq_ref[...]) * (dh ** -0.5)).astype(BF16)
    outs = []
    for hd in range(XATT_HEADS):
        sl = slice(hd * dh, (hd + 1) * dh)
        s = _dot_nt(q[:, sl], k_ref[0, :, sl])
        m = jnp.max(s, -1, keepdims=True)
        p = jnp.exp(s - m)
        l = jnp.sum(p, -1, keepdims=True)
        outs.append((_dot(p.astype(BF16), v_ref[0, :, sl]) / l).astype(BF16))
    y = _dot(jnp.concatenate(outs, axis=-1), wo_ref[...])
    o_ref[0] = _layer_norm(DEEPNORM_ALPHA * h + y, g_ref[...], b_ref[...])


def _xattn(h3, k3, v3, w_q, w_o, g, b):
    batch, seq, d = h3.shape
    m = k3.shape[1]
    wq = w_q.astype(BF16)
    wo = w_o.astype(BF16)
    blk = pl.BlockSpec((1, ROW_BLOCK, d), lambda bb, i: (bb, i, 0))
    mem = pl.BlockSpec((1, m, d), lambda bb, i: (bb, 0, 0))
    return pl.pallas_call(
        _xattn_kernel,
        grid=(batch, seq // ROW_BLOCK),
        in_specs=[blk, mem, mem, _full(wq.shape), _full(wo.shape), _full((1, d)), _full((1, d))],
        out_specs=blk,
        out_shape=jax.ShapeDtypeStruct((batch, seq, d), F32),
        compiler_params=_params("arbitrary", "arbitrary"),
        name="xattn",
    )(h3, k3, v3, wq, wo, g.reshape(1, d), b.reshape(1, d))


def _topk_rows(s, k, big):
    iota = lax.broadcasted_iota(I32, s.shape, 0)
    vals, idxs = [], []
    for _ in range(k):
        m = jnp.max(s, axis=0, keepdims=True)
        sel = jnp.min(jnp.where(s == m, iota, big), axis=0, keepdims=True)
        vals.append(m)
        idxs.append(sel)
        s = jnp.where(iota == sel, -jnp.inf, s)
    return vals, idxs


def _stack_rows(rows_list):
    k = len(rows_list)
    t = rows_list[0].shape[-1]
    iota = lax.broadcasted_iota(I32, (k, t), 0)
    out = jnp.broadcast_to(rows_list[0], (k, t))
    for r in range(1, k):
        out = jnp.where(iota == r, jnp.broadcast_to(rows_list[r], (k, t)), out)
    return out


def _peer_topk_kernel(h_ref, wq_ref, khi_ref, klo_ref, row_ref, hbit_ref, g_ref, e_scr, g_scr):
    t = h_ref.shape[0]
    kk = PEER_TOPK
    q = _dot(h_ref[...].astype(BF16), wq_ref[...])
    half8 = kk // 2
    sub = lax.broadcasted_iota(I32, (half8, t), 0)
    for hh in range(PEER_HEADS):
        tops = []
        for p in range(2):
            c0 = (hh * 2 + p) * N_KEYS
            qhi, qlo = _split2(q[:, c0:c0 + N_KEYS])
            s = _dot_nt(khi_ref[p], qhi) + _dot_nt(khi_ref[p], qlo) + _dot_nt(klo_ref[p], qhi)
            vals, idxs = _topk_rows(s, kk, N_KEYS)
            tops.append((_stack_rows(vals), _stack_rows(idxs)))
        (s0, i0), (s1, i1) = tops
        e0 = i0 * N_KEYS
        cand, cexp, cflat = [], [], []
        for b in range(half8):
            cand.append(s0[:half8] + s1[b:b + 1])
            cexp.append(e0[:half8] + i1[b:b + 1])
            cflat.append(sub * kk + b)
        cand.append(s0[half8:] + s1[0:1])
        cexp.append(e0[half8:] + i1[0:1])
        cflat.append((sub + half8) * kk)
        cand.append(s0[0:1] + s1[half8:])
        cexp.append(e0[0:1] + i1[half8:])
        cflat.append(sub + half8)
        cand = jnp.concatenate(cand, axis=0)
        cexp = jnp.concatenate(cexp, axis=0)
        cflat = jnp.concatenate(cflat, axis=0)
        best_s, best_e = [], []
        for _ in range(kk):
            m = jnp.max(cand, axis=0, keepdims=True)
            jsel = jnp.min(jnp.where(cand == m, cflat, kk * kk), axis=0, keepdims=True)
            hit = cflat == jsel
            best_s.append(m)
            best_e.append(jnp.max(jnp.where(hit, cexp, 0), axis=0, keepdims=True))
            cand = jnp.where(hit, -jnp.inf, cand)
        bs = _stack_rows(best_s)
        ex = jnp.exp(bs - bs[0:1])
        g_scr[hh * kk:(hh + 1) * kk, :] = ex / jnp.sum(ex, axis=0, keepdims=True)
        e_scr[hh * kk:(hh + 1) * kk, :] = _stack_rows(best_e)
    e = e_scr[...]
    hb = e >> (PEER_HALF_EXPERTS.bit_length() - 1)
    row_ref[...] = ((e & (PEER_HALF_EXPERTS - 1)) * SUBLANES).T
    hbit_ref[...] = hb.astype(F32).T
    g_ref[...] = g_scr[...].T


def _peer_topk(h2, w_query, sub_keys):
    n, d = h2.shape
    wq = w_query.astype(BF16)
    khi = sub_keys.astype(BF16)
    klo = (sub_keys - khi.astype(F32)).astype(BF16)
    rows = pl.BlockSpec((TOPK_BLOCK, PEER_PICKS), lambda i: (i, 0))
    return pl.pallas_call(
        _peer_topk_kernel,
        grid=(n // TOPK_BLOCK,),
        in_specs=[pl.BlockSpec((TOPK_BLOCK, d), lambda i: (i, 0)), _full(wq.shape), _full(khi.shape), _full(klo.shape)],
        out_specs=[rows, rows, rows],
        out_shape=[jax.ShapeDtypeStruct((n, PEER_PICKS), I32),
                   jax.ShapeDtypeStruct((n, PEER_PICKS), F32), jax.ShapeDtypeStruct((n, PEER_PICKS), F32)],
        scratch_shapes=[pltpu.VMEM((PEER_PICKS, TOPK_BLOCK), I32), pltpu.VMEM((PEER_PICKS, TOPK_BLOCK), F32)],
        compiler_params=_params("arbitrary"),
        name="peer_topk",
    )(h2, wq, khi, klo)


def _pack_table(tab):
    e, d = tab.shape
    bits = lax.bitcast_convert_type(tab.astype(BF16), jnp.uint16).astype(U32)
    packed = bits[:e // 2] | (bits[e // 2:] << 16)
    return packed.reshape(e // 2 * (d // LANES), LANES)


def _table_tile(tab_ref, row8):
    return pltpu.bitcast(tab_ref[pl.ds(pl.multiple_of(row8, SUBLANES), SUBLANES), :], BF16)


def _peer_u_consts():
    k = jnp.arange(2 * LANES)
    col = jnp.arange(2 * PEER_PICKS)
    row = jnp.arange(PEER_PICKS)
    fold = ((k[None, :] // 16 == jnp.arange(2 * U_TILE_PICKS)[:, None] // 2)
            & (k[None, :] % 2 == jnp.arange(2 * U_TILE_PICKS)[:, None] % 2)).astype(BF16)
    lanesum = (((col[None, :] // 2) % 2) == (k[:, None] // LANES)).astype(BF16)
    place = ((col[None, :] // 64 == row[:, None] // 32) & ((col[None, :] % 64) // 4 == (row[:, None] % 32) // 2)
             & (col[None, :] % 2 == row[:, None] % 2)).astype(F32)
    dup = ((col[None, :] // 2) == jnp.arange(PEER_PICKS)[:, None]).astype(BF16)
    return fold, lanesum, place, dup


def _peer_u_kernel(row_ref, x_ref, g_ref, hbit_ref, tab_ref, fold_ref, lanesum_ref, place_ref, dup_ref,
                   coefh_ref, res_scr):
    tb = x_ref.shape[0]
    ntiles = PEER_PICKS // (2 * U_TILE_PICKS)

    def tokens(i, carry):
        for u in range(PEER_UNROLL):
            t = i * PEER_UNROLL + u
            x = x_ref[t]
            xbits = pltpu.bitcast(x.astype(BF16).astype(F32), U32)
            xw = pltpu.bitcast(xbits | (xbits >> 16), BF16)
            for j in range(ntiles):
                cols = []
                for ab in range(2):
                    prods = []
                    for mm in range(U_TILE_PICKS):
                        c = j * 2 * U_TILE_PICKS + 2 * mm + ab
                        prods.append(_table_tile(tab_ref, row_ref[t, c]) * xw)
                    cols.append(jnp.concatenate(prods, axis=0))
                res_scr[t, j] = _dot(fold_ref[...], jnp.concatenate(cols, axis=1))
        return carry

    lax.fori_loop(0, tb // PEER_UNROLL, tokens, 0)

    r = res_scr[...].reshape(tb * PEER_PICKS, 2 * LANES).astype(BF16)
    s = _dot(r, lanesum_ref[...])
    a2 = jnp.sum(s.reshape(tb, PEER_PICKS, 2 * PEER_PICKS) * place_ref[...], axis=1)
    ghi, glo = _split2(g_ref[...])
    g2 = _dot(ghi, dup_ref[...]) + _dot(glo, dup_ref[...])
    hbit2 = _dot(hbit_ref[...].astype(BF16), dup_ref[...])
    parity = (lax.broadcasted_iota(I32, a2.shape, 1) % 2).astype(F32)
    coef = g2 * (0.5 * a2 * (1.0 + lax.erf(a2 * math.sqrt(0.5))))
    coefh_ref[...] = jnp.where(hbit2 == parity, coef, 0.0).astype(BF16)


def _peer_v_consts():
    col = jnp.arange(2 * PEER_PICKS * SUBLANES)
    ab, p, r = col // (PEER_PICKS * SUBLANES), (col // 16) % (PEER_PICKS // 2), col % 16
    src = jnp.arange(2 * PEER_PICKS)
    expand = ((src[:, None] // 2 == (2 * p + ab)[None, :]) & (src[:, None] % 2 == (r % 2)[None, :])).astype(BF16)
    kk = jnp.arange(PEER_PICKS * SUBLANES)
    diag = ((kk[None, :] % 16) // 2 == (jnp.arange(16) % SUBLANES)[:, None]).astype(F32)
    return expand, diag


def _peer_v_kernel(row_ref, coefh_ref, x_ref, tab_ref, expand_ref, diag_ref, g_ref, b_ref, o_ref, lrow_scr, y_scr):
    tb = x_ref.shape[0]
    kdim = PEER_PICKS * SUBLANES
    lrow_scr[...] = _dot(coefh_ref[...], expand_ref[...])
    diag = diag_ref[...]

    def tokens(i, carry):
        for u in range(PEER_UNROLL):
            t = i * PEER_UNROLL + u
            even = jnp.broadcast_to(lrow_scr[pl.ds(t, 1), 0:kdim], (SUBLANES, kdim))
            odd = jnp.broadcast_to(lrow_scr[pl.ds(t, 1), kdim:2 * kdim], (SUBLANES, kdim))
            lhs = (jnp.concatenate([even, odd], axis=0) * diag).astype(BF16)
            tiles = []
            for p in range(PEER_PICKS // 2):
                wa = _table_tile(tab_ref, row_ref[t, 2 * p])
                wb = _table_tile(tab_ref, row_ref[t, 2 * p + 1])
                tiles.append(jnp.concatenate([wa, wb], axis=1))
            res = _dot(lhs, jnp.concatenate(tiles, axis=0))
            y_scr[t] = res[0:SUBLANES, 0:LANES] + res[SUBLANES:, LANES:]
        return carry

    lax.fori_loop(0, tb // PEER_UNROLL, tokens, 0)
    z = DEEPNORM_ALPHA * x_ref[...] + y_scr[...]
    cnt = z.shape[1] * z.shape[2]
    mu = jnp.sum(jnp.sum(z, axis=2, keepdims=True), axis=1, keepdims=True) / cnt
    zc = z - mu
    var = jnp.sum(jnp.sum(zc * zc, axis=2, keepdims=True), axis=1, keepdims=True) / cnt
    o_ref[...] = zc * lax.rsqrt(var + LN_EPS) * g_ref[...] + b_ref[...]


def _peer_ffn(h2, row, hbit, gate, expert_u, expert_v, ln_g, ln_b):
    n, d = h2.shape
    rows8 = d // LANES
    assert rows8 == SUBLANES and PEER_PICKS == LANES and expert_u.shape[0] == 2 * PEER_HALF_EXPERTS
    x3 = h2.reshape(n, rows8, LANES)
    u_pk = _pack_table(expert_u)
    v_pk = _pack_table(expert_v)
    smem = pl.BlockSpec((PEER_BLOCK, PEER_PICKS), lambda i: (i, 0), memory_space=pltpu.SMEM)
    picks = pl.BlockSpec((PEER_BLOCK, PEER_PICKS), lambda i: (i, 0))
    picks2 = pl.BlockSpec((PEER_BLOCK, 2 * PEER_PICKS), lambda i: (i, 0))
    tok = pl.BlockSpec((PEER_BLOCK, rows8, LANES), lambda i: (i, 0, 0))
    table = pl.BlockSpec(u_pk.shape, lambda i: (0, 0), pipeline_mode=pl.Buffered(1))
    u_consts = _peer_u_consts()
    coefh = pl.pallas_call(
        _peer_u_kernel,
        grid=(n // PEER_BLOCK,),
        in_specs=[smem, tok, picks, picks, table] + [_full(c.shape) for c in u_consts],
        out_specs=picks2,
        out_shape=jax.ShapeDtypeStruct((n, 2 * PEER_PICKS), BF16),
        scratch_shapes=[pltpu.VMEM((PEER_BLOCK, PEER_PICKS // (2 * U_TILE_PICKS), 2 * U_TILE_PICKS, 2 * LANES), F32)],
        compiler_params=_params("arbitrary"),
        name="peer_u",
    )(row, x3, gate, hbit, u_pk, *u_consts)
    v_consts = _peer_v_consts()
    out = pl.pallas_call(
        _peer_v_kernel,
        grid=(n // PEER_BLOCK,),
        in_specs=[smem, picks2, tok, table] + [_full(c.shape) for c in v_consts]
        + [_full((rows8, LANES)), _full((rows8, LANES))],
        out_specs=tok,
        out_shape=jax.ShapeDtypeStruct((n, rows8, LANES), F32),
        scratch_shapes=[pltpu.VMEM((PEER_BLOCK, 2 * PEER_PICKS * SUBLANES), F32),
                        pltpu.VMEM((PEER_BLOCK, rows8, LANES), F32)],
        compiler_params=_params("arbitrary"),
        name="peer_v",
    )(row, coefh, x3, v_pk, *v_consts, ln_g.reshape(rows8, LANES), ln_b.reshape(rows8, LANES))
    return out.reshape(n, d)


def kernel(x, mem, ln_in_g, ln_in_b, w_in, conv_w, conv_b, mlstm_i_bias, mlstm_f_bias, mlstm_norm_g, rel_bias, w_out, ln1_g, ln1_b, xattn_w_q, xattn_w_kv, xattn_w_o, ln2_g, ln2_b, peer_w_query, peer_sub_keys, peer_u, peer_v, ln3_g, ln3_b):
    batch, seq, d = x.shape
    n = batch * seq
    assert w_in.shape[0] == DEPTH
    h, qkv, mqk, mv, mo, gates = _ln_inproj(x.reshape(n, d), ln_in_g, ln_in_b, w_in[0])
    for l in range(DEPTH):
        if l > 0:
            raise NotImplementedError("input projection of deeper layers")
        att = _attention(qkv, rel_bias[l], batch, seq)
        hm = _mlstm(mqk, mv, mo, gates, conv_w[l], conv_b[l], mlstm_i_bias[l], mlstm_f_bias[l], mlstm_norm_g[l],
                    batch, seq)
        h = _outproj(att.reshape(n, ATT_WIDTH), hm.reshape(n, MLSTM_WIDTH), h, w_out[l], ln1_g[l], ln1_b[l])
        k2, v2 = _kvproj(mem.reshape(-1, d), xattn_w_kv[l])
        m = mem.shape[1]
        h = _xattn(h.reshape(batch, seq, d), k2.reshape(batch, m, d), v2.reshape(batch, m, d),
                   xattn_w_q[l], xattn_w_o[l], ln2_g[l], ln2_b[l]).reshape(n, d)
        row, hbit, gate = _peer_topk(h, peer_w_query[l], peer_sub_keys[l])
        h = _peer_ffn(h, row, hbit, gate, peer_u[l], peer_v[l], ln3_g[l], ln3_b[l])
    return h.reshape(batch, seq, d)
```

```python
import functools
import math

import jax
import jax.numpy as jnp
from jax import lax
from jax.experimental import pallas as pl
from jax.experimental.pallas import tpu as pltpu

F32, BF16, I32, U32 = jnp.float32, jnp.bfloat16, jnp.int32, jnp.uint32

DEPTH = 1
LN_EPS = 1e-5
DEEPNORM_ALPHA = (2.0 * DEPTH) ** 0.25
CHUNK = 64
ATT_HEADS, ATT_HEAD_DIM, ATT_LEFT_CHUNKS, MAX_REL = 8, 64, 8, 128
ATT_WIDTH = ATT_HEADS * ATT_HEAD_DIM
MLSTM_HEADS, MLSTM_HEAD_DIM, CONV_WIDTH = 4, 128, 4
MLSTM_WIDTH = MLSTM_HEADS * MLSTM_HEAD_DIM
XATT_HEADS = 4
PEER_HEADS, N_KEYS, PEER_TOPK = 8, 128, 16
PEER_PICKS = PEER_HEADS * PEER_TOPK
PEER_HALF_EXPERTS = N_KEYS * N_KEYS // 2
U_TILE_PICKS = 16
PEER_UNROLL = 8

LANES = 128
SUBLANES = 8
VMEM_LIMIT_BYTES = 56 * 1024 * 1024

NEG_BIG = -1e30

ROW_BLOCK = 512
ATT_BLOCK = 256
ATT_KEY_BLOCKS = 3
MLSTM_BLOCK = 256
TOPK_BLOCK = 256
PEER_BLOCK = 64


def _params(*semantics):
    return pltpu.CompilerParams(dimension_semantics=semantics, vmem_limit_bytes=VMEM_LIMIT_BYTES)


def _layer_norm(x, g, b):
    mu = jnp.mean(x, -1, keepdims=True)
    xc = x - mu
    var = jnp.mean(xc * xc, -1, keepdims=True)
    return xc * lax.rsqrt(var + LN_EPS) * g + b


def _split2(x):
    hi = x.astype(BF16)
    lo = (x - hi.astype(F32)).astype(BF16)
    return hi, lo


def _split3(x):
    hi = x.astype(BF16)
    r = x - hi.astype(F32)
    mid = r.astype(BF16)
    lo = (r - mid.astype(F32)).astype(BF16)
    return hi, mid, lo


def _dot(a, b):
    return jnp.dot(a, b, preferred_element_type=F32)


def _dot_nt(a, b):
    return lax.dot_general(a, b, (((1,), (1,)), ((), ())), preferred_element_type=F32)


def _dot_tn(a, b):
    return lax.dot_general(a, b, (((0,), (0,)), ((), ())), preferred_element_type=F32)


def _full(shape):
    return pl.BlockSpec(shape, lambda *_: (0,) * len(shape))


def _ln_inproj_kernel(x_ref, g_ref, b_ref, wqkv_ref, wmqk_ref, wmv_ref, wmo_ref, wghi_ref, wglo_ref,
                      h_ref, qkv_ref, mqk_ref, mv_ref, mo_ref, gate_ref):
    h = _layer_norm(x_ref[...], g_ref[...], b_ref[...])
    h_ref[...] = h
    hb, hlo = _split2(h)
    qkv_ref[...] = _dot(hb, wqkv_ref[...]).astype(BF16)
    mqk_ref[...] = _dot(hb, wmqk_ref[...]).astype(BF16)
    mv_ref[...] = _dot(hb, wmv_ref[...]).astype(BF16)
    mo_ref[...] = _dot(hb, wmo_ref[...]).astype(BF16)
    gate_ref[...] = _dot(hb, wghi_ref[...]) + _dot(hlo, wghi_ref[...]) + _dot(hb, wglo_ref[...])


def _ln_inproj(x2, g, b, w_in):
    n, d = x2.shape
    a3 = 3 * ATT_WIDTH
    wqkv = w_in[:, :a3].astype(BF16)
    wmqk = w_in[:, a3:a3 + 2 * MLSTM_WIDTH].astype(BF16)
    wmv = w_in[:, a3 + 2 * MLSTM_WIDTH:a3 + 3 * MLSTM_WIDTH].astype(BF16)
    wmo = w_in[:, a3 + 3 * MLSTM_WIDTH:a3 + 4 * MLSTM_WIDTH].astype(BF16)
    wg = jnp.pad(w_in[:, a3 + 4 * MLSTM_WIDTH:], ((0, 0), (0, LANES - 2 * MLSTM_HEADS)))
    wghi = wg.astype(BF16)
    wglo = (wg - wghi.astype(F32)).astype(BF16)
    rows = lambda w: pl.BlockSpec((ROW_BLOCK, w), lambda i: (i, 0))
    return pl.pallas_call(
        _ln_inproj_kernel,
        grid=(n // ROW_BLOCK,),
        in_specs=[rows(d), _full((1, d)), _full((1, d)), _full(wqkv.shape), _full(wmqk.shape),
                  _full(wmv.shape), _full(wmo.shape), _full(wghi.shape), _full(wglo.shape)],
        out_specs=[rows(d), rows(a3), rows(2 * MLSTM_WIDTH), rows(MLSTM_WIDTH), rows(MLSTM_WIDTH), rows(LANES)],
        out_shape=[jax.ShapeDtypeStruct((n, d), F32), jax.ShapeDtypeStruct((n, a3), BF16),
                   jax.ShapeDtypeStruct((n, 2 * MLSTM_WIDTH), BF16), jax.ShapeDtypeStruct((n, MLSTM_WIDTH), BF16),
                   jax.ShapeDtypeStruct((n, MLSTM_WIDTH), BF16), jax.ShapeDtypeStruct((n, LANES), F32)],
        compiler_params=_params("arbitrary"),
        name="ln_inproj",
    )(x2, g.reshape(1, d), b.reshape(1, d), wqkv, wmqk, wmv, wmo, wghi, wglo)


def _attn_kernel(q_ref, k0_ref, k1_ref, k2_ref, v0_ref, v1_ref, v2_ref, bias_ref, o_ref):
    i = pl.program_id(1)
    nkeys = ATT_KEY_BLOCKS * ATT_BLOCK
    q = q_ref[0] * (ATT_HEAD_DIM ** -0.5)
    kcat = jnp.concatenate([k0_ref[0], k1_ref[0], k2_ref[0]], axis=0)
    vcat = jnp.concatenate([v0_ref[0], v1_ref[0], v2_ref[0]], axis=0)
    col = lax.broadcasted_iota(I32, (1, nkeys), 1)
    in_seq = col >= (ATT_KEY_BLOCKS - 1 - i) * ATT_BLOCK
    lane = lax.broadcasted_iota(I32, (1, LANES), 1)
    low = lane < ATT_HEAD_DIM
    outs = []
    for pair in range(ATT_HEADS // 2):
        sl = slice(pair * LANES, (pair + 1) * LANES)
        qp, kp, vp = q[:, sl], kcat[:, sl], vcat[:, sl]
        halves = []
        for half in range(2):
            keep = low if half == 0 else jnp.logical_not(low)
            qh = jnp.where(keep, qp, jnp.zeros_like(qp))
            s = _dot_nt(qh, kp) + bias_ref[2 * pair + half]
            s = jnp.where(in_seq, s, NEG_BIG)
            m = jnp.max(s, -1, keepdims=True)
            p = jnp.exp(s - m)
            l = jnp.sum(p, -1, keepdims=True)
            halves.append(_dot(p.astype(BF16), vp) / l)
        outs.append(jnp.where(low, halves[0], halves[1]))
    o_ref[0] = jnp.concatenate(outs, axis=-1).astype(BF16)


def _attn_bias(rel_bias):
    nq, nk = ATT_BLOCK, ATT_KEY_BLOCKS * ATT_BLOCK
    r = jnp.arange(nq)[:, None]
    j = jnp.arange(nk)[None, :]
    allowed = (j // CHUNK >= r // CHUNK) & (j // CHUNK <= r // CHUNK + ATT_LEFT_CHUNKS)
    period = nq + nk - 1
    i = jnp.arange(period)
    d = jnp.where(i < nk, i, i - period)
    rel = d - (ATT_KEY_BLOCKS - 1) * ATT_BLOCK
    line = rel_bias[:, jnp.clip(rel, -MAX_REL, MAX_REL) + MAX_REL].astype(F32)
    rep = jnp.tile(line, (1, nq + 1))[:, :nq * (period - 1)]
    tab = rep.reshape(-1, nq, period - 1)[:, :, :nk]
    return jnp.where(allowed[None], tab, NEG_BIG)


def _attention(qkv, rel_bias, batch, seq):
    assert ATT_LEFT_CHUNKS * CHUNK == (ATT_KEY_BLOCKS - 1) * ATT_BLOCK
    qkv3 = qkv.reshape(batch, seq, 3 * ATT_WIDTH)
    bias = _attn_bias(rel_bias)
    blk = (1, ATT_BLOCK, ATT_WIDTH)

    def kv_spec(col, j):
        return pl.BlockSpec(blk, lambda b, i: (b, jnp.maximum(i - (ATT_KEY_BLOCKS - 1) + j, 0), col))

    return pl.pallas_call(
        _attn_kernel,
        grid=(batch, seq // ATT_BLOCK),
        in_specs=[pl.BlockSpec(blk, lambda b, i: (b, i, 0))]
        + [kv_spec(1, j) for j in range(ATT_KEY_BLOCKS)] + [kv_spec(2, j) for j in range(ATT_KEY_BLOCKS)]
        + [_full(bias.shape)],
        out_specs=pl.BlockSpec(blk, lambda b, i: (b, i, 0)),
        out_shape=jax.ShapeDtypeStruct((batch, seq, ATT_WIDTH), BF16),
        compiler_params=_params("arbitrary", "arbitrary"),
        name="attn",
    )(qkv3, qkv3, qkv3, qkv3, qkv3, qkv3, qkv3, bias)


def _mlstm_kernel(mqk_ref, mv_ref, mo_ref, gate_ref, convw_ref, convb_ref, gbias_ref, ng_ref, out_ref,
                  xpad, q_scr, k_scr, gate_scr, logf_scr, c_st, n_st, m_st):
    j = pl.program_id(1)
    rows = MLSTM_BLOCK
    d = MLSTM_HEAD_DIM

    @pl.when(j == 0)
    def _():
        xpad[0:SUBLANES, :] = jnp.zeros((SUBLANES, 2 * MLSTM_WIDTH), F32)
        c_st[...] = jnp.zeros_like(c_st)
        n_st[...] = jnp.zeros_like(n_st)
        m_st[...] = jnp.zeros_like(m_st)

    xpad[SUBLANES:SUBLANES + rows, :] = mqk_ref[0].astype(F32)
    acc = jnp.broadcast_to(convb_ref[...], (rows, 2 * MLSTM_WIDTH))
    for t in range(CONV_WIDTH):
        acc = acc + convw_ref[t:t + 1, :] * xpad[pl.ds(SUBLANES - (CONV_WIDTH - 1) + t, rows), :]
    xpad[0:SUBLANES, :] = xpad[rows:rows + SUBLANES, :]
    qk = acc * jax.nn.sigmoid(acc)
    q_scr[...] = qk[:, :MLSTM_WIDTH].astype(BF16)
    k_scr[...] = qk[:, MLSTM_WIDTH:] * (d ** -0.5)

    gates = gate_ref[0] + gbias_ref[...]
    gate_scr[...] = gates
    logf_scr[...] = jax.nn.log_sigmoid(gates)

    ri = lax.broadcasted_iota(I32, (CHUNK, CHUNK), 0)
    ci = lax.broadcasted_iota(I32, (CHUNK, CHUNK), 1)
    causal = ci <= ri
    tri = causal.astype(BF16)

    def chunk_body(c, carry):
        r0 = pl.multiple_of(c * CHUNK, CHUNK)
        g = gate_scr[pl.ds(r0, CHUNK), :]
        lf = logf_scr[pl.ds(r0, CHUNK), :]
        l1, l2, l3 = _split3(lf)
        bcol = _dot(tri, l1) + _dot(tri, l2) + _dot(tri, l3)
        g_t = g.T
        b_t = bcol.T
        for h in range(MLSTM_HEADS):
            hs = slice(h * d, (h + 1) * d)
            ic = g[:, h:h + 1]
            bc = bcol[:, MLSTM_HEADS + h:MLSTM_HEADS + h + 1]
            ir = g_t[h:h + 1, :]
            br = b_t[MLSTM_HEADS + h:MLSTM_HEADS + h + 1, :]
            m_prev = m_st[h:h + 1, 0:1]
            n_prev = n_st[h:h + 1, :]
            c_prev = c_st[h]
            qh = q_scr[pl.ds(r0, CHUNK), hs]
            kh = k_scr[pl.ds(r0, CHUNK), hs]
            vh = mv_ref[0, pl.ds(r0, CHUNK), hs]

            log_d = jnp.where(causal, bc - br + ir, NEG_BIG)
            inter = bc + m_prev
            m_t = jnp.maximum(inter, jnp.max(log_d, -1, keepdims=True))
            d_mat = jnp.exp(log_d - m_t)
            w_inter = jnp.exp(inter - m_t)
            qk_d = _dot_nt(qh, kh.astype(BF16)) * d_mat
            num = w_inter * _dot(qh, c_prev.astype(BF16)) + _dot(qk_d.astype(BF16), vh)
            den = (w_inter * jnp.sum(qh.astype(F32) * n_prev, -1, keepdims=True)
                   + jnp.sum(qk_d, -1, keepdims=True))
            hh = num / jnp.maximum(jnp.abs(den), jnp.exp(-m_t))

            b_last = bc[CHUNK - 1:CHUNK, :]
            log_in = b_last - bc + ic
            m_new = jnp.maximum(b_last + m_prev, jnp.max(log_in, 0, keepdims=True))
            w_prev = jnp.exp(b_last + m_prev - m_new)
            kw = kh * jnp.exp(log_in - m_new)
            c_st[h] = w_prev * c_prev + _dot_tn(kw.astype(BF16), vh)
            n_st[h:h + 1, :] = w_prev * n_prev + jnp.sum(kw, 0, keepdims=True)
            m_st[h:h + 1, :] = jnp.broadcast_to(m_new, (1, LANES))

            mu = jnp.mean(hh, -1, keepdims=True)
            hc = hh - mu
            var = jnp.mean(hc * hc, -1, keepdims=True)
            hn = hc * lax.rsqrt(var + LN_EPS) * ng_ref[:, hs]
            og = jax.nn.sigmoid(mo_ref[0, pl.ds(r0, CHUNK), hs].astype(F32))
            out_ref[0, pl.ds(r0, CHUNK), hs] = (og * hn).astype(BF16)
        return carry

    lax.fori_loop(0, rows // CHUNK, chunk_body, 0)


def _mlstm(mqk, mv, mo, gates, conv_w, conv_b, i_bias, f_bias, norm_g, batch, seq):
    w2 = 2 * MLSTM_WIDTH
    gbias = jnp.pad(jnp.concatenate([i_bias, f_bias]).astype(F32), (0, LANES - 2 * MLSTM_HEADS)).reshape(1, LANES)
    blk = lambda w: pl.BlockSpec((1, MLSTM_BLOCK, w), lambda b, i: (b, i, 0))
    return pl.pallas_call(
        _mlstm_kernel,
        grid=(batch, seq // MLSTM_BLOCK),
        in_specs=[blk(w2), blk(MLSTM_WIDTH), blk(MLSTM_WIDTH), blk(LANES),
                  _full((CONV_WIDTH, w2)), _full((1, w2)), _full((1, LANES)), _full((1, MLSTM_WIDTH))],
        out_specs=blk(MLSTM_WIDTH),
        out_shape=jax.ShapeDtypeStruct((batch, seq, MLSTM_WIDTH), BF16),
        scratch_shapes=[pltpu.VMEM((MLSTM_BLOCK + SUBLANES, w2), F32),
                        pltpu.VMEM((MLSTM_BLOCK, MLSTM_WIDTH), BF16),
                        pltpu.VMEM((MLSTM_BLOCK, MLSTM_WIDTH), F32),
                        pltpu.VMEM((MLSTM_BLOCK, LANES), F32),
                        pltpu.VMEM((MLSTM_BLOCK, LANES), F32),
                        pltpu.VMEM((MLSTM_HEADS, MLSTM_HEAD_DIM, MLSTM_HEAD_DIM), F32),
                        pltpu.VMEM((SUBLANES, MLSTM_HEAD_DIM), F32),
                        pltpu.VMEM((SUBLANES, LANES), F32)],
        compiler_params=_params("arbitrary", "arbitrary"),
        name="mlstm",
    )(mqk.reshape(batch, seq, w2), mv.reshape(batch, seq, MLSTM_WIDTH), mo.reshape(batch, seq, MLSTM_WIDTH),
      gates.reshape(batch, seq, LANES), conv_w.astype(F32), conv_b.reshape(1, w2).astype(F32), gbias,
      norm_g.reshape(1, MLSTM_WIDTH).astype(F32))


def _outproj_kernel(att_ref, hm_ref, h_ref, wa_ref, wm_ref, g_ref, b_ref, o_ref):
    y = _dot(att_ref[...], wa_ref[...]) + _dot(hm_ref[...], wm_ref[...])
    o_ref[...] = _layer_norm(DEEPNORM_ALPHA * h_ref[...] + y, g_ref[...], b_ref[...])


def _outproj(att, hm, h, w_out, g, b):
    n, d = h.shape
    wa = w_out[:ATT_WIDTH].astype(BF16)
    wm = w_out[ATT_WIDTH:].astype(BF16)
    rows = lambda w: pl.BlockSpec((ROW_BLOCK, w), lambda i: (i, 0))
    return pl.pallas_call(
        _outproj_kernel,
        grid=(n // ROW_BLOCK,),
        in_specs=[rows(ATT_WIDTH), rows(MLSTM_WIDTH), rows(d), _full(wa.shape), _full(wm.shape),
                  _full((1, d)), _full((1, d))],
        out_specs=rows(d),
        out_shape=jax.ShapeDtypeStruct((n, d), F32),
        compiler_params=_params("arbitrary"),
        name="outproj",
    )(att, hm, h, wa, wm, g.reshape(1, d), b.reshape(1, d))


def _kvproj_kernel(mem_ref, w_ref, k_ref, v_ref):
    kv = _dot(mem_ref[...].astype(BF16), w_ref[...])
    d = k_ref.shape[-1]
    k_ref[...] = kv[:, :d].astype(BF16)
    v_ref[...] = kv[:, d:].astype(BF16)


def _kvproj(mem2, w_kv):
    n, d = mem2.shape
    w = w_kv.astype(BF16)
    blk = min(ROW_BLOCK, n)
    rows = pl.BlockSpec((blk, d), lambda i: (i, 0))
    return pl.pallas_call(
        _kvproj_kernel,
        grid=(n // blk,),
        in_specs=[rows, _full(w.shape)],
        out_specs=[rows, rows],
        out_shape=[jax.ShapeDtypeStruct((n, d), BF16)] * 2,
        compiler_params=_params("arbitrary"),
        name="kvproj",
    )(mem2, w)


def _xattn_kernel(h_ref, k_ref, v_ref, wq_ref, wo_ref, g_ref, b_ref, o_ref):
    h = h_ref[0]
    d = h.shape[-1]
    dh = d // XATT_HEADS
    q = (_dot(h.astype(BF16), wq_ref[...]) * (dh ** -0.5)).astype(BF16)
    outs = []
    for hd in range(XATT_HEADS):
        sl = slice(hd * dh, (hd + 1) * dh)
        s = _dot_nt(q[:, sl], k_ref[0, :, sl])
        m = jnp.max(s, -1, keepdims=True)
        p = jnp.exp(s - m)
        l = jnp.sum(p, -1, keepdims=True)
        outs.append((_dot(p.astype(BF16), v_ref[0, :, sl]) / l).astype(BF16))
    y = _dot(jnp.concatenate(outs, axis=-1), wo_ref[...])
    o_ref[0] = _layer_norm(DEEPNORM_ALPHA * h + y, g_ref[...], b_ref[...])


def _xattn(h3, k3, v3, w_q, w_o, g, b):
    batch, seq, d = h3.shape
    m = k3.shape[1]
    wq = w_q.astype(BF16)
    wo = w_o.astype(BF16)
    blk = pl.BlockSpec((1, ROW_BLOCK, d), lambda bb, i: (bb, i, 0))
    mem = pl.BlockSpec((1, m, d), lambda bb, i: (bb, 0, 0))
    return pl.pallas_call(
        _xattn_kernel,
        grid=(batch, seq // ROW_BLOCK),
        in_specs=[blk, mem, mem, _full(wq.shape), _full(wo.shape), _full((1, d)), _full((1, d))],
        out_specs=blk,
        out_shape=jax.ShapeDtypeStruct((batch, seq, d), F32),
        compiler_params=_params("arbitrary", "arbitrary"),
        name="xattn",
    )(h3, k3, v3, wq, wo, g.reshape(1, d), b.reshape(1, d))


def _topk_rows(s, k, big):
    iota = lax.broadcasted_iota(I32, s.shape, 0)
    vals, idxs = [], []
    for _ in range(k):
        m = jnp.max(s, axis=0, keepdims=True)
        sel = jnp.min(jnp.where(s == m, iota, big), axis=0, keepdims=True)
        vals.append(m)
        idxs.append(sel)
        s = jnp.where(iota == sel, -jnp.inf, s)
    return vals, idxs


def _stack_rows(rows_list):
    k = len(rows_list)
    t = rows_list[0].shape[-1]
    iota = lax.broadcasted_iota(I32, (k, t), 0)
    out = jnp.broadcast_to(rows_list[0], (k, t))
    for r in range(1, k):
        out = jnp.where(iota == r, jnp.broadcast_to(rows_list[r], (k, t)), out)
    return out


def _peer_topk_kernel(h_ref, wq_ref, khi_ref, klo_ref, row_ref, hbit_ref, g_ref, e_scr, g_scr):
    t = h_ref.shape[0]
    kk = PEER_TOPK
    q = _dot(h_ref[...].astype(BF16), wq_ref[...])
    half8 = kk // 2
    sub = lax.broadcasted_iota(I32, (half8, t), 0)
    for hh in range(PEER_HEADS):
        tops = []
        for p in range(2):
            c0 = (hh * 2 + p) * N_KEYS
            qhi, qlo = _split2(q[:, c0:c0 + N_KEYS])
            s = _dot_nt(khi_ref[p], qhi) + _dot_nt(khi_ref[p], qlo) + _dot_nt(klo_ref[p], qhi)
            vals, idxs = _topk_rows(s, kk, N_KEYS)
            tops.append((_stack_rows(vals), _stack_rows(idxs)))
        (s0, i0), (s1, i1) = tops
        e0 = i0 * N_KEYS
        cand, cexp, cflat = [], [], []
        for b in range(half8):
            cand.append(s0[:half8] + s1[b:b + 1])
            cexp.append(e0[:half8] + i1[b:b + 1])
            cflat.append(sub * kk + b)
        cand.append(s0[half8:] + s1[0:1])
        cexp.append(e0[half8:] + i1[0:1])
        cflat.append((sub + half8) * kk)
        cand.append(s0[0:1] + s1[half8:])
        cexp.append(e0[0:1] + i1[half8:])
        cflat.append(sub + half8)
        cand = jnp.concatenate(cand, axis=0)
        cexp = jnp.concatenate(cexp, axis=0)
        cflat = jnp.concatenate(cflat, axis=0)
        best_s, best_e = [], []
        for _ in range(kk):
            m = jnp.max(cand, axis=0, keepdims=True)
            jsel = jnp.min(jnp.where(cand == m, cflat, kk * kk), axis=0, keepdims=True)
            hit = cflat == jsel
            best_s.append(m)
            best_e.append(jnp.max(jnp.where(hit, cexp, 0), axis=0, keepdims=True))
            cand = jnp.where(hit, -jnp.inf, cand)
        bs = _stack_rows(best_s)
        ex = jnp.exp(bs - bs[0:1])
        g_scr[hh * kk:(hh + 1) * kk, :] = ex / jnp.sum(ex, axis=0, keepdims=True)
        e_scr[hh * kk:(hh + 1) * kk, :] = _stack_rows(best_e)
    e = e_scr[...]
    hb = e >> (PEER_HALF_EXPERTS.bit_length() - 1)
    row_ref[...] = ((e & (PEER_HALF_EXPERTS - 1)) * SUBLANES).T
    hbit_ref[...] = hb.astype(F32).T
    g_ref[...] = g_scr[...].T


def _peer_topk(h2, w_query, sub_keys):
    n, d = h2.shape
    wq = w_query.astype(BF16)
    khi = sub_keys.astype(BF16)
    klo = (sub_keys - khi.astype(F32)).astype(BF16)
    rows = pl.BlockSpec((TOPK_BLOCK, PEER_PICKS), lambda i: (i, 0))
    return pl.pallas_call(
        _peer_topk_kernel,
        grid=(n // TOPK_BLOCK,),
        in_specs=[pl.BlockSpec((TOPK_BLOCK, d), lambda i: (i, 0)), _full(wq.shape), _full(khi.shape), _full(klo.shape)],
        out_specs=[rows, rows, rows],
        out_shape=[jax.ShapeDtypeStruct((n, PEER_PICKS), I32),
                   jax.ShapeDtypeStruct((n, PEER_PICKS), F32), jax.ShapeDtypeStruct((n, PEER_PICKS), F32)],
        scratch_shapes=[pltpu.VMEM((PEER_PICKS, TOPK_BLOCK), I32), pltpu.VMEM((PEER_PICKS, TOPK_BLOCK), F32)],
        compiler_params=_params("arbitrary"),
        name="peer_topk",
    )(h2, wq, khi, klo)


def _pack_table(tab):
    e, d = tab.shape
    bits = lax.bitcast_convert_type(tab.astype(BF16), jnp.uint16).astype(U32)
    packed = bits[:e // 2] | (bits[e // 2:] << 16)
    return packed.reshape(e // 2 * (d // LANES), LANES)


def _table_tile(tab_ref, row8):
    return pltpu.bitcast(tab_ref[pl.ds(pl.multiple_of(row8, SUBLANES), SUBLANES), :], BF16)


def _peer_u_consts():
    k = jnp.arange(2 * LANES)
    col = jnp.arange(2 * PEER_PICKS)
    row = jnp.arange(PEER_PICKS)
    fold = ((k[None, :] // 16 == jnp.arange(2 * U_TILE_PICKS)[:, None] // 2)
            & (k[None, :] % 2 == jnp.arange(2 * U_TILE_PICKS)[:, None] % 2)).astype(BF16)
    lanesum = (((col[None, :] // 2) % 2) == (k[:, None] // LANES)).astype(BF16)
    place = ((col[None, :] // 64 == row[:, None] // 32) & ((col[None, :] % 64) // 4 == (row[:, None] % 32) // 2)
             & (col[None, :] % 2 == row[:, None] % 2)).astype(F32)
    dup = ((col[None, :] // 2) == jnp.arange(PEER_PICKS)[:, None]).astype(BF16)
    return fold, lanesum, place, dup


def _index_copy(row_hbm, idx_smem, sem, batch, slot):
    return pltpu.make_async_copy(row_hbm.at[pl.ds(batch * PEER_UNROLL, PEER_UNROLL)], idx_smem.at[slot], sem.at[slot])


def _for_each_token_batch(row_hbm, idx_smem, sem, block_tokens, body):
    step = pl.program_id(0)
    per_step = block_tokens // PEER_UNROLL
    assert per_step % 2 == 0
    total = pl.num_programs(0) * per_step

    @pl.when(step == 0)
    def _():
        _index_copy(row_hbm, idx_smem, sem, 0, 0).start()

    def pair(k, carry):
        for slot in range(2):
            local = 2 * k + slot
            batch = step * per_step + local
            _index_copy(row_hbm, idx_smem, sem, batch, slot).wait()

            @pl.when(batch + 1 < total)
            def _():
                _index_copy(row_hbm, idx_smem, sem, batch + 1, 1 - slot).start()

            body(local * PEER_UNROLL, idx_smem.at[slot])
        return carry

    lax.fori_loop(0, per_step // 2, pair, 0)


def _peer_u_kernel(row_hbm, x_ref, g_ref, hbit_ref, tab_ref, fold_ref, lanesum_ref, place_ref, dup_ref,
                   coefh_ref, res_scr, idx_smem, sem):
    tb = x_ref.shape[0]
    ntiles = PEER_PICKS // (2 * U_TILE_PICKS)

    def tokens(t0, idx):
        for u in range(PEER_UNROLL):
            t = t0 + u
            x = x_ref[t]
            xbits = pltpu.bitcast(x.astype(BF16).astype(F32), U32)
            xw = pltpu.bitcast(xbits | (xbits >> 16), BF16)
            for j in range(ntiles):
                cols = []
                for ab in range(2):
                    prods = []
                    for mm in range(U_TILE_PICKS):
                        c = j * 2 * U_TILE_PICKS + 2 * mm + ab
                        prods.append(_table_tile(tab_ref, idx[u, c]) * xw)
                    cols.append(jnp.concatenate(prods, axis=0))
                res_scr[t, j] = _dot(fold_ref[...], jnp.concatenate(cols, axis=1))

    _for_each_token_batch(row_hbm, idx_smem, sem, tb, tokens)

    r = res_scr[...].reshape(tb * PEER_PICKS, 2 * LANES).astype(BF16)
    s = _dot(r, lanesum_ref[...])
    a2 = jnp.sum(s.reshape(tb, PEER_PICKS, 2 * PEER_PICKS) * place_ref[...], axis=1)
    ghi, glo = _split2(g_ref[...])
    g2 = _dot(ghi, dup_ref[...]) + _dot(glo, dup_ref[...])
    hbit2 = _dot(hbit_ref[...].astype(BF16), dup_ref[...])
    parity = (lax.broadcasted_iota(I32, a2.shape, 1) % 2).astype(F32)
    coef = g2 * (0.5 * a2 * (1.0 + lax.erf(a2 * math.sqrt(0.5))))
    coefh_ref[...] = jnp.where(hbit2 == parity, coef, 0.0).astype(BF16)


def _peer_v_consts():
    col = jnp.arange(2 * PEER_PICKS * SUBLANES)
    ab, p, r = col // (PEER_PICKS * SUBLANES), (col // 16) % (PEER_PICKS // 2), col % 16
    src = jnp.arange(2 * PEER_PICKS)
    expand = ((src[:, None] // 2 == (2 * p + ab)[None, :]) & (src[:, None] % 2 == (r % 2)[None, :])).astype(BF16)
    kk = jnp.arange(PEER_PICKS * SUBLANES)
    diag = ((kk[None, :] % 16) // 2 == (jnp.arange(16) % SUBLANES)[:, None]).astype(F32)
    return expand, diag


def _peer_v_kernel(row_hbm, coefh_ref, x_ref, tab_ref, expand_ref, diag_ref, g_ref, b_ref, o_ref, lrow_scr, y_scr,
                   idx_smem, sem):
    tb = x_ref.shape[0]
    kdim = PEER_PICKS * SUBLANES
    lrow_scr[...] = _dot(coefh_ref[...], expand_ref[...])
    diag = diag_ref[...]

    def tokens(t0, idx):
        for u in range(PEER_UNROLL):
            t = t0 + u
            even = jnp.broadcast_to(lrow_scr[pl.ds(t, 1), 0:kdim], (SUBLANES, kdim))
            odd = jnp.broadcast_to(lrow_scr[pl.ds(t, 1), kdim:2 * kdim], (SUBLANES, kdim))
            lhs = (jnp.concatenate([even, odd], axis=0) * diag).astype(BF16)
            tiles = []
            for p in range(PEER_PICKS // 2):
                wa = _table_tile(tab_ref, idx[u, 2 * p])
                wb = _table_tile(tab_ref, idx[u, 2 * p + 1])
                tiles.append(jnp.concatenate([wa, wb], axis=1))
            res = _dot(lhs, jnp.concatenate(tiles, axis=0))
            y_scr[t] = res[0:SUBLANES, 0:LANES] + res[SUBLANES:, LANES:]

    _for_each_token_batch(row_hbm, idx_smem, sem, tb, tokens)
    z = DEEPNORM_ALPHA * x_ref[...] + y_scr[...]
    cnt = z.shape[1] * z.shape[2]
    mu = jnp.sum(jnp.sum(z, axis=2, keepdims=True), axis=1, keepdims=True) / cnt
    zc = z - mu
    var = jnp.sum(jnp.sum(zc * zc, axis=2, keepdims=True), axis=1, keepdims=True) / cnt
    o_ref[...] = zc * lax.rsqrt(var + LN_EPS) * g_ref[...] + b_ref[...]


def _peer_ffn(h2, row, hbit, gate, expert_u, expert_v, ln_g, ln_b):
    n, d = h2.shape
    rows8 = d // LANES
    assert rows8 == SUBLANES and PEER_PICKS == LANES and expert_u.shape[0] == 2 * PEER_HALF_EXPERTS
    x3 = h2.reshape(n, rows8, LANES)
    u_pk = _pack_table(expert_u)
    v_pk = _pack_table(expert_v)
    hbm = pl.BlockSpec(memory_space=pl.ANY)
    staging = [pltpu.SMEM((2, PEER_UNROLL, PEER_PICKS), I32), pltpu.SemaphoreType.DMA((2,))]
    picks =pl.BlockSpec((PEER_BLOCK, PEER_PICKS), lambda i: (i, 0))
    picks2 = pl.BlockSpec((PEER_BLOCK, 2 * PEER_PICKS), lambda i: (i, 0))
    tok = pl.BlockSpec((PEER_BLOCK, rows8, LANES), lambda i: (i, 0, 0))
    table = pl.BlockSpec(u_pk.shape, lambda i: (0, 0), pipeline_mode=pl.Buffered(1))
    u_consts = _peer_u_consts()
    coefh = pl.pallas_call(
        _peer_u_kernel,
        grid=(n // PEER_BLOCK,),
        in_specs=[hbm, tok, picks, picks, table] + [_full(c.shape) for c in u_consts],
        out_specs=picks2,
        out_shape=jax.ShapeDtypeStruct((n, 2 * PEER_PICKS), BF16),
        scratch_shapes=[pltpu.VMEM((PEER_BLOCK, PEER_PICKS // (2 * U_TILE_PICKS), 2 * U_TILE_PICKS, 2 * LANES), F32)]
        + staging,
        compiler_params=_params("arbitrary"),
        name="peer_u",
    )(row, x3, gate, hbit, u_pk, *u_consts)
    v_consts = _peer_v_consts()
    out = pl.pallas_call(
        _peer_v_kernel,
        grid=(n // PEER_BLOCK,),
        in_specs=[hbm, picks2, tok, table] + [_full(c.shape) for c in v_consts]
        + [_full((rows8, LANES)), _full((rows8, LANES))],
        out_specs=tok,
        out_shape=jax.ShapeDtypeStruct((n, rows8, LANES), F32),
        scratch_shapes=[pltpu.VMEM((PEER_BLOCK, 2 * PEER_PICKS * SUBLANES), F32),
                        pltpu.VMEM((PEER_BLOCK, rows8, LANES), F32)] + staging,
        compiler_params=_params("arbitrary"),
        name="peer_v",
    )(row, coefh, x3, v_pk, *v_consts, ln_g.reshape(rows8, LANES), ln_b.reshape(rows8, LANES))
    return out.reshape(n, d)


def kernel(x, mem, ln_in_g, ln_in_b, w_in, conv_w, conv_b, mlstm_i_bias, mlstm_f_bias, mlstm_norm_g, rel_bias, w_out, ln1_g, ln1_b, xattn_w_q, xattn_w_kv, xattn_w_o, ln2_g, ln2_b, peer_w_query, peer_sub_keys, peer_u, peer_v, ln3_g, ln3_b):
    batch, seq, d = x.shape
    n = batch * seq
    assert w_in.shape[0] == DEPTH
    h, qkv, mqk, mv, mo, gates = _ln_inproj(x.reshape(n, d), ln_in_g, ln_in_b, w_in[0])
    for l in range(DEPTH):
        if l > 0:
            raise NotImplementedError("input projection of deeper layers")
        att = _attention(qkv, rel_bias[l], batch, seq)
        hm = _mlstm(mqk, mv, mo, gates, conv_w[l], conv_b[l], mlstm_i_bias[l], mlstm_f_bias[l], mlstm_norm_g[l],
                    batch, seq)
        h = _outproj(att.reshape(n, ATT_WIDTH), hm.reshape(n, MLSTM_WIDTH), h, w_out[l], ln1_g[l], ln1_b[l])
        k2, v2 = _kvproj(mem.reshape(-1, d), xattn_w_kv[l])
        m = mem.shape[1]
        h = _xattn(h.reshape(batch, seq, d), k2.reshape(batch, m, d), v2.reshape(batch, m, d),
                   xattn_w_q[l], xattn_w_o[l], ln2_g[l], ln2_b[l]).reshape(n, d)
        row, hbit, gate = _peer_topk(h, peer_w_query[l], peer_sub_keys[l])
        h = _peer_ffn(h, row, hbit, gate, peer_u[l], peer_v[l], ln3_g[l], ln3_b[l])
    return h.reshape(batch, seq, d)
```

```python
import functools
import math

import jax
import jax.numpy as jnp
from jax import lax
from jax.experimental import pallas as pl
from jax.experimental.pallas import tpu as pltpu

F32, BF16, I32, U32 = jnp.float32, jnp.bfloat16, jnp.int32, jnp.uint32

DEPTH = 1
LN_EPS = 1e-5
DEEPNORM_ALPHA = (2.0 * DEPTH) ** 0.25
CHUNK = 64
ATT_HEADS, ATT_HEAD_DIM, ATT_LEFT_CHUNKS, MAX_REL = 8, 64, 8, 128
ATT_WIDTH = ATT_HEADS * ATT_HEAD_DIM
MLSTM_HEADS, MLSTM_HEAD_DIM, CONV_WIDTH = 4, 128, 4
MLSTM_WIDTH = MLSTM_HEADS * MLSTM_HEAD_DIM
XATT_HEADS = 4
PEER_HEADS, N_KEYS, PEER_TOPK = 8, 128, 16
PEER_PICKS = PEER_HEADS * PEER_TOPK
PEER_HALF_EXPERTS = N_KEYS * N_KEYS // 2
U_TILE_PICKS = 16
PEER_UNROLL = 16

LANES = 128
SUBLANES = 8
VMEM_LIMIT_BYTES = 56 * 1024 * 1024

NEG_BIG = -1e30

ROW_BLOCK = 512
ATT_BLOCK = 256
ATT_KEY_BLOCKS = 3
MLSTM_BLOCK = 256
TOPK_BLOCK = 256
PEER_BLOCK = 64


def _params(*semantics):
    return pltpu.CompilerParams(dimension_semantics=semantics, vmem_limit_bytes=VMEM_LIMIT_BYTES)


def _layer_norm(x, g, b):
    mu = jnp.mean(x, -1, keepdims=True)
    xc = x - mu
    var = jnp.mean(xc * xc, -1, keepdims=True)
    return xc * lax.rsqrt(var + LN_EPS) * g + b


def _split2(x):
    hi = x.astype(BF16)
    lo = (x - hi.astype(F32)).astype(BF16)
    return hi, lo


def _split3(x):
    hi = x.astype(BF16)
    r = x - hi.astype(F32)
    mid = r.astype(BF16)
    lo = (r - mid.astype(F32)).astype(BF16)
    return hi, mid, lo


def _dot(a, b):
    return jnp.dot(a, b, preferred_element_type=F32)


def _dot_nt(a, b):
    return lax.dot_general(a, b, (((1,), (1,)), ((), ())), preferred_element_type=F32)


def _dot_tn(a, b):
    return lax.dot_general(a, b, (((0,), (0,)), ((), ())), preferred_element_type=F32)


def _full(shape):
    return pl.BlockSpec(shape, lambda *_: (0,) * len(shape))


def _ln_inproj_kernel(x_ref, g_ref, b_ref, wqkv_ref, wmqk_ref, wmv_ref, wmo_ref, wghi_ref, wglo_ref,
                      h_ref, qkv_ref, mqk_ref, mv_ref, mo_ref, gate_ref):
    h = _layer_norm(x_ref[...], g_ref[...], b_ref[...])
    h_ref[...] = h
    hb, hlo = _split2(h)
    qkv_ref[...] = _dot(hb, wqkv_ref[...]).astype(BF16)
    mqk_ref[...] = _dot(hb, wmqk_ref[...]).astype(BF16)
    mv_ref[...] = _dot(hb, wmv_ref[...]).astype(BF16)
    mo_ref[...] = _dot(hb, wmo_ref[...]).astype(BF16)
    gate_ref[...] = _dot(hb, wghi_ref[...]) + _dot(hlo, wghi_ref[...]) + _dot(hb, wglo_ref[...])


def _ln_inproj(x2, g, b, w_in):
    n, d = x2.shape
    a3 = 3 * ATT_WIDTH
    wqkv = w_in[:, :a3].astype(BF16)
    wmqk = w_in[:, a3:a3 + 2 * MLSTM_WIDTH].astype(BF16)
    wmv = w_in[:, a3 + 2 * MLSTM_WIDTH:a3 + 3 * MLSTM_WIDTH].astype(BF16)
    wmo = w_in[:, a3 + 3 * MLSTM_WIDTH:a3 + 4 * MLSTM_WIDTH].astype(BF16)
    wg = jnp.pad(w_in[:, a3 + 4 * MLSTM_WIDTH:], ((0, 0), (0, LANES - 2 * MLSTM_HEADS)))
    wghi = wg.astype(BF16)
    wglo = (wg - wghi.astype(F32)).astype(BF16)
    rows = lambda w: pl.BlockSpec((ROW_BLOCK, w), lambda i: (i, 0))
    return pl.pallas_call(
        _ln_inproj_kernel,
        grid=(n // ROW_BLOCK,),
        in_specs=[rows(d), _full((1, d)), _full((1, d)), _full(wqkv.shape), _full(wmqk.shape),
                  _full(wmv.shape), _full(wmo.shape), _full(wghi.shape), _full(wglo.shape)],
        out_specs=[rows(d), rows(a3), rows(2 * MLSTM_WIDTH), rows(MLSTM_WIDTH), rows(MLSTM_WIDTH), rows(LANES)],
        out_shape=[jax.ShapeDtypeStruct((n, d), F32), jax.ShapeDtypeStruct((n, a3), BF16),
                   jax.ShapeDtypeStruct((n, 2 * MLSTM_WIDTH), BF16), jax.ShapeDtypeStruct((n, MLSTM_WIDTH), BF16),
                   jax.ShapeDtypeStruct((n, MLSTM_WIDTH), BF16), jax.ShapeDtypeStruct((n, LANES), F32)],
        compiler_params=_params("arbitrary"),
        name="ln_inproj",
    )(x2, g.reshape(1, d), b.reshape(1, d), wqkv, wmqk, wmv, wmo, wghi, wglo)


def _attn_kernel(q_ref, k0_ref, k1_ref, k2_ref, v0_ref, v1_ref, v2_ref, bias_ref, o_ref):
    i = pl.program_id(1)
    nkeys = ATT_KEY_BLOCKS * ATT_BLOCK
    q = q_ref[0] * (ATT_HEAD_DIM ** -0.5)
    kcat = jnp.concatenate([k0_ref[0], k1_ref[0], k2_ref[0]], axis=0)
    vcat = jnp.concatenate([v0_ref[0], v1_ref[0], v2_ref[0]], axis=0)
    col = lax.broadcasted_iota(I32, (1, nkeys), 1)
    in_seq = col >= (ATT_KEY_BLOCKS - 1 - i) * ATT_BLOCK
    lane = lax.broadcasted_iota(I32, (1, LANES), 1)
    low = lane < ATT_HEAD_DIM
    outs = []
    for pair in range(ATT_HEADS // 2):
        sl = slice(pair * LANES, (pair + 1) * LANES)
        qp, kp, vp = q[:, sl], kcat[:, sl], vcat[:, sl]
        halves = []
        for half in range(2):
            keep = low if half == 0 else jnp.logical_not(low)
            qh = jnp.where(keep, qp, jnp.zeros_like(qp))
            s = _dot_nt(qh, kp) + bias_ref[2 * pair + half]
            s = jnp.where(in_seq, s, NEG_BIG)
            m = jnp.max(s, -1, keepdims=True)
            p = jnp.exp(s - m)
            l = jnp.sum(p, -1, keepdims=True)
            halves.append(_dot(p.astype(BF16), vp) / l)
        outs.append(jnp.where(low, halves[0], halves[1]))
    o_ref[0] = jnp.concatenate(outs, axis=-1).astype(BF16)


def _attn_bias(rel_bias):
    nq, nk = ATT_BLOCK, ATT_KEY_BLOCKS * ATT_BLOCK
    r = jnp.arange(nq)[:, None]
    j = jnp.arange(nk)[None, :]
    allowed = (j // CHUNK >= r // CHUNK) & (j // CHUNK <= r // CHUNK + ATT_LEFT_CHUNKS)
    period = nq + nk - 1
    i = jnp.arange(period)
    d = jnp.where(i < nk, i, i - period)
    rel = d - (ATT_KEY_BLOCKS - 1) * ATT_BLOCK
    line = rel_bias[:, jnp.clip(rel, -MAX_REL, MAX_REL) + MAX_REL].astype(F32)
    rep = jnp.tile(line, (1, nq + 1))[:, :nq * (period - 1)]
    tab = rep.reshape(-1, nq, period - 1)[:, :, :nk]
    return jnp.where(allowed[None], tab, NEG_BIG)


def _attention(qkv, rel_bias, batch, seq):
    assert ATT_LEFT_CHUNKS * CHUNK == (ATT_KEY_BLOCKS - 1) * ATT_BLOCK
    qkv3 = qkv.reshape(batch, seq, 3 * ATT_WIDTH)
    bias = _attn_bias(rel_bias)
    blk = (1, ATT_BLOCK, ATT_WIDTH)

    def kv_spec(col, j):
        return pl.BlockSpec(blk, lambda b, i: (b, jnp.maximum(i - (ATT_KEY_BLOCKS - 1) + j, 0), col))

    return pl.pallas_call(
        _attn_kernel,
        grid=(batch, seq // ATT_BLOCK),
        in_specs=[pl.BlockSpec(blk, lambda b, i: (b, i, 0))]
        + [kv_spec(1, j) for j in range(ATT_KEY_BLOCKS)] + [kv_spec(2, j) for j in range(ATT_KEY_BLOCKS)]
        + [_full(bias.shape)],
        out_specs=pl.BlockSpec(blk, lambda b, i: (b, i, 0)),
        out_shape=jax.ShapeDtypeStruct((batch, seq, ATT_WIDTH), BF16),
        compiler_params=_params("arbitrary", "arbitrary"),
        name="attn",
    )(qkv3, qkv3, qkv3, qkv3, qkv3, qkv3, qkv3, bias)


def _mlstm_kernel(mqk_ref, mv_ref, mo_ref, gate_ref, convw_ref, convb_ref, gbias_ref, ng_ref, out_ref,
                  xpad, q_scr, k_scr, gate_scr, logf_scr, c_st, n_st, m_st):
    j = pl.program_id(1)
    rows = MLSTM_BLOCK
    d = MLSTM_HEAD_DIM

    @pl.when(j == 0)
    def _():
        xpad[0:SUBLANES, :] = jnp.zeros((SUBLANES, 2 * MLSTM_WIDTH), F32)
        c_st[...] = jnp.zeros_like(c_st)
        n_st[...] = jnp.zeros_like(n_st)
        m_st[...] = jnp.zeros_like(m_st)

    xpad[SUBLANES:SUBLANES + rows, :] = mqk_ref[0].astype(F32)
    acc = jnp.broadcast_to(convb_ref[...], (rows, 2 * MLSTM_WIDTH))
    for t in range(CONV_WIDTH):
        acc = acc + convw_ref[t:t + 1, :] * xpad[pl.ds(SUBLANES - (CONV_WIDTH - 1) + t, rows), :]
    xpad[0:SUBLANES, :] = xpad[rows:rows + SUBLANES, :]
    qk = acc * jax.nn.sigmoid(acc)
    q_scr[...] = qk[:, :MLSTM_WIDTH].astype(BF16)
    k_scr[...] = qk[:, MLSTM_WIDTH:] * (d ** -0.5)

    gates = gate_ref[0] + gbias_ref[...]
    gate_scr[...] = gates
    logf_scr[...] = jax.nn.log_sigmoid(gates)

    ri = lax.broadcasted_iota(I32, (CHUNK, CHUNK), 0)
    ci = lax.broadcasted_iota(I32, (CHUNK, CHUNK), 1)
    causal = ci <= ri
    tri = causal.astype(BF16)

    def chunk_body(c, carry):
        r0 = pl.multiple_of(c * CHUNK, CHUNK)
        g = gate_scr[pl.ds(r0, CHUNK), :]
        lf = logf_scr[pl.ds(r0, CHUNK), :]
        l1, l2, l3 = _split3(lf)
        bcol = _dot(tri, l1) + _dot(tri, l2) + _dot(tri, l3)
        g_t = g.T
        b_t = bcol.T
        m_all = m_st[...]
        n_all = n_st[...]
        c_all = [c_st[h] for h in range(MLSTM_HEADS)]
        c_new, n_new, m_new_rows = [], [], []
        for h in range(MLSTM_HEADS):
            hs = slice(h * d, (h + 1) * d)
            ic = g[:, h:h + 1]
            bc = bcol[:, MLSTM_HEADS + h:MLSTM_HEADS + h + 1]
            ir = g_t[h:h + 1, :]
            br = b_t[MLSTM_HEADS + h:MLSTM_HEADS + h + 1, :]
            m_prev = m_all[h:h + 1, 0:1]
            n_prev = n_all[h:h + 1, :]
            c_prev = c_all[h]
            qh = q_scr[pl.ds(r0, CHUNK), hs]
            kh = k_scr[pl.ds(r0, CHUNK), hs]
            vh = mv_ref[0, pl.ds(r0, CHUNK), hs]

            log_d = jnp.where(causal, bc - br + ir, NEG_BIG)
            inter = bc + m_prev
            m_t = jnp.maximum(inter, jnp.max(log_d, -1, keepdims=True))
            d_mat = jnp.exp(log_d - m_t)
            w_inter = jnp.exp(inter - m_t)
            qk_d = _dot_nt(qh, kh.astype(BF16)) * d_mat
            num = w_inter * _dot(qh, c_prev.astype(BF16)) + _dot(qk_d.astype(BF16), vh)
            den = (w_inter * jnp.sum(qh.astype(F32) * n_prev, -1, keepdims=True)
                   + jnp.sum(qk_d, -1, keepdims=True))
            hh = num / jnp.maximum(jnp.abs(den), jnp.exp(-m_t))

            b_last = bc[CHUNK - 1:CHUNK, :]
            log_in = b_last - bc + ic
            m_new = jnp.maximum(b_last + m_prev, jnp.max(log_in, 0, keepdims=True))
            w_prev = jnp.exp(b_last + m_prev - m_new)
            kw = kh * jnp.exp(log_in - m_new)
            c_new.append(w_prev * c_prev + _dot_tn(kw.astype(BF16), vh))
            n_new.append(w_prev * n_prev + jnp.sum(kw, 0, keepdims=True))
            m_new_rows.append(jnp.broadcast_to(m_new, (1, LANES)))

            mu = jnp.mean(hh, -1, keepdims=True)
            hc = hh - mu
            var = jnp.mean(hc * hc, -1, keepdims=True)
            hn = hc * lax.rsqrt(var + LN_EPS) * ng_ref[:, hs]
            og = jax.nn.sigmoid(mo_ref[0, pl.ds(r0, CHUNK), hs].astype(F32))
            out_ref[0, pl.ds(r0, CHUNK), hs] = (og * hn).astype(BF16)
        for h in range(MLSTM_HEADS):
            c_st[h] = c_new[h]
            n_st[h:h + 1, :] = n_new[h]
            m_st[h:h + 1, :] = m_new_rows[h]
        return carry

    lax.fori_loop(0, rows // CHUNK, chunk_body, 0)


def _mlstm(mqk, mv, mo, gates, conv_w, conv_b, i_bias, f_bias, norm_g, batch, seq):
    w2 = 2 * MLSTM_WIDTH
    gbias = jnp.pad(jnp.concatenate([i_bias, f_bias]).astype(F32), (0, LANES - 2 * MLSTM_HEADS)).reshape(1, LANES)
    blk = lambda w: pl.BlockSpec((1, MLSTM_BLOCK, w), lambda b, i: (b, i, 0))
    return pl.pallas_call(
        _mlstm_kernel,
        grid=(batch, seq // MLSTM_BLOCK),
        in_specs=[blk(w2), blk(MLSTM_WIDTH), blk(MLSTM_WIDTH), blk(LANES),
                  _full((CONV_WIDTH, w2)), _full((1, w2)), _full((1, LANES)), _full((1, MLSTM_WIDTH))],
        out_specs=blk(MLSTM_WIDTH),
        out_shape=jax.ShapeDtypeStruct((batch, seq, MLSTM_WIDTH), BF16),
        scratch_shapes=[pltpu.VMEM((MLSTM_BLOCK + SUBLANES, w2), F32),
                        pltpu.VMEM((MLSTM_BLOCK, MLSTM_WIDTH), BF16),
                        pltpu.VMEM((MLSTM_BLOCK, MLSTM_WIDTH), F32),
                        pltpu.VMEM((MLSTM_BLOCK, LANES), F32),
                        pltpu.VMEM((MLSTM_BLOCK, LANES), F32),
                        pltpu.VMEM((MLSTM_HEADS, MLSTM_HEAD_DIM, MLSTM_HEAD_DIM), F32),
                        pltpu.VMEM((SUBLANES, MLSTM_HEAD_DIM), F32),
                        pltpu.VMEM((SUBLANES, LANES), F32)],
        compiler_params=_params("arbitrary", "arbitrary"),
        name="mlstm",
    )(mqk.reshape(batch, seq, w2), mv.reshape(batch, seq, MLSTM_WIDTH), mo.reshape(batch, seq, MLSTM_WIDTH),
      gates.reshape(batch, seq, LANES), conv_w.astype(F32), conv_b.reshape(1, w2).astype(F32), gbias,
      norm_g.reshape(1, MLSTM_WIDTH).astype(F32))


def _outproj_kernel(att_ref, hm_ref, h_ref, wa_ref, wm_ref, g_ref, b_ref, o_ref):
    y = _dot(att_ref[...], wa_ref[...]) + _dot(hm_ref[...], wm_ref[...])
    o_ref[...] = _layer_norm(DEEPNORM_ALPHA * h_ref[...] + y, g_ref[...], b_ref[...])


def _outproj(att, hm, h, w_out, g, b):
    n, d = h.shape
    wa = w_out[:ATT_WIDTH].astype(BF16)
    wm = w_out[ATT_WIDTH:].astype(BF16)
    rows = lambda w: pl.BlockSpec((ROW_BLOCK, w), lambda i: (i, 0))
    return pl.pallas_call(
        _outproj_kernel,
        grid=(n // ROW_BLOCK,),
        in_specs=[rows(ATT_WIDTH), rows(MLSTM_WIDTH), rows(d), _full(wa.shape), _full(wm.shape),
                  _full((1, d)), _full((1, d))],
        out_specs=rows(d),
        out_shape=jax.ShapeDtypeStruct((n, d), F32),
        compiler_params=_params("arbitrary"),
        name="outproj",
    )(att, hm, h, wa, wm, g.reshape(1, d), b.reshape(1, d))


def _kvproj_kernel(mem_ref, w_ref, k_ref, v_ref):
    kv = _dot(mem_ref[...].astype(BF16), w_ref[...])
    d = k_ref.shape[-1]
    k_ref[...] = kv[:, :d].astype(BF16)
    v_ref[...] = kv[:, d:].astype(BF16)


def _kvproj(mem2, w_kv):
    n, d = mem2.shape
    w = w_kv.astype(BF16)
    blk = min(ROW_BLOCK, n)
    rows = pl.BlockSpec((blk, d), lambda i: (i, 0))
    return pl.pallas_call(
        _kvproj_kernel,
        grid=(n // blk,),
        in_specs=[rows, _full(w.shape)],
        out_specs=[rows, rows],
        out_shape=[jax.ShapeDtypeStruct((n, d), BF16)] * 2,
        compiler_params=_params("arbitrary"),
        name="kvproj",
    )(mem2, w)


def _xattn_kernel(h_ref, k_ref, v_ref, wq_ref, wo_ref, g_ref, b_ref, o_ref):
    h = h_ref[0]
    d = h.shape[-1]
    dh = d // XATT_HEADS
    q = (_dot(h.astype(BF16), wq_ref[...]) * (dh ** -0.5)).astype(BF16)
    outs = []
    for hd in range(XATT_HEADS):
        sl = slice(hd * dh, (hd + 1) * dh)
        s = _dot_nt(q[:, sl], k_ref[0, :, sl])
        m = jnp.max(s, -1, keepdims=True)
        p = jnp.exp(s - m)
        l = jnp.sum(p, -1, keepdims=True)
        outs.append((_dot(p.astype(BF16), v_ref[0, :, sl]) / l).astype(BF16))
    y = _dot(jnp.concatenate(outs, axis=-1), wo_ref[...])
    o_ref[0] = _layer_norm(DEEPNORM_ALPHA * h + y, g_ref[...], b_ref[...])


def _xattn(h3, k3, v3, w_q, w_o, g, b):
    batch, seq, d = h3.shape
    m = k3.shape[1]
    wq = w_q.astype(BF16)
    wo = w_o.astype(BF16)
    blk = pl.BlockSpec((1, ROW_BLOCK, d), lambda bb, i: (bb, i, 0))
    mem = pl.BlockSpec((1, m, d), lambda bb, i: (bb, 0, 0))
    return pl.pallas_call(
        _xattn_kernel,
        grid=(batch, seq // ROW_BLOCK),
        in_specs=[blk, mem, mem, _full(wq.shape), _full(wo.shape), _full((1, d)), _full((1, d))],
        out_specs=blk,
        out_shape=jax.ShapeDtypeStruct((batch, seq, d), F32),
        compiler_params=_params("arbitrary", "arbitrary"),
        name="xattn",
    )(h3, k3, v3, wq, wo, g.reshape(1, d), b.reshape(1, d))


def _topk_rows(s, k):
    groups = s.shape[0] // SUBLANES
    t = s.shape[1]
    sg = [s[g * SUBLANES:(g + 1) * SUBLANES] for g in range(groups)]
    sub = lax.broadcasted_iota(I32, (SUBLANES, t), 0)
    vals, ids = [], []
    for _ in range(k):
        m8, a8 = sg[0], jnp.zeros((SUBLANES, t), I32)
        for g in range(1, groups):
            upd = sg[g] > m8
            m8 = jnp.where(upd, sg[g], m8)
            a8 = jnp.where(upd, g, a8)
        m = jnp.max(m8, axis=0, keepdims=True)
        rsel = jnp.min(jnp.where(m8 == m, sub, SUBLANES), axis=0, keepdims=True)
        hit = sub == rsel
        gsel = jnp.max(jnp.where(hit, a8, 0), axis=0, keepdims=True)
        vals.append(m)
        ids.append(rsel * groups + gsel)
        gone = jnp.where(hit, gsel, -1)
        sg = [jnp.where(gone == g, -jnp.inf, sg[g]) for g in range(groups)]
    return vals, ids


def _stack_rows(rows_list):
    k = len(rows_list)
    t = rows_list[0].shape[-1]
    iota = lax.broadcasted_iota(I32, (k, t), 0)
    out = jnp.broadcast_to(rows_list[0], (k, t))
    for r in range(1, k):
        out = jnp.where(iota == r, jnp.broadcast_to(rows_list[r], (k, t)), out)
    return out


def _peer_topk_kernel(h_ref, wq_ref, khi_ref, klo_ref, row_ref, hbit_ref, g_ref, e_scr, g_scr):
    t = h_ref.shape[0]
    kk = PEER_TOPK
    q = _dot(h_ref[...].astype(BF16), wq_ref[...])
    half8 = kk // 2
    sub = lax.broadcasted_iota(I32, (half8, t), 0)
    for hh in range(PEER_HEADS):
        tops = []
        for p in range(2):
            c0 = (hh * 2 + p) * N_KEYS
            qhi, qlo = _split2(q[:, c0:c0 + N_KEYS])
            s = _dot_nt(khi_ref[p], qhi) + _dot_nt(khi_ref[p], qlo) + _dot_nt(klo_ref[p], qhi)
            vals, idxs = _topk_rows(s, kk)
            tops.append((_stack_rows(vals), _stack_rows(idxs)))
        (s0, i0), (s1, i1) = tops
        e0 = i0 * N_KEYS
        cand, cexp, cflat = [], [], []
        for b in range(half8):
            cand.append(s0[:half8] + s1[b:b + 1])
            cexp.append(e0[:half8] + i1[b:b + 1])
            cflat.append(sub * kk + b)
        cand.append(s0[half8:] + s1[0:1])
        cexp.append(e0[half8:] + i1[0:1])
        cflat.append((sub + half8) * kk)
        cand.append(s0[0:1] + s1[half8:])
        cexp.append(e0[0:1] + i1[half8:])
        cflat.append(sub + half8)
        cand = jnp.concatenate(cand, axis=0)
        cexp = jnp.concatenate(cexp, axis=0)
        cflat = jnp.concatenate(cflat, axis=0)
        best_s, best_e = [], []
        for _ in range(kk):
            m = jnp.max(cand, axis=0, keepdims=True)
            jsel = jnp.min(jnp.where(cand == m, cflat, kk * kk), axis=0, keepdims=True)
            hit = cflat == jsel
            best_s.append(m)
            best_e.append(jnp.max(jnp.where(hit, cexp, 0), axis=0, keepdims=True))
            cand = jnp.where(hit, -jnp.inf, cand)
        bs = _stack_rows(best_s)
        ex = jnp.exp(bs - bs[0:1])
        g_scr[hh * kk:(hh + 1) * kk, :] = ex / jnp.sum(ex, axis=0, keepdims=True)
        e_scr[hh * kk:(hh + 1) * kk, :] = _stack_rows(best_e)
    e = e_scr[...]
    hb = e >> (PEER_HALF_EXPERTS.bit_length() - 1)
    row_ref[...] = ((e & (PEER_HALF_EXPERTS - 1)) * SUBLANES).T
    hbit_ref[...] = hb.astype(F32).T
    g_ref[...] = g_scr[...].T


def _peer_topk(h2, w_query, sub_keys):
    n, d = h2.shape
    wq = w_query.astype(BF16)
    keys = sub_keys.reshape(2, SUBLANES, N_KEYS // SUBLANES, -1).swapaxes(1, 2).reshape(sub_keys.shape)
    khi = keys.astype(BF16)
    klo = (keys - khi.astype(F32)).astype(BF16)
    rows = pl.BlockSpec((TOPK_BLOCK, PEER_PICKS), lambda i: (i, 0))
    return pl.pallas_call(
        _peer_topk_kernel,
        grid=(n // TOPK_BLOCK,),
        in_specs=[pl.BlockSpec((TOPK_BLOCK, d), lambda i: (i, 0)), _full(wq.shape), _full(khi.shape), _full(klo.shape)],
        out_specs=[rows, rows, rows],
        out_shape=[jax.ShapeDtypeStruct((n, PEER_PICKS), I32),
                   jax.ShapeDtypeStruct((n, PEER_PICKS), F32), jax.ShapeDtypeStruct((n, PEER_PICKS), F32)],
        scratch_shapes=[pltpu.VMEM((PEER_PICKS, TOPK_BLOCK), I32), pltpu.VMEM((PEER_PICKS, TOPK_BLOCK), F32)],
        compiler_params=_params("arbitrary"),
        name="peer_topk",
    )(h2, wq, khi, klo)


def _pack_table(tab):
    e, d = tab.shape
    bits = lax.bitcast_convert_type(tab.astype(BF16), jnp.uint16).astype(U32)
    packed = bits[:e // 2] | (bits[e // 2:] << 16)
    return packed.reshape(e // 2 * (d // LANES), LANES)


def _table_tile(tab_ref, row8):
    return pltpu.bitcast(tab_ref[pl.ds(pl.multiple_of(row8, SUBLANES), SUBLANES), :], BF16)


def _peer_u_consts():
    k = jnp.arange(2 * LANES)
    col = jnp.arange(2 * PEER_PICKS)
    row = jnp.arange(PEER_PICKS)
    fold = ((k[None, :] // 16 == jnp.arange(2 * U_TILE_PICKS)[:, None] // 2)
            & (k[None, :] % 2 == jnp.arange(2 * U_TILE_PICKS)[:, None] % 2)).astype(BF16)
    lanesum = (((col[None, :] // 2) % 2) == (k[:, None] // LANES)).astype(BF16)
    place = ((col[None, :] // 64 == row[:, None] // 32) & ((col[None, :] % 64) // 4 == (row[:, None] % 32) // 2)
             & (col[None, :] % 2 == row[:, None] % 2)).astype(F32)
    dup = ((col[None, :] // 2) == jnp.arange(PEER_PICKS)[:, None]).astype(BF16)
    return fold, lanesum, place, dup


def _index_copy(row_hbm, idx_smem, sem, batch, slot):
    return pltpu.make_async_copy(row_hbm.at[pl.ds(batch * PEER_UNROLL, PEER_UNROLL)], idx_smem.at[slot], sem.at[slot])


def _for_each_token_batch(row_hbm, idx_smem, sem, block_tokens, body):
    step = pl.program_id(0)
    per_step = block_tokens // PEER_UNROLL
    assert per_step % 2 == 0
    total = pl.num_programs(0) * per_step

    @pl.when(step == 0)
    def _():
        _index_copy(row_hbm, idx_smem, sem, 0, 0).start()

    def pair(k, carry):
        for slot in range(2):
            local = 2 * k + slot
            batch = step * per_step + local
            _index_copy(row_hbm, idx_smem, sem, batch, slot).wait()

            @pl.when(batch + 1 < total)
            def _():
                _index_copy(row_hbm, idx_smem, sem, batch + 1, 1 - slot).start()

            body(local * PEER_UNROLL, idx_smem.at[slot])
        return carry

    lax.fori_loop(0, per_step // 2, pair, 0)


def _peer_u_kernel(row_hbm, x_ref, g_ref, hbit_ref, tab_ref, fold_ref, lanesum_ref, place_ref, dup_ref,
                   coefh_ref, res_scr, idx_smem, sem):
    tb = x_ref.shape[0]
    ntiles = PEER_PICKS // (2 * U_TILE_PICKS)

    def tokens(t0, idx):
        for u in range(PEER_UNROLL):
            t = t0 + u
            x = x_ref[t]
            xbits = pltpu.bitcast(x.astype(BF16).astype(F32), U32)
            xw = pltpu.bitcast(xbits | (xbits >> 16), BF16)
            for j in range(ntiles):
                cols = []
                for ab in range(2):
                    prods = []
                    for mm in range(U_TILE_PICKS):
                        c = j * 2 * U_TILE_PICKS + 2 * mm + ab
                        prods.append(_table_tile(tab_ref, idx[u, c]) * xw)
                    cols.append(jnp.concatenate(prods, axis=0))
                res_scr[t, j] = _dot(fold_ref[...], jnp.concatenate(cols, axis=1))

    _for_each_token_batch(row_hbm, idx_smem, sem, tb, tokens)

    r = res_scr[...].reshape(tb * PEER_PICKS, 2 * LANES).astype(BF16)
    s = _dot(r, lanesum_ref[...])
    a2 = jnp.sum(s.reshape(tb, PEER_PICKS, 2 * PEER_PICKS) * place_ref[...], axis=1)
    ghi, glo = _split2(g_ref[...])
    g2 = _dot(ghi, dup_ref[...]) + _dot(glo, dup_ref[...])
    hbit2 = _dot(hbit_ref[...].astype(BF16), dup_ref[...])
    parity = (lax.broadcasted_iota(I32, a2.shape, 1) % 2).astype(F32)
    coef = g2 * (0.5 * a2 * (1.0 + lax.erf(a2 * math.sqrt(0.5))))
    coefh_ref[...] = jnp.where(hbit2 == parity, coef, 0.0).astype(BF16)


def _peer_v_consts():
    col = jnp.arange(2 * PEER_PICKS * SUBLANES)
    ab, p, r = col // (PEER_PICKS * SUBLANES), (col // 16) % (PEER_PICKS // 2), col % 16
    src = jnp.arange(2 * PEER_PICKS)
    expand = ((src[:, None] // 2 == (2 * p + ab)[None, :]) & (src[:, None] % 2 == (r % 2)[None, :])).astype(BF16)
    kk = jnp.arange(PEER_PICKS * SUBLANES)
    diag = ((kk[None, :] % 16) // 2 == (jnp.arange(16) % SUBLANES)[:, None]).astype(F32)
    return expand, diag


def _peer_v_kernel(row_hbm, coefh_ref, x_ref, tab_ref, expand_ref, diag_ref, g_ref, b_ref, o_ref, lrow_scr, y_scr,
                   idx_smem, sem):
    tb = x_ref.shape[0]
    kdim = PEER_PICKS * SUBLANES
    lrow_scr[...] = _dot(coefh_ref[...], expand_ref[...])
    diag = diag_ref[...]

    def tokens(t0, idx):
        for u in range(PEER_UNROLL):
            t = t0 + u
            even = jnp.broadcast_to(lrow_scr[pl.ds(t, 1), 0:kdim], (SUBLANES, kdim))
            odd = jnp.broadcast_to(lrow_scr[pl.ds(t, 1), kdim:2 * kdim], (SUBLANES, kdim))
            lhs = (jnp.concatenate([even, odd], axis=0) * diag).astype(BF16)
            tiles = []
            for p in range(PEER_PICKS // 2):
                wa = _table_tile(tab_ref, idx[u, 2 * p])
                wb = _table_tile(tab_ref, idx[u, 2 * p + 1])
                tiles.append(jnp.concatenate([wa, wb], axis=1))
            res = _dot(lhs, jnp.concatenate(tiles, axis=0))
            y_scr[t] = res[0:SUBLANES, 0:LANES] + res[SUBLANES:, LANES:]

    _for_each_token_batch(row_hbm, idx_smem, sem, tb, tokens)
    z = DEEPNORM_ALPHA * x_ref[...] + y_scr[...]
    cnt = z.shape[1] * z.shape[2]
    mu = jnp.sum(jnp.sum(z, axis=2, keepdims=True), axis=1, keepdims=True) / cnt
    zc = z - mu
    var = jnp.sum(jnp.sum(zc * zc, axis=2, keepdims=True), axis=1, keepdims=True) / cnt
    o_ref[...] = zc * lax.rsqrt(var + LN_EPS) * g_ref[...] + b_ref[...]


def _peer_ffn(h2, row, hbit, gate, expert_u, expert_v, ln_g, ln_b):
    n, d = h2.shape
    rows8 = d // LANES
    assert rows8 == SUBLANES and PEER_PICKS == LANES and expert_u.shape[0] == 2 * PEER_HALF_EXPERTS
    x3 = h2.reshape(n, rows8, LANES)
    u_pk = _pack_table(expert_u)
    v_pk = _pack_table(expert_v)
    hbm = pl.BlockSpec(memory_space=pl.ANY)
    staging = [pltpu.SMEM((2, PEER_UNROLL, PEER_PICKS), I32), pltpu.SemaphoreType.DMA((2,))]
    picks =pl.BlockSpec((PEER_BLOCK, PEER_PICKS), lambda i: (i, 0))
    picks2 = pl.BlockSpec((PEER_BLOCK, 2 * PEER_PICKS), lambda i: (i, 0))
    tok = pl.BlockSpec((PEER_BLOCK, rows8, LANES), lambda i: (i, 0, 0))
    table = pl.BlockSpec(u_pk.shape, lambda i: (0, 0), pipeline_mode=pl.Buffered(1))
    u_consts = _peer_u_consts()
    coefh = pl.pallas_call(
        _peer_u_kernel,
        grid=(n // PEER_BLOCK,),
        in_specs=[hbm, tok, picks, picks, table] + [_full(c.shape) for c in u_consts],
        out_specs=picks2,
        out_shape=jax.ShapeDtypeStruct((n, 2 * PEER_PICKS), BF16),
        scratch_shapes=[pltpu.VMEM((PEER_BLOCK, PEER_PICKS // (2 * U_TILE_PICKS), 2 * U_TILE_PICKS, 2 * LANES), F32)]
        + staging,
        compiler_params=_params("arbitrary"),
        name="peer_u",
    )(row, x3, gate, hbit, u_pk, *u_consts)
    v_consts = _peer_v_consts()
    out = pl.pallas_call(
        _peer_v_kernel,
        grid=(n // PEER_BLOCK,),
        in_specs=[hbm, picks2, tok, table] + [_full(c.shape) for c in v_consts]
        + [_full((rows8, LANES)), _full((rows8, LANES))],
        out_specs=tok,
        out_shape=jax.ShapeDtypeStruct((n, rows8, LANES), F32),
        scratch_shapes=[pltpu.VMEM((PEER_BLOCK, 2 * PEER_PICKS * SUBLANES), F32),
                        pltpu.VMEM((PEER_BLOCK, rows8, LANES), F32)] + staging,
        compiler_params=_params("arbitrary"),
        name="peer_v",
    )(row, coefh, x3, v_pk, *v_consts, ln_g.reshape(rows8, LANES), ln_b.reshape(rows8, LANES))
    return out.reshape(n, d)


def kernel(x, mem, ln_in_g, ln_in_b, w_in, conv_w, conv_b, mlstm_i_bias, mlstm_f_bias, mlstm_norm_g, rel_bias, w_out, ln1_g, ln1_b, xattn_w_q, xattn_w_kv, xattn_w_o, ln2_g, ln2_b, peer_w_query, peer_sub_keys, peer_u, peer_v, ln3_g, ln3_b):
    batch, seq, d = x.shape
    n = batch * seq
    assert w_in.shape[0] == DEPTH
    h, qkv, mqk, mv, mo, gates = _ln_inproj(x.reshape(n, d), ln_in_g, ln_in_b, w_in[0])
    for l in range(DEPTH):
        if l > 0:
            raise NotImplementedError("input projection of deeper layers")
        att = _attention(qkv, rel_bias[l], batch, seq)
        hm = _mlstm(mqk, mv, mo, gates, conv_w[l], conv_b[l], mlstm_i_bias[l], mlstm_f_bias[l], mlstm_norm_g[l],
                    batch, seq)
        h = _outproj(att.reshape(n, ATT_WIDTH), hm.reshape(n, MLSTM_WIDTH), h, w_out[l], ln1_g[l], ln1_b[l])
        k2, v2 = _kvproj(mem.reshape(-1, d), xattn_w_kv[l])
        m = mem.shape[1]
        h = _xattn(h.reshape(batch, seq, d), k2.reshape(batch, m, d), v2.reshape(batch, m, d),
                   xattn_w_q[l], xattn_w_o[l], ln2_g[l], ln2_b[l]).reshape(n, d)
        row, hbit, gate = _peer_topk(h, peer_w_query[l], peer_sub_keys[l])
        h = _peer_ffn(h, row, hbit, gate, peer_u[l], peer_v[l], ln3_g[l], ln3_b[l])
    return h.reshape(batch, seq, d)
```

```python
import functools
import math

import jax
import jax.numpy as jnp
from jax import lax
from jax.experimental import pallas as pl
from jax.experimental.pallas import tpu as pltpu

F32, BF16, I32, U32 = jnp.float32, jnp.bfloat16, jnp.int32, jnp.uint32

DEPTH = 1
LN_EPS = 1e-5
DEEPNORM_ALPHA = (2.0 * DEPTH) ** 0.25
CHUNK = 64
ATT_HEADS, ATT_HEAD_DIM, ATT_LEFT_CHUNKS, MAX_REL = 8, 64, 8, 128
ATT_WIDTH = ATT_HEADS * ATT_HEAD_DIM
MLSTM_HEADS, MLSTM_HEAD_DIM, CONV_WIDTH = 4, 128, 4
MLSTM_WIDTH = MLSTM_HEADS * MLSTM_HEAD_DIM
XATT_HEADS = 4
PEER_HEADS, N_KEYS, PEER_TOPK = 8, 128, 16
PEER_PICKS = PEER_HEADS * PEER_TOPK
PEER_HALF_EXPERTS = N_KEYS * N_KEYS // 2
U_TILE_PICKS = 16
PEER_UNROLL = 16

LANES = 128
SUBLANES = 8
VMEM_LIMIT_BYTES = 56 * 1024 * 1024

NEG_BIG = -1e30

ROW_BLOCK = 512
ATT_BLOCK = 256
ATT_KEY_BLOCKS = 3
MLSTM_BLOCK = 256
TOPK_BLOCK = 256
PEER_BLOCK = 64


def _params(*semantics):
    return pltpu.CompilerParams(dimension_semantics=semantics, vmem_limit_bytes=VMEM_LIMIT_BYTES)


def _layer_norm(x, g, b):
    mu = jnp.mean(x, -1, keepdims=True)
    xc = x - mu
    var = jnp.mean(xc * xc, -1, keepdims=True)
    return xc * lax.rsqrt(var + LN_EPS) * g + b


def _split2(x):
    hi = x.astype(BF16)
    lo = (x - hi.astype(F32)).astype(BF16)
    return hi, lo


def _split3(x):
    hi = x.astype(BF16)
    r = x - hi.astype(F32)
    mid = r.astype(BF16)
    lo = (r - mid.astype(F32)).astype(BF16)
    return hi, mid, lo


def _dot(a, b):
    return jnp.dot(a, b, preferred_element_type=F32)


def _dot_nt(a, b):
    return lax.dot_general(a, b, (((1,), (1,)), ((), ())), preferred_element_type=F32)


def _dot_tn(a, b):
    return lax.dot_general(a, b, (((0,), (0,)), ((), ())), preferred_element_type=F32)


def _full(shape):
    return pl.BlockSpec(shape, lambda *_: (0,) * len(shape))


def _ln_inproj_kernel(x_ref, g_ref, b_ref, wqkv_ref, wmqk_ref, wmv_ref, wmo_ref, wghi_ref, wglo_ref,
                      h_ref, qkv_ref, mqk_ref, mv_ref, mo_ref, gate_ref):
    h = _layer_norm(x_ref[...], g_ref[...], b_ref[...])
    h_ref[...] = h
    hb, hlo = _split2(h)
    qkv_ref[...] = _dot(hb, wqkv_ref[...]).astype(BF16)
    mqk_ref[...] = _dot(hb, wmqk_ref[...]).astype(BF16)
    mv_ref[...] = _dot(hb, wmv_ref[...]).astype(BF16)
    mo_ref[...] = _dot(hb, wmo_ref[...]).astype(BF16)
    gate_ref[...] = _dot(hb, wghi_ref[...]) + _dot(hlo, wghi_ref[...]) + _dot(hb, wglo_ref[...])


def _ln_inproj(x2, g, b, w_in):
    n, d = x2.shape
    a3 = 3 * ATT_WIDTH
    wqkv = w_in[:, :a3].astype(BF16)
    wmqk = w_in[:, a3:a3 + 2 * MLSTM_WIDTH].astype(BF16)
    wmv = w_in[:, a3 + 2 * MLSTM_WIDTH:a3 + 3 * MLSTM_WIDTH].astype(BF16)
    wmo = w_in[:, a3 + 3 * MLSTM_WIDTH:a3 + 4 * MLSTM_WIDTH].astype(BF16)
    wg = jnp.pad(w_in[:, a3 + 4 * MLSTM_WIDTH:], ((0, 0), (0, LANES - 2 * MLSTM_HEADS)))
    wghi = wg.astype(BF16)
    wglo = (wg - wghi.astype(F32)).astype(BF16)
    rows = lambda w: pl.BlockSpec((ROW_BLOCK, w), lambda i: (i, 0))
    return pl.pallas_call(
        _ln_inproj_kernel,
        grid=(n // ROW_BLOCK,),
        in_specs=[rows(d), _full((1, d)), _full((1, d)), _full(wqkv.shape), _full(wmqk.shape),
                  _full(wmv.shape), _full(wmo.shape), _full(wghi.shape), _full(wglo.shape)],
        out_specs=[rows(d), rows(a3), rows(2 * MLSTM_WIDTH), rows(MLSTM_WIDTH), rows(MLSTM_WIDTH), rows(LANES)],
        out_shape=[jax.ShapeDtypeStruct((n, d), F32), jax.ShapeDtypeStruct((n, a3), BF16),
                   jax.ShapeDtypeStruct((n, 2 * MLSTM_WIDTH), BF16), jax.ShapeDtypeStruct((n, MLSTM_WIDTH), BF16),
                   jax.ShapeDtypeStruct((n, MLSTM_WIDTH), BF16), jax.ShapeDtypeStruct((n, LANES), F32)],
        compiler_params=_params("arbitrary"),
        name="ln_inproj",
    )(x2, g.reshape(1, d), b.reshape(1, d), wqkv, wmqk, wmv, wmo, wghi, wglo)


def _attn_kernel(q_ref, k0_ref, k1_ref, k2_ref, v0_ref, v1_ref, v2_ref, bias_ref, o_ref):
    i = pl.program_id(1)
    nkeys = ATT_KEY_BLOCKS * ATT_BLOCK
    q = q_ref[0] * (ATT_HEAD_DIM ** -0.5)
    kcat = jnp.concatenate([k0_ref[0], k1_ref[0], k2_ref[0]], axis=0)
    vcat = jnp.concatenate([v0_ref[0], v1_ref[0], v2_ref[0]], axis=0)
    col = lax.broadcasted_iota(I32, (1, nkeys), 1)
    in_seq = col >= (ATT_KEY_BLOCKS - 1 - i) * ATT_BLOCK
    lane = lax.broadcasted_iota(I32, (1, LANES), 1)
    low = lane < ATT_HEAD_DIM
    outs = []
    for pair in range(ATT_HEADS // 2):
        sl = slice(pair * LANES, (pair + 1) * LANES)
        qp, kp, vp = q[:, sl], kcat[:, sl], vcat[:, sl]
        halves = []
        for half in range(2):
            keep = low if half == 0 else jnp.logical_not(low)
            qh = jnp.where(keep, qp, jnp.zeros_like(qp))
            s = _dot_nt(qh, kp) + bias_ref[2 * pair + half]
            s = jnp.where(in_seq, s, NEG_BIG)
            m = jnp.max(s, -1, keepdims=True)
            p = jnp.exp(s - m)
            l = jnp.sum(p, -1, keepdims=True)
            halves.append(_dot(p.astype(BF16), vp) / l)
        outs.append(jnp.where(low, halves[0], halves[1]))
    o_ref[0] = jnp.concatenate(outs, axis=-1).astype(BF16)


def _attn_bias(rel_bias):
    nq, nk = ATT_BLOCK, ATT_KEY_BLOCKS * ATT_BLOCK
    r = jnp.arange(nq)[:, None]
    j = jnp.arange(nk)[None, :]
    allowed = (j // CHUNK >= r // CHUNK) & (j // CHUNK <= r // CHUNK + ATT_LEFT_CHUNKS)
    period = nq + nk - 1
    i = jnp.arange(period)
    d = jnp.where(i < nk, i, i - period)
    rel = d - (ATT_KEY_BLOCKS - 1) * ATT_BLOCK
    line = rel_bias[:, jnp.clip(rel, -MAX_REL, MAX_REL) + MAX_REL].astype(F32)
    rep = jnp.tile(line, (1, nq + 1))[:, :nq * (period - 1)]
    tab = rep.reshape(-1, nq, period - 1)[:, :, :nk]
    return jnp.where(allowed[None], tab, NEG_BIG)


def _attention(qkv, rel_bias, batch, seq):
    assert ATT_LEFT_CHUNKS * CHUNK == (ATT_KEY_BLOCKS - 1) * ATT_BLOCK
    qkv3 = qkv.reshape(batch, seq, 3 * ATT_WIDTH)
    bias = _attn_bias(rel_bias)
    blk = (1, ATT_BLOCK, ATT_WIDTH)

    def kv_spec(col, j):
        return pl.BlockSpec(blk, lambda b, i: (b, jnp.maximum(i - (ATT_KEY_BLOCKS - 1) + j, 0), col))

    return pl.pallas_call(
        _attn_kernel,
        grid=(batch, seq // ATT_BLOCK),
        in_specs=[pl.BlockSpec(blk, lambda b, i: (b, i, 0))]
        + [kv_spec(1, j) for j in range(ATT_KEY_BLOCKS)] + [kv_spec(2, j) for j in range(ATT_KEY_BLOCKS)]
        + [_full(bias.shape)],
        out_specs=pl.BlockSpec(blk, lambda b, i: (b, i, 0)),
        out_shape=jax.ShapeDtypeStruct((batch, seq, ATT_WIDTH), BF16),
        compiler_params=_params("arbitrary", "arbitrary"),
        name="attn",
    )(qkv3, qkv3, qkv3, qkv3, qkv3, qkv3, qkv3, bias)


def _mlstm_kernel(mqk_ref, mv_ref, mo_ref, gate_ref, convw_ref, convb_ref, gbias_ref, ng_ref, out_ref,
                  xpad, q_scr, k_scr, gate_scr, logf_scr, c_st, n_st, m_st):
    j = pl.program_id(1)
    rows = MLSTM_BLOCK
    d = MLSTM_HEAD_DIM

    @pl.when(j == 0)
    def _():
        xpad[0:SUBLANES, :] = jnp.zeros((SUBLANES, 2 * MLSTM_WIDTH), F32)
        c_st[...] = jnp.zeros_like(c_st)
        n_st[...] = jnp.zeros_like(n_st)
        m_st[...] = jnp.zeros_like(m_st)

    xpad[SUBLANES:SUBLANES + rows, :] = mqk_ref[0].astype(F32)
    acc = jnp.broadcast_to(convb_ref[...], (rows, 2 * MLSTM_WIDTH))
    for t in range(CONV_WIDTH):
        acc = acc + convw_ref[t:t + 1, :] * xpad[pl.ds(SUBLANES - (CONV_WIDTH - 1) + t, rows), :]
    xpad[0:SUBLANES, :] = xpad[rows:rows + SUBLANES, :]
    qk = acc * jax.nn.sigmoid(acc)
    q_scr[...] = qk[:, :MLSTM_WIDTH].astype(BF16)
    k_scr[...] = qk[:, MLSTM_WIDTH:] * (d ** -0.5)

    gates = gate_ref[0] + gbias_ref[...]
    gate_scr[...] = gates
    logf_scr[...] = jax.nn.log_sigmoid(gates)

    ri = lax.broadcasted_iota(I32, (CHUNK, CHUNK), 0)
    ci = lax.broadcasted_iota(I32, (CHUNK, CHUNK), 1)
    causal = ci <= ri
    tri = causal.astype(BF16)

    def chunk_body(c, carry):
        r0 = pl.multiple_of(c * CHUNK, CHUNK)
        g = gate_scr[pl.ds(r0, CHUNK), :]
        lf = logf_scr[pl.ds(r0, CHUNK), :]
        l1, l2, l3 = _split3(lf)
        bcol = _dot(tri, l1) + _dot(tri, l2) + _dot(tri, l3)
        g_t = g.T
        b_t = bcol.T
        m_all = m_st[...]
        n_all = n_st[...]
        c_all = [c_st[h] for h in range(MLSTM_HEADS)]
        c_new, n_new, m_new_rows = [], [], []
        for h in range(MLSTM_HEADS):
            hs = slice(h * d, (h + 1) * d)
            ic = g[:, h:h + 1]
            bc = bcol[:, MLSTM_HEADS + h:MLSTM_HEADS + h + 1]
            ir = g_t[h:h + 1, :]
            br = b_t[MLSTM_HEADS + h:MLSTM_HEADS + h + 1, :]
            m_prev = m_all[h:h + 1, 0:1]
            n_prev = n_all[h:h + 1, :]
            c_prev = c_all[h]
            qh = q_scr[pl.ds(r0, CHUNK), hs]
            kh = k_scr[pl.ds(r0, CHUNK), hs]
            vh = mv_ref[0, pl.ds(r0, CHUNK), hs]

            log_d = jnp.where(causal, bc - br + ir, NEG_BIG)
            inter = bc + m_prev
            m_t = jnp.maximum(inter, jnp.max(log_d, -1, keepdims=True))
            d_mat = jnp.exp(log_d - m_t)
            w_inter = jnp.exp(inter - m_t)
            qk_d = _dot_nt(qh, kh.astype(BF16)) * d_mat
            num = w_inter * _dot(qh, c_prev.astype(BF16)) + _dot(qk_d.astype(BF16), vh)
            den = (w_inter * jnp.sum(qh.astype(F32) * n_prev, -1, keepdims=True)
                   + jnp.sum(qk_d, -1, keepdims=True))
            hh = num / jnp.maximum(jnp.abs(den), jnp.exp(-m_t))

            b_last = bc[CHUNK - 1:CHUNK, :]
            log_in = b_last - bc + ic
            m_new = jnp.maximum(b_last + m_prev, jnp.max(log_in, 0, keepdims=True))
            w_prev = jnp.exp(b_last + m_prev - m_new)
            kw = kh * jnp.exp(log_in - m_new)
            c_new.append(w_prev * c_prev + _dot_tn(kw.astype(BF16), vh))
            n_new.append(w_prev * n_prev + jnp.sum(kw, 0, keepdims=True))
            m_new_rows.append(jnp.broadcast_to(m_new, (1, LANES)))

            mu = jnp.mean(hh, -1, keepdims=True)
            hc = hh - mu
            var = jnp.mean(hc * hc, -1, keepdims=True)
            hn = hc * lax.rsqrt(var + LN_EPS) * ng_ref[:, hs]
            og = jax.nn.sigmoid(mo_ref[0, pl.ds(r0, CHUNK), hs].astype(F32))
            out_ref[0, pl.ds(r0, CHUNK), hs] = (og * hn).astype(BF16)
        for h in range(MLSTM_HEADS):
            c_st[h] = c_new[h]
            n_st[h:h + 1, :] = n_new[h]
            m_st[h:h + 1, :] = m_new_rows[h]
        return carry

    lax.fori_loop(0, rows // CHUNK, chunk_body, 0)


def _mlstm(mqk, mv, mo, gates, conv_w, conv_b, i_bias, f_bias, norm_g, batch, seq):
    w2 = 2 * MLSTM_WIDTH
    gbias = jnp.pad(jnp.concatenate([i_bias, f_bias]).astype(F32), (0, LANES - 2 * MLSTM_HEADS)).reshape(1, LANES)
    blk = lambda w: pl.BlockSpec((1, MLSTM_BLOCK, w), lambda b, i: (b, i, 0))
    return pl.pallas_call(
        _mlstm_kernel,
        grid=(batch, seq // MLSTM_BLOCK),
        in_specs=[blk(w2), blk(MLSTM_WIDTH), blk(MLSTM_WIDTH), blk(LANES),
                  _full((CONV_WIDTH, w2)), _full((1, w2)), _full((1, LANES)), _full((1, MLSTM_WIDTH))],
        out_specs=blk(MLSTM_WIDTH),
        out_shape=jax.ShapeDtypeStruct((batch, seq, MLSTM_WIDTH), BF16),
        scratch_shapes=[pltpu.VMEM((MLSTM_BLOCK + SUBLANES, w2), F32),
                        pltpu.VMEM((MLSTM_BLOCK, MLSTM_WIDTH), BF16),
                        pltpu.VMEM((MLSTM_BLOCK, MLSTM_WIDTH), F32),
                        pltpu.VMEM((MLSTM_BLOCK, LANES), F32),
                        pltpu.VMEM((MLSTM_BLOCK, LANES), F32),
                        pltpu.VMEM((MLSTM_HEADS, MLSTM_HEAD_DIM, MLSTM_HEAD_DIM), F32),
                        pltpu.VMEM((SUBLANES, MLSTM_HEAD_DIM), F32),
                        pltpu.VMEM((SUBLANES, LANES), F32)],
        compiler_params=_params("arbitrary", "arbitrary"),
        name="mlstm",
    )(mqk.reshape(batch, seq, w2), mv.reshape(batch, seq, MLSTM_WIDTH), mo.reshape(batch, seq, MLSTM_WIDTH),
      gates.reshape(batch, seq, LANES), conv_w.astype(F32), conv_b.reshape(1, w2).astype(F32), gbias,
      norm_g.reshape(1, MLSTM_WIDTH).astype(F32))


def _outproj_kernel(att_ref, hm_ref, h_ref, wa_ref, wm_ref, g_ref, b_ref, o_ref):
    y = _dot(att_ref[...], wa_ref[...]) + _dot(hm_ref[...], wm_ref[...])
    o_ref[...] = _layer_norm(DEEPNORM_ALPHA * h_ref[...] + y, g_ref[...], b_ref[...])


def _outproj(att, hm, h, w_out, g, b):
    n, d = h.shape
    wa = w_out[:ATT_WIDTH].astype(BF16)
    wm = w_out[ATT_WIDTH:].astype(BF16)
    rows = lambda w: pl.BlockSpec((ROW_BLOCK, w), lambda i: (i, 0))
    return pl.pallas_call(
        _outproj_kernel,
        grid=(n // ROW_BLOCK,),
        in_specs=[rows(ATT_WIDTH), rows(MLSTM_WIDTH), rows(d), _full(wa.shape), _full(wm.shape),
                  _full((1, d)), _full((1, d))],
        out_specs=rows(d),
        out_shape=jax.ShapeDtypeStruct((n, d), F32),
        compiler_params=_params("arbitrary"),
        name="outproj",
    )(att, hm, h, wa, wm, g.reshape(1, d), b.reshape(1, d))


def _kvproj_kernel(mem_ref, w_ref, k_ref, v_ref):
    kv = _dot(mem_ref[...].astype(BF16), w_ref[...])
    d = k_ref.shape[-1]
    k_ref[...] = kv[:, :d].astype(BF16)
    v_ref[...] = kv[:, d:].astype(BF16)


def _kvproj(mem2, w_kv):
    n, d = mem2.shape
    w = w_kv.astype(BF16)
    blk = min(ROW_BLOCK, n)
    rows = pl.BlockSpec((blk, d), lambda i: (i, 0))
    return pl.pallas_call(
        _kvproj_kernel,
        grid=(n // blk,),
        in_specs=[rows, _full(w.shape)],
        out_specs=[rows, rows],
        out_shape=[jax.ShapeDtypeStruct((n, d), BF16)] * 2,
        compiler_params=_params("arbitrary"),
        name="kvproj",
    )(mem2, w)


def _xattn_kernel(h_ref, k_ref, v_ref, wq_ref, wo_ref, g_ref, b_ref, o_ref):
    h = h_ref[0]
    d = h.shape[-1]
    dh = d // XATT_HEADS
    q = (_dot(h.astype(BF16), wq_ref[...]) * (dh ** -0.5)).astype(BF16)
    outs = []
    for hd in range(XATT_HEADS):
        sl = slice(hd * dh, (hd + 1) * dh)
        s = _dot_nt(q[:, sl], k_ref[0, :, sl])
        m = jnp.max(s, -1, keepdims=True)
        p = jnp.exp(s - m)
        l = jnp.sum(p, -1, keepdims=True)
        outs.append((_dot(p.astype(BF16), v_ref[0, :, sl]) / l).astype(BF16))
    y = _dot(jnp.concatenate(outs, axis=-1), wo_ref[...])
    o_ref[0] = _layer_norm(DEEPNORM_ALPHA * h + y, g_ref[...], b_ref[...])


def _xattn(h3, k3, v3, w_q, w_o, g, b):
    batch, seq, d = h3.shape
    m = k3.shape[1]
    wq = w_q.astype(BF16)
    wo = w_o.astype(BF16)
    blk = pl.BlockSpec((1, ROW_BLOCK, d), lambda bb, i: (bb, i, 0))
    mem = pl.BlockSpec((1, m, d), lambda bb, i: (bb, 0, 0))
    return pl.pallas_call(
        _xattn_kernel,
        grid=(batch, seq // ROW_BLOCK),
        in_specs=[blk, mem, mem, _full(wq.shape), _full(wo.shape), _full((1, d)), _full((1, d))],
        out_specs=blk,
        out_shape=jax.ShapeDtypeStruct((batch, seq, d), F32),
        compiler_params=_params("arbitrary", "arbitrary"),
        name="xattn",
    )(h3, k3, v3, wq, wo, g.reshape(1, d), b.reshape(1, d))


def _topk_rows(s, k):
    groups = s.shape[0] // SUBLANES
    t = s.shape[1]
    sg = [s[g * SUBLANES:(g + 1) * SUBLANES] for g in range(groups)]
    sub = lax.broadcasted_iota(I32, (SUBLANES, t), 0)
    vals, ids = [], []
    for _ in range(k):
        m8, a8 = sg[0], jnp.zeros((SUBLANES, t), I32)
        for g in range(1, groups):
            upd = sg[g] > m8
            m8 = jnp.where(upd, sg[g], m8)
            a8 = jnp.where(upd, g, a8)
        m = jnp.max(m8, axis=0, keepdims=True)
        kid = jnp.min(jnp.where(m8 == m, sub * groups + a8, SUBLANES * groups), axis=0, keepdims=True)
        vals.append(m)
        ids.append(kid)
        gone = jnp.where(sub * groups + a8 == kid, a8, -1)
        sg = [jnp.where(gone == g, -jnp.inf, sg[g]) for g in range(groups)]
    return vals, ids


def _stack_rows(rows_list):
    k = len(rows_list)
    t = rows_list[0].shape[-1]
    iota = lax.broadcasted_iota(I32, (k, t), 0)
    out = jnp.broadcast_to(rows_list[0], (k, t))
    for r in range(1, k):
        out = jnp.where(iota == r, jnp.broadcast_to(rows_list[r], (k, t)), out)
    return out


def _peer_topk_kernel(h_ref, wq_ref, khi_ref, klo_ref, row_ref, hbit_ref, g_ref, e_scr, g_scr):
    t = h_ref.shape[0]
    kk = PEER_TOPK
    q = _dot(h_ref[...].astype(BF16), wq_ref[...])
    half8 = kk // 2
    sub = lax.broadcasted_iota(I32, (half8, t), 0)
    for hh in range(PEER_HEADS):
        tops = []
        for p in range(2):
            c0 = (hh * 2 + p) * N_KEYS
            qhi, qlo = _split2(q[:, c0:c0 + N_KEYS])
            s = _dot_nt(khi_ref[p], qhi) + _dot_nt(khi_ref[p], qlo) + _dot_nt(klo_ref[p], qhi)
            vals, idxs = _topk_rows(s, kk)
            tops.append((_stack_rows(vals), _stack_rows(idxs)))
        (s0, i0), (s1, i1) = tops
        e0 = i0 * N_KEYS
        cand, cexp, cflat = [], [], []
        for b in range(half8):
            cand.append(s0[:half8] + s1[b:b + 1])
            cexp.append(e0[:half8] + i1[b:b + 1])
            cflat.append(sub * kk + b)
        cand.append(s0[half8:] + s1[0:1])
        cexp.append(e0[half8:] + i1[0:1])
        cflat.append((sub + half8) * kk)
        cand.append(s0[0:1] + s1[half8:])
        cexp.append(e0[0:1] + i1[half8:])
        cflat.append(sub + half8)
        cand = jnp.concatenate(cand, axis=0)
        cexp = jnp.concatenate(cexp, axis=0)
        cflat = jnp.concatenate(cflat, axis=0)
        best_s, best_e = [], []
        for _ in range(kk):
            m = jnp.max(cand, axis=0, keepdims=True)
            jsel = jnp.min(jnp.where(cand == m, cflat, kk * kk), axis=0, keepdims=True)
            hit = cflat == jsel
            best_s.append(m)
            best_e.append(jnp.max(jnp.where(hit, cexp, 0), axis=0, keepdims=True))
            cand = jnp.where(hit, -jnp.inf, cand)
        bs = _stack_rows(best_s)
        ex = jnp.exp(bs - bs[0:1])
        g_scr[hh * kk:(hh + 1) * kk, :] = ex / jnp.sum(ex, axis=0, keepdims=True)
        e_scr[hh * kk:(hh + 1) * kk, :] = _stack_rows(best_e)
    e = e_scr[...]
    hb = e >> (PEER_HALF_EXPERTS.bit_length() - 1)
    row_ref[...] = ((e & (PEER_HALF_EXPERTS - 1)) * SUBLANES).T
    hbit_ref[...] = hb.astype(F32).T
    g_ref[...] = g_scr[...].T


def _peer_topk(h2, w_query, sub_keys):
    n, d = h2.shape
    wq = w_query.astype(BF16)
    keys = sub_keys.reshape(2, SUBLANES, N_KEYS // SUBLANES, -1).swapaxes(1, 2).reshape(sub_keys.shape)
    khi = keys.astype(BF16)
    klo = (keys - khi.astype(F32)).astype(BF16)
    rows = pl.BlockSpec((TOPK_BLOCK, PEER_PICKS), lambda i: (i, 0))
    return pl.pallas_call(
        _peer_topk_kernel,
        grid=(n // TOPK_BLOCK,),
        in_specs=[pl.BlockSpec((TOPK_BLOCK, d), lambda i: (i, 0)), _full(wq.shape), _full(khi.shape), _full(klo.shape)],
        out_specs=[rows, rows, rows],
        out_shape=[jax.ShapeDtypeStruct((n, PEER_PICKS), I32),
                   jax.ShapeDtypeStruct((n, PEER_PICKS), F32), jax.ShapeDtypeStruct((n, PEER_PICKS), F32)],
        scratch_shapes=[pltpu.VMEM((PEER_PICKS, TOPK_BLOCK), I32), pltpu.VMEM((PEER_PICKS, TOPK_BLOCK), F32)],
        compiler_params=_params("arbitrary"),
        name="peer_topk",
    )(h2, wq, khi, klo)


def _pack_table(tab):
    e, d = tab.shape
    bits = lax.bitcast_convert_type(tab.astype(BF16), jnp.uint16).astype(U32)
    packed = bits[:e // 2] | (bits[e // 2:] << 16)
    return packed.reshape(e // 2 * (d // LANES), LANES)


def _table_tile(tab_ref, row8):
    return pltpu.bitcast(tab_ref[pl.ds(pl.multiple_of(row8, SUBLANES), SUBLANES), :], BF16)


def _peer_u_consts():
    k = jnp.arange(2 * LANES)
    col = jnp.arange(2 * PEER_PICKS)
    row = jnp.arange(PEER_PICKS)
    fold = ((k[None, :] // 16 == jnp.arange(2 * U_TILE_PICKS)[:, None] // 2)
            & (k[None, :] % 2 == jnp.arange(2 * U_TILE_PICKS)[:, None] % 2)).astype(BF16)
    q = jnp.arange(LANES)
    r = jnp.arange(SUBLANES)[:, None, None]
    rowq = 8 * ((q % 8) // 2)[None, :, None] + r
    colq = 2 * (32 * (q // 8)[None, :, None] + 2 * (rowq // 2) + (q % 2)[None, :, None]) + rowq % 2
    place = ((col[None, None, :] == colq) & (q < 32)[None, :, None]).astype(BF16)
    dup = ((col[None, :] // 2) == jnp.arange(PEER_PICKS)[:, None]).astype(BF16)
    return fold, place, dup


def _index_copy(row_hbm, idx_smem, sem, batch, slot):
    return pltpu.make_async_copy(row_hbm.at[pl.ds(batch * PEER_UNROLL, PEER_UNROLL)], idx_smem.at[slot], sem.at[slot])


def _for_each_token_batch(row_hbm, idx_smem, sem, block_tokens, body):
    step = pl.program_id(0)
    per_step = block_tokens // PEER_UNROLL
    assert per_step % 2 == 0
    total = pl.num_programs(0) * per_step

    @pl.when(step == 0)
    def _():
        _index_copy(row_hbm, idx_smem, sem, 0, 0).start()

    def pair(k, carry):
        for slot in range(2):
            local = 2 * k + slot
            batch = step * per_step + local
            _index_copy(row_hbm, idx_smem, sem, batch, slot).wait()

            @pl.when(batch + 1 < total)
            def _():
                _index_copy(row_hbm, idx_smem, sem, batch + 1, 1 - slot).start()

            body(local * PEER_UNROLL, idx_smem.at[slot])
        return carry

    lax.fori_loop(0, per_step // 2, pair, 0)


def _peer_u_kernel(row_hbm, x_ref, g_ref, hbit_ref, tab_ref, fold_ref, place_ref, dup_ref,
                   coefh_ref, sum_scr, idx_smem, sem):
    tb = x_ref.shape[0]
    ntiles = PEER_PICKS // (2 * U_TILE_PICKS)
    lane = lax.broadcasted_iota(I32, (SUBLANES, LANES), 1)

    def tokens(t0, idx):
        for u in range(PEER_UNROLL):
            t = t0 + u
            x = x_ref[t]
            xbits = pltpu.bitcast(x.astype(BF16).astype(F32), U32)
            xw = pltpu.bitcast(xbits | (xbits >> 16), BF16)
            sums = jnp.zeros((SUBLANES, LANES), F32)
            for j in range(ntiles):
                cols = []
                for ab in range(2):
                    prods = []
                    for mm in range(U_TILE_PICKS):
                        c = j * 2 * U_TILE_PICKS + 2 * mm + ab
                        prods.append(_table_tile(tab_ref, idx[u, c]) * xw)
                    cols.append(jnp.concatenate(prods, axis=0))
                res = _dot(fold_ref[...], jnp.concatenate(cols, axis=1))
                for rb in range(2 * U_TILE_PICKS // SUBLANES):
                    for ab in range(2):
                        part = res[rb * SUBLANES:(rb + 1) * SUBLANES, ab * LANES:(ab + 1) * LANES]
                        q = j * 8 + rb * 2 + ab
                        sums = jnp.where(lane == q, jnp.sum(part, axis=1, keepdims=True), sums)
            sum_scr[t] = sums

    _for_each_token_batch(row_hbm, idx_smem, sem, tb, tokens)

    a2 = jnp.zeros((tb, 2 * PEER_PICKS), F32)
    for r in range(SUBLANES):
        shi, slo = _split2(sum_scr[:, r, :])
        a2 = a2 + _dot(shi, place_ref[r]) + _dot(slo, place_ref[r])
    ghi, glo = _split2(g_ref[...])
    g2 = _dot(ghi, dup_ref[...]) + _dot(glo, dup_ref[...])
    hbit2 = _dot(hbit_ref[...].astype(BF16), dup_ref[...])
    parity = (lax.broadcasted_iota(I32, a2.shape, 1) % 2).astype(F32)
    coef = g2 * (0.5 * a2 * (1.0 + lax.erf(a2 * math.sqrt(0.5))))
    coefh_ref[...] = jnp.where(hbit2 == parity, coef, 0.0).astype(BF16)


def _peer_v_consts():
    col = jnp.arange(2 * PEER_PICKS * SUBLANES)
    ab, p, r = col // (PEER_PICKS * SUBLANES), (col // 16) % (PEER_PICKS // 2), col % 16
    src = jnp.arange(2 * PEER_PICKS)
    expand = ((src[:, None] // 2 == (2 * p + ab)[None, :]) & (src[:, None] % 2 == (r % 2)[None, :])).astype(BF16)
    kk = jnp.arange(PEER_PICKS * SUBLANES)
    diag = ((kk[None, :] % 16) // 2 == (jnp.arange(16) % SUBLANES)[:, None]).astype(F32)
    return expand, diag


def _peer_v_kernel(row_hbm, coefh_ref, x_ref, tab_ref, expand_ref, diag_ref, g_ref, b_ref, o_ref, lrow_scr, y_scr,
                   idx_smem, sem):
    tb = x_ref.shape[0]
    kdim = PEER_PICKS * SUBLANES
    lrow_scr[...] = _dot(coefh_ref[...], expand_ref[...])
    diag = diag_ref[...]

    def tokens(t0, idx):
        for u in range(PEER_UNROLL):
            t = t0 + u
            even = jnp.broadcast_to(lrow_scr[pl.ds(t, 1), 0:kdim], (SUBLANES, kdim))
            odd = jnp.broadcast_to(lrow_scr[pl.ds(t, 1), kdim:2 * kdim], (SUBLANES, kdim))
            lhs = (jnp.concatenate([even, odd], axis=0) * diag).astype(BF16)
            tiles = []
            for p in range(PEER_PICKS // 2):
                wa = _table_tile(tab_ref, idx[u, 2 * p])
                wb = _table_tile(tab_ref, idx[u, 2 * p + 1])
                tiles.append(jnp.concatenate([wa, wb], axis=1))
            res = _dot(lhs, jnp.concatenate(tiles, axis=0))
            y_scr[t] = res[0:SUBLANES, 0:LANES] + res[SUBLANES:, LANES:]

    _for_each_token_batch(row_hbm, idx_smem, sem, tb, tokens)
    z = DEEPNORM_ALPHA * x_ref[...] + y_scr[...]
    cnt = z.shape[1] * z.shape[2]
    mu = jnp.sum(jnp.sum(z, axis=2, keepdims=True), axis=1, keepdims=True) / cnt
    zc = z - mu
    var = jnp.sum(jnp.sum(zc * zc, axis=2, keepdims=True), axis=1, keepdims=True) / cnt
    o_ref[...] = zc * lax.rsqrt(var + LN_EPS) * g_ref[...] + b_ref[...]


def _peer_ffn(h2, row, hbit, gate, expert_u, expert_v, ln_g, ln_b):
    n, d = h2.shape
    rows8 = d // LANES
    assert rows8 == SUBLANES and PEER_PICKS == LANES and expert_u.shape[0] == 2 * PEER_HALF_EXPERTS
    x3 = h2.reshape(n, rows8, LANES)
    u_pk = _pack_table(expert_u)
    v_pk = _pack_table(expert_v)
    hbm = pl.BlockSpec(memory_space=pl.ANY)
    staging = [pltpu.SMEM((2, PEER_UNROLL, PEER_PICKS), I32), pltpu.SemaphoreType.DMA((2,))]
    picks =pl.BlockSpec((PEER_BLOCK, PEER_PICKS), lambda i: (i, 0))
    picks2 = pl.BlockSpec((PEER_BLOCK, 2 * PEER_PICKS), lambda i: (i, 0))
    tok = pl.BlockSpec((PEER_BLOCK, rows8, LANES), lambda i: (i, 0, 0))
    table = pl.BlockSpec(u_pk.shape, lambda i: (0, 0), pipeline_mode=pl.Buffered(1))
    u_consts = _peer_u_consts()
    coefh = pl.pallas_call(
        _peer_u_kernel,
        grid=(n // PEER_BLOCK,),
        in_specs=[hbm, tok, picks, picks, table] + [_full(c.shape) for c in u_consts],
        out_specs=picks2,
        out_shape=jax.ShapeDtypeStruct((n, 2 * PEER_PICKS), BF16),
        scratch_shapes=[pltpu.VMEM((PEER_BLOCK, SUBLANES, LANES), F32)] + staging,
        compiler_params=_params("arbitrary"),
        name="peer_u",
    )(row, x3, gate, hbit, u_pk, *u_consts)
    v_consts = _peer_v_consts()
    out = pl.pallas_call(
        _peer_v_kernel,
        grid=(n // PEER_BLOCK,),
        in_specs=[hbm, picks2, tok, table] + [_full(c.shape) for c in v_consts]
        + [_full((rows8, LANES)), _full((rows8, LANES))],
        out_specs=tok,
        out_shape=jax.ShapeDtypeStruct((n, rows8, LANES), F32),
        scratch_shapes=[pltpu.VMEM((PEER_BLOCK, 2 * PEER_PICKS * SUBLANES), F32),
                        pltpu.VMEM((PEER_BLOCK, rows8, LANES), F32)] + staging,
        compiler_params=_params("arbitrary"),
        name="peer_v",
    )(row, coefh, x3, v_pk, *v_consts, ln_g.reshape(rows8, LANES), ln_b.reshape(rows8, LANES))
    return out.reshape(n, d)


def kernel(x, mem, ln_in_g, ln_in_b, w_in, conv_w, conv_b, mlstm_i_bias, mlstm_f_bias, mlstm_norm_g, rel_bias, w_out, ln1_g, ln1_b, xattn_w_q, xattn_w_kv, xattn_w_o, ln2_g, ln2_b, peer_w_query, peer_sub_keys, peer_u, peer_v, ln3_g, ln3_b):
    batch, seq, d = x.shape
    n = batch * seq
    assert w_in.shape[0] == DEPTH
    h, qkv, mqk, mv, mo, gates = _ln_inproj(x.reshape(n, d), ln_in_g, ln_in_b, w_in[0])
    for l in range(DEPTH):
        if l > 0:
            raise NotImplementedError("input projection of deeper layers")
        att = _attention(qkv, rel_bias[l], batch, seq)
        hm = _mlstm(mqk, mv, mo, gates, conv_w[l], conv_b[l], mlstm_i_bias[l], mlstm_f_bias[l], mlstm_norm_g[l],
                    batch, seq)
        h = _outproj(att.reshape(n, ATT_WIDTH), hm.reshape(n, MLSTM_WIDTH), h, w_out[l], ln1_g[l], ln1_b[l])
        k2, v2 = _kvproj(mem.reshape(-1, d), xattn_w_kv[l])
        m = mem.shape[1]
        h = _xattn(h.reshape(batch, seq, d), k2.reshape(batch, m, d), v2.reshape(batch, m, d),
                   xattn_w_q[l], xattn_w_o[l], ln2_g[l], ln2_b[l]).reshape(n, d)
        row, hbit, gate = _peer_topk(h, peer_w_query[l], peer_sub_keys[l])
        h = _peer_ffn(h, row, hbit, gate, peer_u[l], peer_v[l], ln3_g[l], ln3_b[l])
    return h.reshape(batch, seq, d)
```

```python
import functools
import math

import jax
import jax.numpy as jnp
from jax import lax
from jax.experimental import pallas as pl
from jax.experimental.pallas import tpu as pltpu

F32, BF16, I32, U32 = jnp.float32, jnp.bfloat16, jnp.int32, jnp.uint32

DEPTH = 1
LN_EPS = 1e-5
DEEPNORM_ALPHA = (2.0 * DEPTH) ** 0.25
CHUNK = 64
ATT_HEADS, ATT_HEAD_DIM, ATT_LEFT_CHUNKS, MAX_REL = 8, 64, 8, 128
ATT_WIDTH = ATT_HEADS * ATT_HEAD_DIM
MLSTM_HEADS, MLSTM_HEAD_DIM, CONV_WIDTH = 4, 128, 4
MLSTM_WIDTH = MLSTM_HEADS * MLSTM_HEAD_DIM
XATT_HEADS = 4
PEER_HEADS, N_KEYS, PEER_TOPK = 8, 128, 16
PEER_PICKS = PEER_HEADS * PEER_TOPK
PEER_HALF_EXPERTS = N_KEYS * N_KEYS // 2
U_TILE_PICKS = 16
PEER_UNROLL = 16

LANES = 128
SUBLANES = 8
VMEM_LIMIT_BYTES = 56 * 1024 * 1024

NEG_BIG = -1e30

ROW_BLOCK = 512
ATT_BLOCK = 256
ATT_KEY_BLOCKS = 3
MLSTM_BLOCK = 256
MLSTM_SEQS = 1
TOPK_BLOCK = 256
PEER_BLOCK = 128


def _params(*semantics):
    return pltpu.CompilerParams(dimension_semantics=semantics, vmem_limit_bytes=VMEM_LIMIT_BYTES)


def _layer_norm(x, g, b):
    mu = jnp.mean(x, -1, keepdims=True)
    xc = x - mu
    var = jnp.mean(xc * xc, -1, keepdims=True)
    return xc * lax.rsqrt(var + LN_EPS) * g + b


def _split2(x):
    hi = x.astype(BF16)
    lo = (x - hi.astype(F32)).astype(BF16)
    return hi, lo


def _split3(x):
    hi = x.astype(BF16)
    r = x - hi.astype(F32)
    mid = r.astype(BF16)
    lo = (r - mid.astype(F32)).astype(BF16)
    return hi, mid, lo


def _dot(a, b):
    return jnp.dot(a, b, preferred_element_type=F32)


def _dot_nt(a, b):
    return lax.dot_general(a, b, (((1,), (1,)), ((), ())), preferred_element_type=F32)


def _dot_tn(a, b):
    return lax.dot_general(a, b, (((0,), (0,)), ((), ())), preferred_element_type=F32)


def _full(shape):
    return pl.BlockSpec(shape, lambda *_: (0,) * len(shape))


def _ln_inproj_kernel(x_ref, g_ref, b_ref, wqkv_ref, wmqk_ref, wmv_ref, wmo_ref, wghi_ref, wglo_ref,
                      h_ref, qkv_ref, mqk_ref, mv_ref, mo_ref, gate_ref):
    h = _layer_norm(x_ref[...], g_ref[...], b_ref[...])
    h_ref[...] = h
    hb, hlo = _split2(h)
    qkv_ref[...] = _dot(hb, wqkv_ref[...]).astype(BF16)
    mqk_ref[...] = _dot(hb, wmqk_ref[...]).astype(BF16)
    mv_ref[...] = _dot(hb, wmv_ref[...]).astype(BF16)
    mo_ref[...] = _dot(hb, wmo_ref[...]).astype(BF16)
    gate_ref[...] = _dot(hb, wghi_ref[...]) + _dot(hlo, wghi_ref[...]) + _dot(hb, wglo_ref[...])


def _ln_inproj(x2, g, b, w_in):
    n, d = x2.shape
    a3 = 3 * ATT_WIDTH
    wqkv = w_in[:, :a3].astype(BF16)
    wmqk = w_in[:, a3:a3 + 2 * MLSTM_WIDTH].astype(BF16)
    wmv = w_in[:, a3 + 2 * MLSTM_WIDTH:a3 + 3 * MLSTM_WIDTH].astype(BF16)
    wmo = w_in[:, a3 + 3 * MLSTM_WIDTH:a3 + 4 * MLSTM_WIDTH].astype(BF16)
    wg = jnp.pad(w_in[:, a3 + 4 * MLSTM_WIDTH:], ((0, 0), (0, LANES - 2 * MLSTM_HEADS)))
    wghi = wg.astype(BF16)
    wglo = (wg - wghi.astype(F32)).astype(BF16)
    rows = lambda w: pl.BlockSpec((ROW_BLOCK, w), lambda i: (i, 0))
    return pl.pallas_call(
        _ln_inproj_kernel,
        grid=(n // ROW_BLOCK,),
        in_specs=[rows(d), _full((1, d)), _full((1, d)), _full(wqkv.shape), _full(wmqk.shape),
                  _full(wmv.shape), _full(wmo.shape), _full(wghi.shape), _full(wglo.shape)],
        out_specs=[rows(d), rows(a3), rows(2 * MLSTM_WIDTH), rows(MLSTM_WIDTH), rows(MLSTM_WIDTH), rows(LANES)],
        out_shape=[jax.ShapeDtypeStruct((n, d), F32), jax.ShapeDtypeStruct((n, a3), BF16),
                   jax.ShapeDtypeStruct((n, 2 * MLSTM_WIDTH), BF16), jax.ShapeDtypeStruct((n, MLSTM_WIDTH), BF16),
                   jax.ShapeDtypeStruct((n, MLSTM_WIDTH), BF16), jax.ShapeDtypeStruct((n, LANES), F32)],
        compiler_params=_params("arbitrary"),
        name="ln_inproj",
    )(x2, g.reshape(1, d), b.reshape(1, d), wqkv, wmqk, wmv, wmo, wghi, wglo)


def _attn_kernel(q_ref, k0_ref, k1_ref, k2_ref, v0_ref, v1_ref, v2_ref, bias_ref, o_ref):
    i = pl.program_id(1)
    nkeys = ATT_KEY_BLOCKS * ATT_BLOCK
    q = q_ref[0] * (ATT_HEAD_DIM ** -0.5)
    kcat = jnp.concatenate([k0_ref[0], k1_ref[0], k2_ref[0]], axis=0)
    vcat = jnp.concatenate([v0_ref[0], v1_ref[0], v2_ref[0]], axis=0)
    col = lax.broadcasted_iota(I32, (1, nkeys), 1)
    in_seq = col >= (ATT_KEY_BLOCKS - 1 - i) * ATT_BLOCK
    lane = lax.broadcasted_iota(I32, (1, LANES), 1)
    low = lane < ATT_HEAD_DIM
    outs = []
    for pair in range(ATT_HEADS // 2):
        sl = slice(pair * LANES, (pair + 1) * LANES)
        qp, kp, vp = q[:, sl], kcat[:, sl], vcat[:, sl]
        halves = []
        for half in range(2):
            keep = low if half == 0 else jnp.logical_not(low)
            qh = jnp.where(keep, qp, jnp.zeros_like(qp))
            s = _dot_nt(qh, kp) + bias_ref[2 * pair + half]
            s = jnp.where(in_seq, s, NEG_BIG)
            m = jnp.max(s, -1, keepdims=True)
            p = jnp.exp(s - m)
            l = jnp.sum(p, -1, keepdims=True)
            halves.append(_dot(p.astype(BF16), vp) / l)
        outs.append(jnp.where(low, halves[0], halves[1]))
    o_ref[0] = jnp.concatenate(outs, axis=-1).astype(BF16)


def _attn_bias(rel_bias):
    nq, nk = ATT_BLOCK, ATT_KEY_BLOCKS * ATT_BLOCK
    r = jnp.arange(nq)[:, None]
    j = jnp.arange(nk)[None, :]
    allowed = (j // CHUNK >= r // CHUNK) & (j // CHUNK <= r // CHUNK + ATT_LEFT_CHUNKS)
    period = nq + nk - 1
    i = jnp.arange(period)
    d = jnp.where(i < nk, i, i - period)
    rel = d - (ATT_KEY_BLOCKS - 1) * ATT_BLOCK
    line = rel_bias[:, jnp.clip(rel, -MAX_REL, MAX_REL) + MAX_REL].astype(F32)
    rep = jnp.tile(line, (1, nq + 1))[:, :nq * (period - 1)]
    tab = rep.reshape(-1, nq, period - 1)[:, :, :nk]
    return jnp.where(allowed[None], tab, NEG_BIG)


def _attention(qkv, rel_bias, batch, seq):
    assert ATT_LEFT_CHUNKS * CHUNK == (ATT_KEY_BLOCKS - 1) * ATT_BLOCK
    qkv3 = qkv.reshape(batch, seq, 3 * ATT_WIDTH)
    bias = _attn_bias(rel_bias)
    blk = (1, ATT_BLOCK, ATT_WIDTH)

    def kv_spec(col, j):
        return pl.BlockSpec(blk, lambda b, i: (b, jnp.maximum(i - (ATT_KEY_BLOCKS - 1) + j, 0), col))

    return pl.pallas_call(
        _attn_kernel,
        grid=(batch, seq // ATT_BLOCK),
        in_specs=[pl.BlockSpec(blk, lambda b, i: (b, i, 0))]
        + [kv_spec(1, j) for j in range(ATT_KEY_BLOCKS)] + [kv_spec(2, j) for j in range(ATT_KEY_BLOCKS)]
        + [_full(bias.shape)],
        out_specs=pl.BlockSpec(blk, lambda b, i: (b, i, 0)),
        out_shape=jax.ShapeDtypeStruct((batch, seq, ATT_WIDTH), BF16),
        compiler_params=_params("arbitrary", "arbitrary"),
        name="attn",
    )(qkv3, qkv3, qkv3, qkv3, qkv3, qkv3, qkv3, bias)


def _mlstm_kernel(mqk_ref, mv_ref, mo_ref, gate_ref, convw_ref, convb_ref, gbias_ref, ng_ref, out_ref,
                  xpad, q_scr, k_scr, gate_scr, logf_scr, c_st, n_st, m_st):
    j = pl.program_id(1)
    rows = MLSTM_BLOCK
    d = MLSTM_HEAD_DIM

    @pl.when(j == 0)
    def _():
        xpad[:, 0:SUBLANES, :] = jnp.zeros((MLSTM_SEQS, SUBLANES, 2 * MLSTM_WIDTH), F32)
        c_st[...] = jnp.zeros_like(c_st)
        n_st[...] = jnp.zeros_like(n_st)
        m_st[...] = jnp.zeros_like(m_st)

    for bb in range(MLSTM_SEQS):
        xpad[bb, SUBLANES:SUBLANES + rows, :] = mqk_ref[bb].astype(F32)
        acc = jnp.broadcast_to(convb_ref[...], (rows, 2 * MLSTM_WIDTH))
        for t in range(CONV_WIDTH):
            acc = acc + convw_ref[t:t + 1, :] * xpad[bb, pl.ds(SUBLANES - (CONV_WIDTH - 1) + t, rows), :]
        xpad[bb, 0:SUBLANES, :] = xpad[bb, rows:rows + SUBLANES, :]
        qk = acc * jax.nn.sigmoid(acc)
        q_scr[bb] = qk[:, :MLSTM_WIDTH].astype(BF16)
        k_scr[bb] = qk[:, MLSTM_WIDTH:] * (d ** -0.5)

        gates = gate_ref[bb] + gbias_ref[...]
        gate_scr[bb] = gates
        logf_scr[bb] = jax.nn.log_sigmoid(gates)

    ri = lax.broadcasted_iota(I32, (CHUNK, CHUNK), 0)
    ci = lax.broadcasted_iota(I32, (CHUNK, CHUNK), 1)
    causal = ci <= ri
    tri = causal.astype(BF16)

    def chunk_body(c, carry):
        for bb in range(MLSTM_SEQS):
            _mlstm_chunk(bb, pl.multiple_of(c * CHUNK, CHUNK), causal, tri, mv_ref, mo_ref, ng_ref, out_ref,
                         q_scr, k_scr, gate_scr, logf_scr, c_st, n_st, m_st)
        return carry

    lax.fori_loop(0, rows // CHUNK, chunk_body, 0)


def _mlstm_chunk(bb, r0, causal, tri, mv_ref, mo_ref, ng_ref, out_ref, q_scr, k_scr, gate_scr, logf_scr,
                 c_st, n_st, m_st):
    d = MLSTM_HEAD_DIM
    g = gate_scr[bb, pl.ds(r0, CHUNK), :]
    lf = logf_scr[bb, pl.ds(r0, CHUNK), :]
    b3 = _dot(tri, jnp.concatenate(_split3(lf), axis=1))
    bcol = b3[:, :LANES] + b3[:, LANES:2 * LANES] + b3[:, 2 * LANES:]
    g_t = g.T
    b_t = bcol.T
    m_all = m_st[bb]
    n_all = n_st[bb]
    for h in range(MLSTM_HEADS):
        hs = slice(h * d, (h + 1) * d)
        ic = g[:, h:h + 1]
        bc = bcol[:, MLSTM_HEADS + h:MLSTM_HEADS + h + 1]
        ir = g_t[h:h + 1, :]
        br = b_t[MLSTM_HEADS + h:MLSTM_HEADS + h + 1, :]
        m_prev = m_all[h:h + 1, 0:1]
        n_prev = n_all[h:h + 1, :]
        c_prev = c_st[bb, h]
        qh = q_scr[bb, pl.ds(r0, CHUNK), hs]
        kh = k_scr[bb, pl.ds(r0, CHUNK), hs]
        vh = mv_ref[bb, pl.ds(r0, CHUNK), hs]

        log_d = jnp.where(causal, bc - br + ir, NEG_BIG)
        inter = bc + m_prev
        m_t = jnp.maximum(inter, jnp.max(log_d, -1, keepdims=True))
        d_mat = jnp.exp(log_d - m_t)
        w_inter = jnp.exp(inter - m_t)
        qk_d = _dot_nt(qh, kh.astype(BF16)) * d_mat
        lhs = jnp.concatenate([(w_inter * qh.astype(F32)).astype(BF16), qk_d.astype(BF16)], axis=1)
        num = _dot(lhs, jnp.concatenate([c_prev.astype(BF16), vh], axis=0))
        den = (w_inter * jnp.sum(qh.astype(F32) * n_prev, -1, keepdims=True)
               + jnp.sum(qk_d, -1, keepdims=True))
        hh = num / jnp.maximum(jnp.abs(den), jnp.exp(-m_t))

        b_last = bc[CHUNK - 1:CHUNK, :]
        log_in = b_last - bc + ic
        m_new = jnp.maximum(b_last + m_prev, jnp.max(log_in, 0, keepdims=True))
        w_prev = jnp.exp(b_last + m_prev - m_new)
        kw = kh * jnp.exp(log_in - m_new)
        c_st[bb, h] = w_prev * c_prev + _dot_tn(kw.astype(BF16), vh)
        n_st[bb, h:h + 1, :] = w_prev * n_prev + jnp.sum(kw, 0, keepdims=True)
        m_st[bb, h:h + 1, :] = jnp.broadcast_to(m_new, (1, LANES))

        mu = jnp.mean(hh, -1, keepdims=True)
        hc = hh - mu
        var = jnp.mean(hc * hc, -1, keepdims=True)
        hn = hc * lax.rsqrt(var + LN_EPS) * ng_ref[:, hs]
        og = jax.nn.sigmoid(mo_ref[bb, pl.ds(r0, CHUNK), hs].astype(F32))
        out_ref[bb, pl.ds(r0, CHUNK), hs] = (og * hn).astype(BF16)


def _mlstm(mqk, mv, mo, gates, conv_w, conv_b, i_bias, f_bias, norm_g, batch, seq):
    w2 = 2 * MLSTM_WIDTH
    gbias = jnp.pad(jnp.concatenate([i_bias, f_bias]).astype(F32), (0, LANES - 2 * MLSTM_HEADS)).reshape(1, LANES)
    ns = MLSTM_SEQS
    assert batch % ns == 0
    blk = lambda w: pl.BlockSpec((ns, MLSTM_BLOCK, w), lambda b, i: (b, i, 0))
    return pl.pallas_call(
        _mlstm_kernel,
        grid=(batch // ns, seq // MLSTM_BLOCK),
        in_specs=[blk(w2), blk(MLSTM_WIDTH), blk(MLSTM_WIDTH), blk(LANES),
                  _full((CONV_WIDTH, w2)), _full((1, w2)), _full((1, LANES)), _full((1, MLSTM_WIDTH))],
        out_specs=blk(MLSTM_WIDTH),
        out_shape=jax.ShapeDtypeStruct((batch, seq, MLSTM_WIDTH), BF16),
        scratch_shapes=[pltpu.VMEM((ns, MLSTM_BLOCK + SUBLANES, w2), F32),
                        pltpu.VMEM((ns, MLSTM_BLOCK, MLSTM_WIDTH), BF16),
                        pltpu.VMEM((ns, MLSTM_BLOCK, MLSTM_WIDTH), F32),
                        pltpu.VMEM((ns, MLSTM_BLOCK, LANES), F32),
                        pltpu.VMEM((ns, MLSTM_BLOCK, LANES), F32),
                        pltpu.VMEM((ns, MLSTM_HEADS, MLSTM_HEAD_DIM, MLSTM_HEAD_DIM), F32),
                        pltpu.VMEM((ns, SUBLANES, MLSTM_HEAD_DIM), F32),
                        pltpu.VMEM((ns, SUBLANES, LANES), F32)],
        compiler_params=_params("arbitrary", "arbitrary"),
        name="mlstm",
    )(mqk.reshape(batch, seq, w2), mv.reshape(batch, seq, MLSTM_WIDTH), mo.reshape(batch, seq, MLSTM_WIDTH),
      gates.reshape(batch, seq, LANES), conv_w.astype(F32), conv_b.reshape(1, w2).astype(F32), gbias,
      norm_g.reshape(1, MLSTM_WIDTH).astype(F32))


def _outproj_kernel(att_ref, hm_ref, h_ref, wa_ref, wm_ref, g_ref, b_ref, o_ref):
    y = _dot(att_ref[...], wa_ref[...]) + _dot(hm_ref[...], wm_ref[...])
    o_ref[...] = _layer_norm(DEEPNORM_ALPHA * h_ref[...] + y, g_ref[...], b_ref[...])


def _outproj(att, hm, h, w_out, g, b):
    n, d = h.shape
    wa = w_out[:ATT_WIDTH].astype(BF16)
    wm = w_out[ATT_WIDTH:].astype(BF16)
    rows = lambda w: pl.BlockSpec((ROW_BLOCK, w), lambda i: (i, 0))
    return pl.pallas_call(
        _outproj_kernel,
        grid=(n // ROW_BLOCK,),
        in_specs=[rows(ATT_WIDTH), rows(MLSTM_WIDTH), rows(d), _full(wa.shape), _full(wm.shape),
                  _full((1, d)), _full((1, d))],
        out_specs=rows(d),
        out_shape=jax.ShapeDtypeStruct((n, d), F32),
        compiler_params=_params("arbitrary"),
        name="outproj",
    )(att, hm, h, wa, wm, g.reshape(1, d), b.reshape(1, d))


def _kvproj_kernel(mem_ref, w_ref, k_ref, v_ref):
    kv = _dot(mem_ref[...].astype(BF16), w_ref[...])
    d = k_ref.shape[-1]
    k_ref[...] = kv[:, :d].astype(BF16)
    v_ref[...] = kv[:, d:].astype(BF16)


def _kvproj(mem2, w_kv):
    n, d = mem2.shape
    w = w_kv.astype(BF16)
    blk = min(ROW_BLOCK, n)
    rows = pl.BlockSpec((blk, d), lambda i: (i, 0))
    return pl.pallas_call(
        _kvproj_kernel,
        grid=(n // blk,),
        in_specs=[rows, _full(w.shape)],
        out_specs=[rows, rows],
        out_shape=[jax.ShapeDtypeStruct((n, d), BF16)] * 2,
        compiler_params=_params("arbitrary"),
        name="kvproj",
    )(mem2, w)


def _xattn_kernel(h_ref, k_ref, v_ref, wq_ref, wo_ref, g_ref, b_ref, o_ref):
    h = h_ref[0]
    d = h.shape[-1]
    dh = d // XATT_HEADS
    q = (_dot(h.astype(BF16), wq_ref[...]) * (dh ** -0.5)).astype(BF16)
    outs = []
    for hd in range(XATT_HEADS):
        sl = slice(hd * dh, (hd + 1) * dh)
        s = _dot_nt(q[:, sl], k_ref[0, :, sl])
        m = jnp.max(s, -1, keepdims=True)
        p = jnp.exp(s - m)
        l = jnp.sum(p, -1, keepdims=True)
        outs.append((_dot(p.astype(BF16), v_ref[0, :, sl]) / l).astype(BF16))
    y = _dot(jnp.concatenate(outs, axis=-1), wo_ref[...])
    o_ref[0] = _layer_norm(DEEPNORM_ALPHA * h + y, g_ref[...], b_ref[...])


def _xattn(h3, k3, v3, w_q, w_o, g, b):
    batch, seq, d = h3.shape
    m = k3.shape[1]
    wq = w_q.astype(BF16)
    wo = w_o.astype(BF16)
    blk = pl.BlockSpec((1, ROW_BLOCK, d), lambda bb, i: (bb, i, 0))
    mem = pl.BlockSpec((1, m, d), lambda bb, i: (bb, 0, 0))
    return pl.pallas_call(
        _xattn_kernel,
        grid=(batch, seq // ROW_BLOCK),
        in_specs=[blk, mem, mem, _full(wq.shape), _full(wo.shape), _full((1, d)), _full((1, d))],
        out_specs=blk,
        out_shape=jax.ShapeDtypeStruct((batch, seq, d), F32),
        compiler_params=_params("arbitrary", "arbitrary"),
        name="xattn",
    )(h3, k3, v3, wq, wo, g.reshape(1, d), b.reshape(1, d))


def _topk_rows(s, k):
    groups = s.shape[0] // SUBLANES
    t = s.shape[1]
    sg = [s[g * SUBLANES:(g + 1) * SUBLANES] for g in range(groups)]
    sub = lax.broadcasted_iota(I32, (SUBLANES, t), 0)
    vals, ids = [], []
    for _ in range(k):
        m8, a8 = sg[0], jnp.zeros((SUBLANES, t), I32)
        for g in range(1, groups):
            upd = sg[g] > m8
            m8 = jnp.where(upd, sg[g], m8)
            a8 = jnp.where(upd, g, a8)
        m = jnp.max(m8, axis=0, keepdims=True)
        kid = jnp.min(jnp.where(m8 == m, sub * groups + a8, SUBLANES * groups), axis=0, keepdims=True)
        vals.append(m)
        ids.append(kid)
        gone = jnp.where(sub * groups + a8 == kid, a8, -1)
        sg = [jnp.where(gone == g, -jnp.inf, sg[g]) for g in range(groups)]
    return vals, ids


def _stack_rows(rows_list):
    k = len(rows_list)
    t = rows_list[0].shape[-1]
    iota = lax.broadcasted_iota(I32, (k, t), 0)
    out = jnp.broadcast_to(rows_list[0], (k, t))
    for r in range(1, k):
        out = jnp.where(iota == r, jnp.broadcast_to(rows_list[r], (k, t)), out)
    return out


def _peer_topk_kernel(h_ref, wq_ref, khi_ref, klo_ref, row_ref, hbit_ref, g_ref, e_scr, g_scr):
    t = h_ref.shape[0]
    kk = PEER_TOPK
    q = _dot(h_ref[...].astype(BF16), wq_ref[...])
    half8 = kk // 2
    sub = lax.broadcasted_iota(I32, (half8, t), 0)
    for hh in range(PEER_HEADS):
        tops = []
        for p in range(2):
            c0 = (hh * 2 + p) * N_KEYS
            qhi, qlo = _split2(q[:, c0:c0 + N_KEYS])
            s = _dot_nt(khi_ref[p], qhi) + _dot_nt(khi_ref[p], qlo) + _dot_nt(klo_ref[p], qhi)
            vals, idxs = _topk_rows(s, kk)
            tops.append((_stack_rows(vals), _stack_rows(idxs)))
        (s0, i0), (s1, i1) = tops
        e0 = i0 * N_KEYS
        cand, cexp, cflat = [], [], []
        for b in range(half8):
            cand.append(s0[:half8] + s1[b:b + 1])
            cexp.append(e0[:half8] + i1[b:b + 1])
            cflat.append(sub * kk + b)
        cand.append(s0[half8:] + s1[0:1])
        cexp.append(e0[half8:] + i1[0:1])
        cflat.append((sub + half8) * kk)
        cand.append(s0[0:1] + s1[half8:])
        cexp.append(e0[0:1] + i1[half8:])
        cflat.append(sub + half8)
        cand = jnp.concatenate(cand, axis=0)
        cexp = jnp.concatenate(cexp, axis=0)
        cflat = jnp.concatenate(cflat, axis=0)
        best_s, best_e = [], []
        for _ in range(kk):
            m = jnp.max(cand, axis=0, keepdims=True)
            jsel = jnp.min(jnp.where(cand == m, cflat, kk * kk), axis=0, keepdims=True)
            hit = cflat == jsel
            best_s.append(m)
            best_e.append(jnp.max(jnp.where(hit, cexp, 0), axis=0, keepdims=True))
            cand = jnp.where(hit, -jnp.inf, cand)
        bs = _stack_rows(best_s)
        ex = jnp.exp(bs - bs[0:1])
        g_scr[hh * kk:(hh + 1) * kk, :] = ex / jnp.sum(ex, axis=0, keepdims=True)
        e_scr[hh * kk:(hh + 1) * kk, :] = _stack_rows(best_e)
    e = e_scr[...]
    hb = e >> (PEER_HALF_EXPERTS.bit_length() - 1)
    row_ref[...] = ((e & (PEER_HALF_EXPERTS - 1)) * SUBLANES).T
    hbit_ref[...] = hb.astype(F32).T
    g_ref[...] = g_scr[...].T


def _peer_topk(h2, w_query, sub_keys):
    n, d = h2.shape
    wq = w_query.astype(BF16)
    keys = sub_keys.reshape(2, SUBLANES, N_KEYS // SUBLANES, -1).swapaxes(1, 2).reshape(sub_keys.shape)
    khi = keys.astype(BF16)
    klo = (keys - khi.astype(F32)).astype(BF16)
    rows = pl.BlockSpec((TOPK_BLOCK, PEER_PICKS), lambda i: (i, 0))
    return pl.pallas_call(
        _peer_topk_kernel,
        grid=(n // TOPK_BLOCK,),
        in_specs=[pl.BlockSpec((TOPK_BLOCK, d), lambda i: (i, 0)), _full(wq.shape), _full(khi.shape), _full(klo.shape)],
        out_specs=[rows, rows, rows],
        out_shape=[jax.ShapeDtypeStruct((n, PEER_PICKS), I32),
                   jax.ShapeDtypeStruct((n, PEER_PICKS), F32), jax.ShapeDtypeStruct((n, PEER_PICKS), F32)],
        scratch_shapes=[pltpu.VMEM((PEER_PICKS, TOPK_BLOCK), I32), pltpu.VMEM((PEER_PICKS, TOPK_BLOCK), F32)],
        compiler_params=_params("arbitrary"),
        name="peer_topk",
    )(h2, wq, khi, klo)


def _pack_table(tab):
    e, d = tab.shape
    bits = lax.bitcast_convert_type(tab.astype(BF16), jnp.uint16).astype(U32)
    packed = bits[:e // 2] | (bits[e // 2:] << 16)
    return packed.reshape(e // 2 * (d // LANES), LANES)


def _table_tile(tab_ref, row8):
    return pltpu.bitcast(tab_ref[pl.ds(pl.multiple_of(row8, SUBLANES), SUBLANES), :], BF16)


def _peer_u_consts():
    k = jnp.arange(2 * LANES)
    col = jnp.arange(2 * PEER_PICKS)
    row = jnp.arange(PEER_PICKS)
    fold = ((k[None, :] // 16 == jnp.arange(2 * U_TILE_PICKS)[:, None] // 2)
            & (k[None, :] % 2 == jnp.arange(2 * U_TILE_PICKS)[:, None] % 2)).astype(BF16)
    q = jnp.arange(LANES)
    r = jnp.arange(SUBLANES)[:, None, None]
    rowq = 8 * ((q % 8) // 2)[None, :, None] + r
    colq = 2 * (32 * (q // 8)[None, :, None] + 2 * (rowq // 2) + (q % 2)[None, :, None]) + rowq % 2
    place = ((col[None, None, :] == colq) & (q < 32)[None, :, None]).astype(BF16)
    dup = ((col[None, :] // 2) == jnp.arange(PEER_PICKS)[:, None]).astype(BF16)
    return fold, place, dup


def _index_copy(row_hbm, idx_smem, sem, batch, slot):
    return pltpu.make_async_copy(row_hbm.at[pl.ds(batch * PEER_UNROLL, PEER_UNROLL)], idx_smem.at[slot], sem.at[slot])


def _for_each_token_batch(row_hbm, idx_smem, sem, block_tokens, body):
    step = pl.program_id(0)
    per_step = block_tokens // PEER_UNROLL
    assert per_step % 2 == 0
    total = pl.num_programs(0) * per_step

    @pl.when(step == 0)
    def _():
        _index_copy(row_hbm, idx_smem, sem, 0, 0).start()

    def pair(k, carry):
        for slot in range(2):
            local = 2 * k + slot
            batch = step * per_step + local
            _index_copy(row_hbm, idx_smem, sem, batch, slot).wait()

            @pl.when(batch + 1 < total)
            def _():
                _index_copy(row_hbm, idx_smem, sem, batch + 1, 1 - slot).start()

            body(local * PEER_UNROLL, idx_smem.at[slot])
        return carry

    lax.fori_loop(0, per_step // 2, pair, 0)


def _peer_u_kernel(row_hbm, x_ref, g_ref, hbit_ref, tab_ref, fold_ref, place_ref, dup_ref,
                   coefh_ref, sum_scr, idx_smem, sem):
    tb = x_ref.shape[0]
    ntiles = PEER_PICKS // (2 * U_TILE_PICKS)
    lane = lax.broadcasted_iota(I32, (SUBLANES, LANES), 1)
    sub = lax.broadcasted_iota(I32, (SUBLANES, LANES), 0)

    def tokens(t0, idx):
        by_row = None
        for u in range(PEER_UNROLL):
            t = t0 + u
            x = x_ref[t]
            xbits = pltpu.bitcast(x.astype(BF16).astype(F32), U32)
            xw = pltpu.bitcast(xbits | (xbits >> 16), BF16)
            sums = jnp.zeros((SUBLANES, LANES), F32)
            for j in range(ntiles):
                cols = []
                for ab in range(2):
                    prods = []
                    for mm in range(U_TILE_PICKS):
                        c = j * 2 * U_TILE_PICKS + 2 * mm + ab
                        prods.append(_table_tile(tab_ref, idx[u, c]) * xw)
                    cols.append(jnp.concatenate(prods, axis=0))
                res = _dot(fold_ref[...], jnp.concatenate(cols, axis=1))
                for rb in range(2 * U_TILE_PICKS // SUBLANES):
                    for ab in range(2):
                        part = res[rb * SUBLANES:(rb + 1) * SUBLANES, ab * LANES:(ab + 1) * LANES]
                        q = j * 8 + rb * 2 + ab
                        sums = jnp.where(lane == q, jnp.sum(part, axis=1, keepdims=True), sums)
            if u % SUBLANES == 0:
                by_row = [jnp.zeros((SUBLANES, LANES), F32)] * SUBLANES
            by_row = [jnp.where(sub == u % SUBLANES, jnp.broadcast_to(sums[r:r + 1], (SUBLANES, LANES)), by_row[r])
                      for r in range(SUBLANES)]
            if u % SUBLANES == SUBLANES - 1:
                first = pl.multiple_of(t0 + u - (SUBLANES - 1), SUBLANES)
                for r in range(SUBLANES):
                    sum_scr[r, pl.ds(first, SUBLANES), :] = by_row[r]

    _for_each_token_batch(row_hbm, idx_smem, sem, tb, tokens)

    a2 = jnp.zeros((tb, 2 * PEER_PICKS), F32)
    for r in range(SUBLANES):
        shi, slo = _split2(sum_scr[r])
        a2 = a2 + _dot(shi, place_ref[r]) + _dot(slo, place_ref[r])
    ghi, glo = _split2(g_ref[...])
    g2 = _dot(ghi, dup_ref[...]) + _dot(glo, dup_ref[...])
    hbit2 = _dot(hbit_ref[...].astype(BF16), dup_ref[...])
    parity = (lax.broadcasted_iota(I32, a2.shape, 1) % 2).astype(F32)
    coef = g2 * (0.5 * a2 * (1.0 + lax.erf(a2 * math.sqrt(0.5))))
    coefh_ref[...] = jnp.where(hbit2 == parity, coef, 0.0).astype(BF16)


def _peer_v_consts():
    col = jnp.arange(2 * PEER_PICKS * SUBLANES)
    ab, p, r = col // (PEER_PICKS * SUBLANES), (col // 16) % (PEER_PICKS // 2), col % 16
    src = jnp.arange(2 * PEER_PICKS)
    expand = ((src[:, None] // 2 == (2 * p + ab)[None, :]) & (src[:, None] % 2 == (r % 2)[None, :])).astype(BF16)
    kk = jnp.arange(PEER_PICKS * SUBLANES)
    diag = ((kk[None, :] % 16) // 2 == (jnp.arange(16) % SUBLANES)[:, None]).astype(F32)
    return expand, diag


def _peer_v_kernel(row_hbm, coefh_ref, x_ref, tab_ref, expand_ref, diag_ref, g_ref, b_ref, o_ref, lrow_scr, y_scr,
                   idx_smem, sem):
    tb = x_ref.shape[0]
    kdim = PEER_PICKS * SUBLANES
    lrow_scr[...] = _dot(coefh_ref[...], expand_ref[...])
    diag = diag_ref[...]

    def tokens(t0, idx):
        for u in range(PEER_UNROLL):
            t = t0 + u
            even = jnp.broadcast_to(lrow_scr[pl.ds(t, 1), 0:kdim], (SUBLANES, kdim))
            odd = jnp.broadcast_to(lrow_scr[pl.ds(t, 1), kdim:2 * kdim], (SUBLANES, kdim))
            lhs = (jnp.concatenate([even, odd], axis=0) * diag).astype(BF16)
            tiles = []
            for p in range(PEER_PICKS // 2):
                wa = _table_tile(tab_ref, idx[u, 2 * p])
                wb = _table_tile(tab_ref, idx[u, 2 * p + 1])
                tiles.append(jnp.concatenate([wa, wb], axis=1))
            res = _dot(lhs, jnp.concatenate(tiles, axis=0))
            y_scr[t] = res[0:SUBLANES, 0:LANES] + res[SUBLANES:, LANES:]

    _for_each_token_batch(row_hbm, idx_smem, sem, tb, tokens)
    z = DEEPNORM_ALPHA * x_ref[...] + y_scr[...]
    cnt = z.shape[1] * z.shape[2]
    mu = jnp.sum(jnp.sum(z, axis=2, keepdims=True), axis=1, keepdims=True) / cnt
    zc = z - mu
    var = jnp.sum(jnp.sum(zc * zc, axis=2, keepdims=True), axis=1, keepdims=True) / cnt
    o_ref[...] = zc * lax.rsqrt(var + LN_EPS) * g_ref[...] + b_ref[...]


def _peer_ffn(h2, row, hbit, gate, expert_u, expert_v, ln_g, ln_b):
    n, d = h2.shape
    rows8 = d // LANES
    assert rows8 == SUBLANES and PEER_PICKS == LANES and expert_u.shape[0] == 2 * PEER_HALF_EXPERTS
    x3 = h2.reshape(n, rows8, LANES)
    u_pk = _pack_table(expert_u)
    v_pk = _pack_table(expert_v)
    hbm = pl.BlockSpec(memory_space=pl.ANY)
    staging = [pltpu.SMEM((2, PEER_UNROLL, PEER_PICKS), I32), pltpu.SemaphoreType.DMA((2,))]
    picks =pl.BlockSpec((PEER_BLOCK, PEER_PICKS), lambda i: (i, 0))
    picks2 = pl.BlockSpec((PEER_BLOCK, 2 * PEER_PICKS), lambda i: (i, 0))
    tok = pl.BlockSpec((PEER_BLOCK, rows8, LANES), lambda i: (i, 0, 0))
    table = pl.BlockSpec(u_pk.shape, lambda i: (0, 0), pipeline_mode=pl.Buffered(1))
    u_consts = _peer_u_consts()
    coefh = pl.pallas_call(
        _peer_u_kernel,
        grid=(n // PEER_BLOCK,),
        in_specs=[hbm, tok, picks, picks, table] + [_full(c.shape) for c in u_consts],
        out_specs=picks2,
        out_shape=jax.ShapeDtypeStruct((n, 2 * PEER_PICKS), BF16),
        scratch_shapes=[pltpu.VMEM((SUBLANES, PEER_BLOCK, LANES), F32)] + staging,
        compiler_params=_params("arbitrary"),
        name="peer_u",
    )(row, x3, gate, hbit, u_pk, *u_consts)
    v_consts = _peer_v_consts()
    out = pl.pallas_call(
        _peer_v_kernel,
        grid=(n // PEER_BLOCK,),
        in_specs=[hbm, picks2, tok, table] + [_full(c.shape) for c in v_consts]
        + [_full((rows8, LANES)), _full((rows8, LANES))],
        out_specs=tok,
        out_shape=jax.ShapeDtypeStruct((n, rows8, LANES), F32),
        scratch_shapes=[pltpu.VMEM((PEER_BLOCK, 2 * PEER_PICKS * SUBLANES), F32),
                        pltpu.VMEM((PEER_BLOCK, rows8, LANES), F32)] + staging,
        compiler_params=_params("arbitrary"),
        name="peer_v",
    )(row, coefh, x3, v_pk, *v_consts, ln_g.reshape(rows8, LANES), ln_b.reshape(rows8, LANES))
    return out.reshape(n, d)


def kernel(x, mem, ln_in_g, ln_in_b, w_in, conv_w, conv_b, mlstm_i_bias, mlstm_f_bias, mlstm_norm_g, rel_bias, w_out, ln1_g, ln1_b, xattn_w_q, xattn_w_kv, xattn_w_o, ln2_g, ln2_b, peer_w_query, peer_sub_keys, peer_u, peer_v, ln3_g, ln3_b):
    batch, seq, d = x.shape
    n = batch * seq
    assert w_in.shape[0] == DEPTH
    h, qkv, mqk, mv, mo, gates = _ln_inproj(x.reshape(n, d), ln_in_g, ln_in_b, w_in[0])
    for l in range(DEPTH):
        if l > 0:
            raise NotImplementedError("input projection of deeper layers")
        att = _attention(qkv, rel_bias[l], batch, seq)
        hm = _mlstm(mqk, mv, mo, gates, conv_w[l], conv_b[l], mlstm_i_bias[l], mlstm_f_bias[l], mlstm_norm_g[l],
                    batch, seq)
        h = _outproj(att.reshape(n, ATT_WIDTH), hm.reshape(n, MLSTM_WIDTH), h, w_out[l], ln1_g[l], ln1_b[l])
        k2, v2 = _kvproj(mem.reshape(-1, d), xattn_w_kv[l])
        m = mem.shape[1]
        h = _xattn(h.reshape(batch, seq, d), k2.reshape(batch, m, d), v2.reshape(batch, m, d),
                   xattn_w_q[l], xattn_w_o[l], ln2_g[l], ln2_b[l]).reshape(n, d)
        row, hbit, gate = _peer_topk(h, peer_w_query[l], peer_sub_keys[l])
        h = _peer_ffn(h, row, hbit, gate, peer_u[l], peer_v[l], ln3_g[l], ln3_b[l])
    return h.reshape(batch, seq, d)
```

```python
import functools
import math

import jax
import jax.numpy as jnp
from jax import lax
from jax.experimental import pallas as pl
from jax.experimental.pallas import tpu as pltpu

F32, BF16, I32, U32 = jnp.float32, jnp.bfloat16, jnp.int32, jnp.uint32

DEPTH = 1
LN_EPS = 1e-5
DEEPNORM_ALPHA = (2.0 * DEPTH) ** 0.25
CHUNK = 64
ATT_HEADS, ATT_HEAD_DIM, ATT_LEFT_CHUNKS, MAX_REL = 8, 64, 8, 128
ATT_WIDTH = ATT_HEADS * ATT_HEAD_DIM
MLSTM_HEADS, MLSTM_HEAD_DIM, CONV_WIDTH = 4, 128, 4
MLSTM_WIDTH = MLSTM_HEADS * MLSTM_HEAD_DIM
XATT_HEADS = 4
PEER_HEADS, N_KEYS, PEER_TOPK = 8, 128, 16
PEER_PICKS = PEER_HEADS * PEER_TOPK
PEER_HALF_EXPERTS = N_KEYS * N_KEYS // 2
U_TILE_PICKS = 16
PEER_UNROLL = 16

LANES = 128
SUBLANES = 8
VMEM_LIMIT_BYTES = 56 * 1024 * 1024

NEG_BIG = -1e30

ROW_BLOCK = 512
ATT_BLOCK = 256
ATT_KEY_BLOCKS = 3
MLSTM_BLOCK = 256
MLSTM_SEQS = 1
TOPK_BLOCK = 256
PEER_BLOCK = 128


def _params(*semantics):
    return pltpu.CompilerParams(dimension_semantics=semantics, vmem_limit_bytes=VMEM_LIMIT_BYTES)


def _layer_norm(x, g, b):
    mu = jnp.mean(x, -1, keepdims=True)
    xc = x - mu
    var = jnp.mean(xc * xc, -1, keepdims=True)
    return xc * lax.rsqrt(var + LN_EPS) * g + b


def _split2(x):
    hi = x.astype(BF16)
    lo = (x - hi.astype(F32)).astype(BF16)
    return hi, lo


def _split3(x):
    hi = x.astype(BF16)
    r = x - hi.astype(F32)
    mid = r.astype(BF16)
    lo = (r - mid.astype(F32)).astype(BF16)
    return hi, mid, lo


def _dot(a, b):
    return jnp.dot(a, b, preferred_element_type=F32)


def _dot_nt(a, b):
    return lax.dot_general(a, b, (((1,), (1,)), ((), ())), preferred_element_type=F32)


def _dot_tn(a, b):
    return lax.dot_general(a, b, (((0,), (0,)), ((), ())), preferred_element_type=F32)


def _full(shape):
    return pl.BlockSpec(shape, lambda *_: (0,) * len(shape))


def _ln_inproj_kernel(x_ref, g_ref, b_ref, wqkv_ref, wmqk_ref, wmv_ref, wmo_ref, wghi_ref, wglo_ref,
                      h_ref, qkv_ref, mqk_ref, mv_ref, mo_ref, gate_ref):
    h = _layer_norm(x_ref[...], g_ref[...], b_ref[...])
    h_ref[...] = h
    hb, hlo = _split2(h)
    qkv_ref[...] = _dot(hb, wqkv_ref[...]).astype(BF16)
    mqk_ref[...] = _dot(hb, wmqk_ref[...]).astype(BF16)
    mv_ref[...] = _dot(hb, wmv_ref[...]).astype(BF16)
    mo_ref[...] = _dot(hb, wmo_ref[...]).astype(BF16)
    gate_ref[...] = _dot(hb, wghi_ref[...]) + _dot(hlo, wghi_ref[...]) + _dot(hb, wglo_ref[...])


def _ln_inproj(x2, g, b, w_in):
    n, d = x2.shape
    a3 = 3 * ATT_WIDTH
    wqkv = w_in[:, :a3].astype(BF16)
    wmqk = w_in[:, a3:a3 + 2 * MLSTM_WIDTH].astype(BF16)
    wmv = w_in[:, a3 + 2 * MLSTM_WIDTH:a3 + 3 * MLSTM_WIDTH].astype(BF16)
    wmo = w_in[:, a3 + 3 * MLSTM_WIDTH:a3 + 4 * MLSTM_WIDTH].astype(BF16)
    wg = jnp.pad(w_in[:, a3 + 4 * MLSTM_WIDTH:], ((0, 0), (0, LANES - 2 * MLSTM_HEADS)))
    wghi = wg.astype(BF16)
    wglo = (wg - wghi.astype(F32)).astype(BF16)
    rows = lambda w: pl.BlockSpec((ROW_BLOCK, w), lambda i: (i, 0))
    return pl.pallas_call(
        _ln_inproj_kernel,
        grid=(n // ROW_BLOCK,),
        in_specs=[rows(d), _full((1, d)), _full((1, d)), _full(wqkv.shape), _full(wmqk.shape),
                  _full(wmv.shape), _full(wmo.shape), _full(wghi.shape), _full(wglo.shape)],
        out_specs=[rows(d), rows(a3), rows(2 * MLSTM_WIDTH), rows(MLSTM_WIDTH), rows(MLSTM_WIDTH), rows(LANES)],
        out_shape=[jax.ShapeDtypeStruct((n, d), F32), jax.ShapeDtypeStruct((n, a3), BF16),
                   jax.ShapeDtypeStruct((n, 2 * MLSTM_WIDTH), BF16), jax.ShapeDtypeStruct((n, MLSTM_WIDTH), BF16),
                   jax.ShapeDtypeStruct((n, MLSTM_WIDTH), BF16), jax.ShapeDtypeStruct((n, LANES), F32)],
        compiler_params=_params("arbitrary"),
        name="ln_inproj",
    )(x2, g.reshape(1, d), b.reshape(1, d), wqkv, wmqk, wmv, wmo, wghi, wglo)


def _attn_kernel(q_ref, k0_ref, k1_ref, k2_ref, v0_ref, v1_ref, v2_ref, bias_ref, o_ref):
    i = pl.program_id(1)
    nkeys = ATT_KEY_BLOCKS * ATT_BLOCK
    q = q_ref[0] * (ATT_HEAD_DIM ** -0.5)
    kcat = jnp.concatenate([k0_ref[0], k1_ref[0], k2_ref[0]], axis=0)
    vcat = jnp.concatenate([v0_ref[0], v1_ref[0], v2_ref[0]], axis=0)
    col = lax.broadcasted_iota(I32, (1, nkeys), 1)
    in_seq = col >= (ATT_KEY_BLOCKS - 1 - i) * ATT_BLOCK
    lane = lax.broadcasted_iota(I32, (1, LANES), 1)
    low = lane < ATT_HEAD_DIM
    outs = []
    for pair in range(ATT_HEADS // 2):
        sl = slice(pair * LANES, (pair + 1) * LANES)
        qp, kp, vp = q[:, sl], kcat[:, sl], vcat[:, sl]
        halves = []
        for half in range(2):
            keep = low if half == 0 else jnp.logical_not(low)
            qh = jnp.where(keep, qp, jnp.zeros_like(qp))
            s = _dot_nt(qh, kp) + bias_ref[2 * pair + half]
            s = jnp.where(in_seq, s, NEG_BIG)
            m = jnp.max(s, -1, keepdims=True)
            p = jnp.exp(s - m)
            l = jnp.sum(p, -1, keepdims=True)
            halves.append(_dot(p.astype(BF16), vp) / l)
        outs.append(jnp.where(low, halves[0], halves[1]))
    o_ref[0] = jnp.concatenate(outs, axis=-1).astype(BF16)


def _attn_bias(rel_bias):
    nq, nk = ATT_BLOCK, ATT_KEY_BLOCKS * ATT_BLOCK
    r = jnp.arange(nq)[:, None]
    j = jnp.arange(nk)[None, :]
    allowed = (j // CHUNK >= r // CHUNK) & (j // CHUNK <= r // CHUNK + ATT_LEFT_CHUNKS)
    period = nq + nk - 1
    i = jnp.arange(period)
    d = jnp.where(i < nk, i, i - period)
    rel = d - (ATT_KEY_BLOCKS - 1) * ATT_BLOCK
    line = rel_bias[:, jnp.clip(rel, -MAX_REL, MAX_REL) + MAX_REL].astype(F32)
    rep = jnp.tile(line, (1, nq + 1))[:, :nq * (period - 1)]
    tab = rep.reshape(-1, nq, period - 1)[:, :, :nk]
    return jnp.where(allowed[None], tab, NEG_BIG)


def _attention(qkv, rel_bias, batch, seq):
    assert ATT_LEFT_CHUNKS * CHUNK == (ATT_KEY_BLOCKS - 1) * ATT_BLOCK
    qkv3 = qkv.reshape(batch, seq, 3 * ATT_WIDTH)
    bias = _attn_bias(rel_bias)
    blk = (1, ATT_BLOCK, ATT_WIDTH)

    def kv_spec(col, j):
        return pl.BlockSpec(blk, lambda b, i: (b, jnp.maximum(i - (ATT_KEY_BLOCKS - 1) + j, 0), col))

    return pl.pallas_call(
        _attn_kernel,
        grid=(batch, seq // ATT_BLOCK),
        in_specs=[pl.BlockSpec(blk, lambda b, i: (b, i, 0))]
        + [kv_spec(1, j) for j in range(ATT_KEY_BLOCKS)] + [kv_spec(2, j) for j in range(ATT_KEY_BLOCKS)]
        + [_full(bias.shape)],
        out_specs=pl.BlockSpec(blk, lambda b, i: (b, i, 0)),
        out_shape=jax.ShapeDtypeStruct((batch, seq, ATT_WIDTH), BF16),
        compiler_params=_params("arbitrary", "arbitrary"),
        name="attn",
    )(qkv3, qkv3, qkv3, qkv3, qkv3, qkv3, qkv3, bias)


def _mlstm_kernel(mqk_ref, mv_ref, mo_ref, gate_ref, convw_ref, convb_ref, gbias_ref, ng_ref, out_ref,
                  xpad, q_scr, k_scr, gate_scr, logf_scr, c_st, n_st, m_st):
    j = pl.program_id(1)
    rows = MLSTM_BLOCK
    d = MLSTM_HEAD_DIM

    @pl.when(j == 0)
    def _():
        xpad[:, 0:SUBLANES, :] = jnp.zeros((MLSTM_SEQS, SUBLANES, 2 * MLSTM_WIDTH), F32)
        c_st[...] = jnp.zeros_like(c_st)
        n_st[...] = jnp.zeros_like(n_st)
        m_st[...] = jnp.zeros_like(m_st)

    for bb in range(MLSTM_SEQS):
        xpad[bb, SUBLANES:SUBLANES + rows, :] = mqk_ref[bb].astype(F32)
        acc = jnp.broadcast_to(convb_ref[...], (rows, 2 * MLSTM_WIDTH))
        for t in range(CONV_WIDTH):
            acc = acc + convw_ref[t:t + 1, :] * xpad[bb, pl.ds(SUBLANES - (CONV_WIDTH - 1) + t, rows), :]
        xpad[bb, 0:SUBLANES, :] = xpad[bb, rows:rows + SUBLANES, :]
        qk = acc * jax.nn.sigmoid(acc)
        q_scr[bb] = qk[:, :MLSTM_WIDTH].astype(BF16)
        k_scr[bb] = qk[:, MLSTM_WIDTH:] * (d ** -0.5)

        gates = gate_ref[bb] + gbias_ref[...]
        gate_scr[bb] = gates
        logf_scr[bb] = jax.nn.log_sigmoid(gates)

    ri = lax.broadcasted_iota(I32, (CHUNK, CHUNK), 0)
    ci = lax.broadcasted_iota(I32, (CHUNK, CHUNK), 1)
    causal = ci <= ri
    tri = causal.astype(BF16)

    def chunk_body(c, carry):
        for bb in range(MLSTM_SEQS):
            _mlstm_chunk(bb, pl.multiple_of(c * CHUNK, CHUNK), causal, tri, mv_ref, mo_ref, ng_ref, out_ref,
                         q_scr, k_scr, gate_scr, logf_scr, c_st, n_st, m_st)
        return carry

    lax.fori_loop(0, rows // CHUNK, chunk_body, 0)


def _mlstm_chunk(bb, r0, causal, tri, mv_ref, mo_ref, ng_ref, out_ref, q_scr, k_scr, gate_scr, logf_scr,
                 c_st, n_st, m_st):
    d = MLSTM_HEAD_DIM
    g = gate_scr[bb, pl.ds(r0, CHUNK), :]
    lf = logf_scr[bb, pl.ds(r0, CHUNK), :]
    b3 = _dot(tri, jnp.concatenate(_split3(lf), axis=1))
    bcol = b3[:, :LANES] + b3[:, LANES:2 * LANES] + b3[:, 2 * LANES:]
    g_t = g.T
    b_t = bcol.T
    m_all = m_st[bb]
    n_all = n_st[bb]
    for h in range(MLSTM_HEADS):
        hs = slice(h * d, (h + 1) * d)
        ic = g[:, h:h + 1]
        bc = bcol[:, MLSTM_HEADS + h:MLSTM_HEADS + h + 1]
        ir = g_t[h:h + 1, :]
        br = b_t[MLSTM_HEADS + h:MLSTM_HEADS + h + 1, :]
        m_prev = m_all[h:h + 1, 0:1]
        n_prev = n_all[h:h + 1, :]
        c_prev = c_st[bb, h]
        qh = q_scr[bb, pl.ds(r0, CHUNK), hs]
        kh = k_scr[bb, pl.ds(r0, CHUNK), hs]
        vh = mv_ref[bb, pl.ds(r0, CHUNK), hs]

        log_d = jnp.where(causal, bc - br + ir, NEG_BIG)
        inter = bc + m_prev
        m_t = jnp.maximum(inter, jnp.max(log_d, -1, keepdims=True))
        d_mat = jnp.exp(log_d - m_t)
        w_inter = jnp.exp(inter - m_t)
        qk_d = _dot_nt(qh, kh.astype(BF16)) * d_mat
        lhs = jnp.concatenate([(w_inter * qh.astype(F32)).astype(BF16), qk_d.astype(BF16)], axis=1)
        num = _dot(lhs, jnp.concatenate([c_prev.astype(BF16), vh], axis=0))
        den = (w_inter * jnp.sum(qh.astype(F32) * n_prev, -1, keepdims=True)
               + jnp.sum(qk_d, -1, keepdims=True))
        hh = num / jnp.maximum(jnp.abs(den), jnp.exp(-m_t))

        b_last = bc[CHUNK - 1:CHUNK, :]
        log_in = b_last - bc + ic
        m_new = jnp.maximum(b_last + m_prev, jnp.max(log_in, 0, keepdims=True))
        w_prev = jnp.exp(b_last + m_prev - m_new)
        kw = kh * jnp.exp(log_in - m_new)
        c_st[bb, h] = w_prev * c_prev + _dot_tn(kw.astype(BF16), vh)
        n_st[bb, h:h + 1, :] = w_prev * n_prev + jnp.sum(kw, 0, keepdims=True)
        m_st[bb, h:h + 1, :] = jnp.broadcast_to(m_new, (1, LANES))

        mu = jnp.mean(hh, -1, keepdims=True)
        hc = hh - mu
        var = jnp.mean(hc * hc, -1, keepdims=True)
        hn = hc * lax.rsqrt(var + LN_EPS) * ng_ref[:, hs]
        og = jax.nn.sigmoid(mo_ref[bb, pl.ds(r0, CHUNK), hs].astype(F32))
        out_ref[bb, pl.ds(r0, CHUNK), hs] = (og * hn).astype(BF16)


def _mlstm(mqk, mv, mo, gates, conv_w, conv_b, i_bias, f_bias, norm_g, batch, seq):
    w2 = 2 * MLSTM_WIDTH
    gbias = jnp.pad(jnp.concatenate([i_bias, f_bias]).astype(F32), (0, LANES - 2 * MLSTM_HEADS)).reshape(1, LANES)
    ns = MLSTM_SEQS
    assert batch % ns == 0
    blk = lambda w: pl.BlockSpec((ns, MLSTM_BLOCK, w), lambda b, i: (b, i, 0))
    return pl.pallas_call(
        _mlstm_kernel,
        grid=(batch // ns, seq // MLSTM_BLOCK),
        in_specs=[blk(w2), blk(MLSTM_WIDTH), blk(MLSTM_WIDTH), blk(LANES),
                  _full((CONV_WIDTH, w2)), _full((1, w2)), _full((1, LANES)), _full((1, MLSTM_WIDTH))],
        out_specs=blk(MLSTM_WIDTH),
        out_shape=jax.ShapeDtypeStruct((batch, seq, MLSTM_WIDTH), BF16),
        scratch_shapes=[pltpu.VMEM((ns, MLSTM_BLOCK + SUBLANES, w2), F32),
                        pltpu.VMEM((ns, MLSTM_BLOCK, MLSTM_WIDTH), BF16),
                        pltpu.VMEM((ns, MLSTM_BLOCK, MLSTM_WIDTH), F32),
                        pltpu.VMEM((ns, MLSTM_BLOCK, LANES), F32),
                        pltpu.VMEM((ns, MLSTM_BLOCK, LANES), F32),
                        pltpu.VMEM((ns, MLSTM_HEADS, MLSTM_HEAD_DIM, MLSTM_HEAD_DIM), F32),
                        pltpu.VMEM((ns, SUBLANES, MLSTM_HEAD_DIM), F32),
                        pltpu.VMEM((ns, SUBLANES, LANES), F32)],
        compiler_params=_params("arbitrary", "arbitrary"),
        name="mlstm",
    )(mqk.reshape(batch, seq, w2), mv.reshape(batch, seq, MLSTM_WIDTH), mo.reshape(batch, seq, MLSTM_WIDTH),
      gates.reshape(batch, seq, LANES), conv_w.astype(F32), conv_b.reshape(1, w2).astype(F32), gbias,
      norm_g.reshape(1, MLSTM_WIDTH).astype(F32))


def _outproj_kernel(att_ref, hm_ref, h_ref, wa_ref, wm_ref, g_ref, b_ref, o_ref):
    y = _dot(att_ref[...], wa_ref[...]) + _dot(hm_ref[...], wm_ref[...])
    o_ref[...] = _layer_norm(DEEPNORM_ALPHA * h_ref[...] + y, g_ref[...], b_ref[...])


def _outproj(att, hm, h, w_out, g, b):
    n, d = h.shape
    wa = w_out[:ATT_WIDTH].astype(BF16)
    wm = w_out[ATT_WIDTH:].astype(BF16)
    rows = lambda w: pl.BlockSpec((ROW_BLOCK, w), lambda i: (i, 0))
    return pl.pallas_call(
        _outproj_kernel,
        grid=(n // ROW_BLOCK,),
        in_specs=[rows(ATT_WIDTH), rows(MLSTM_WIDTH), rows(d), _full(wa.shape), _full(wm.shape),
                  _full((1, d)), _full((1, d))],
        out_specs=rows(d),
        out_shape=jax.ShapeDtypeStruct((n, d), F32),
        compiler_params=_params("arbitrary"),
        name="outproj",
    )(att, hm, h, wa, wm, g.reshape(1, d), b.reshape(1, d))


def _kvproj_kernel(mem_ref, w_ref, k_ref, v_ref):
    kv = _dot(mem_ref[...].astype(BF16), w_ref[...])
    d = k_ref.shape[-1]
    k_ref[...] = kv[:, :d].astype(BF16)
    v_ref[...] = kv[:, d:].astype(BF16)


def _kvproj(mem2, w_kv):
    n, d = mem2.shape
    w = w_kv.astype(BF16)
    blk = min(ROW_BLOCK, n)
    rows = pl.BlockSpec((blk, d), lambda i: (i, 0))
    return pl.pallas_call(
        _kvproj_kernel,
        grid=(n // blk,),
        in_specs=[rows, _full(w.shape)],
        out_specs=[rows, rows],
        out_shape=[jax.ShapeDtypeStruct((n, d), BF16)] * 2,
        compiler_params=_params("arbitrary"),
        name="kvproj",
    )(mem2, w)


def _xattn_kernel(h_ref, k_ref, v_ref, wq_ref, wo_ref, g_ref, b_ref, o_ref):
    h = h_ref[0]
    d = h.shape[-1]
    dh = d // XATT_HEADS
    q = (_dot(h.astype(BF16), wq_ref[...]) * (dh ** -0.5)).astype(BF16)
    outs = []
    for hd in range(XATT_HEADS):
        sl = slice(hd * dh, (hd + 1) * dh)
        s = _dot_nt(q[:, sl], k_ref[0, :, sl])
        m = jnp.max(s, -1, keepdims=True)
        p = jnp.exp(s - m)
        l = jnp.sum(p, -1, keepdims=True)
        outs.append((_dot(p.astype(BF16), v_ref[0, :, sl]) / l).astype(BF16))
    y = _dot(jnp.concatenate(outs, axis=-1), wo_ref[...])
    o_ref[0] = _layer_norm(DEEPNORM_ALPHA * h + y, g_ref[...], b_ref[...])


def _xattn(h3, k3, v3, w_q, w_o, g, b):
    batch, seq, d = h3.shape
    m = k3.shape[1]
    wq = w_q.astype(BF16)
    wo = w_o.astype(BF16)
    blk = pl.BlockSpec((1, ROW_BLOCK, d), lambda bb, i: (bb, i, 0))
    mem = pl.BlockSpec((1, m, d), lambda bb, i: (bb, 0, 0))
    return pl.pallas_call(
        _xattn_kernel,
        grid=(batch, seq // ROW_BLOCK),
        in_specs=[blk, mem, mem, _full(wq.shape), _full(wo.shape), _full((1, d)), _full((1, d))],
        out_specs=blk,
        out_shape=jax.ShapeDtypeStruct((batch, seq, d), F32),
        compiler_params=_params("arbitrary", "arbitrary"),
        name="xattn",
    )(h3, k3, v3, wq, wo, g.reshape(1, d), b.reshape(1, d))


def _topk_rows(s, k):
    groups = s.shape[0] // SUBLANES
    t = s.shape[1]
    sg = [s[g * SUBLANES:(g + 1) * SUBLANES] for g in range(groups)]
    sub = lax.broadcasted_iota(I32, (SUBLANES, t), 0)
    vals, ids = [], []
    for _ in range(k):
        m8, a8 = sg[0], jnp.zeros((SUBLANES, t), I32)
        for g in range(1, groups):
            upd = sg[g] > m8
            m8 = jnp.where(upd, sg[g], m8)
            a8 = jnp.where(upd, g, a8)
        m = jnp.max(m8, axis=0, keepdims=True)
        kid = jnp.min(jnp.where(m8 == m, sub * groups + a8, SUBLANES * groups), axis=0, keepdims=True)
        vals.append(m)
        ids.append(kid)
        gone = jnp.where(sub * groups + a8 == kid, a8, -1)
        sg = [jnp.where(gone == g, -jnp.inf, sg[g]) for g in range(groups)]
    return vals, ids


def _stack_rows(rows_list):
    k = len(rows_list)
    t = rows_list[0].shape[-1]
    iota = lax.broadcasted_iota(I32, (k, t), 0)
    out = jnp.broadcast_to(rows_list[0], (k, t))
    for r in range(1, k):
        out = jnp.where(iota == r, jnp.broadcast_to(rows_list[r], (k, t)), out)
    return out


def _peer_topk_kernel(h_ref, wq_ref, khi_ref, klo_ref, row_ref, hbit_ref, g_ref, e_scr, g_scr):
    t = h_ref.shape[0]
    kk = PEER_TOPK
    q = _dot(h_ref[...].astype(BF16), wq_ref[...])
    half8 = kk // 2
    sub = lax.broadcasted_iota(I32, (half8, t), 0)
    for hh in range(PEER_HEADS):
        tops = []
        for p in range(2):
            c0 = (hh * 2 + p) * N_KEYS
            qhi, qlo = _split2(q[:, c0:c0 + N_KEYS])
            s = _dot_nt(khi_ref[p], qhi) + _dot_nt(khi_ref[p], qlo) + _dot_nt(klo_ref[p], qhi)
            vals, idxs = _topk_rows(s, kk)
            tops.append((_stack_rows(vals), _stack_rows(idxs)))
        (s0, i0), (s1, i1) = tops
        e0 = i0 * N_KEYS
        cand, cexp, cflat = [], [], []
        for b in range(half8):
            cand.append(s0[:half8] + s1[b:b + 1])
            cexp.append(e0[:half8] + i1[b:b + 1])
            cflat.append(sub * kk + b)
        cand.append(s0[half8:] + s1[0:1])
        cexp.append(e0[half8:] + i1[0:1])
        cflat.append((sub + half8) * kk)
        cand.append(s0[0:1] + s1[half8:])
        cexp.append(e0[0:1] + i1[half8:])
        cflat.append(sub + half8)
        cand = jnp.concatenate(cand, axis=0)
        cexp = jnp.concatenate(cexp, axis=0)
        cflat = jnp.concatenate(cflat, axis=0)
        best_s, best_e = [], []
        for _ in range(kk):
            m = jnp.max(cand, axis=0, keepdims=True)
            jsel = jnp.min(jnp.where(cand == m, cflat, kk * kk), axis=0, keepdims=True)
            hit = cflat == jsel
            best_s.append(m)
            best_e.append(jnp.max(jnp.where(hit, cexp, 0), axis=0, keepdims=True))
            cand = jnp.where(hit, -jnp.inf, cand)
        bs = _stack_rows(best_s)
        ex = jnp.exp(bs - bs[0:1])
        g_scr[hh * kk:(hh + 1) * kk, :] = ex / jnp.sum(ex, axis=0, keepdims=True)
        e_scr[hh * kk:(hh + 1) * kk, :] = _stack_rows(best_e)
    e = e_scr[...]
    hb = e >> (PEER_HALF_EXPERTS.bit_length() - 1)
    row_ref[...] = ((e & (PEER_HALF_EXPERTS - 1)) * SUBLANES).T
    hbit_ref[...] = hb.astype(F32).T
    g_ref[...] = g_scr[...].T


def _peer_topk(h2, w_query, sub_keys):
    n, d = h2.shape
    wq = w_query.astype(BF16)
    keys = sub_keys.reshape(2, SUBLANES, N_KEYS // SUBLANES, -1).swapaxes(1, 2).reshape(sub_keys.shape)
    khi = keys.astype(BF16)
    klo = (keys - khi.astype(F32)).astype(BF16)
    rows = pl.BlockSpec((TOPK_BLOCK, PEER_PICKS), lambda i: (i, 0))
    return pl.pallas_call(
        _peer_topk_kernel,
        grid=(n // TOPK_BLOCK,),
        in_specs=[pl.BlockSpec((TOPK_BLOCK, d), lambda i: (i, 0)), _full(wq.shape), _full(khi.shape), _full(klo.shape)],
        out_specs=[rows, rows, rows],
        out_shape=[jax.ShapeDtypeStruct((n, PEER_PICKS), I32),
                   jax.ShapeDtypeStruct((n, PEER_PICKS), F32), jax.ShapeDtypeStruct((n, PEER_PICKS), F32)],
        scratch_shapes=[pltpu.VMEM((PEER_PICKS, TOPK_BLOCK), I32), pltpu.VMEM((PEER_PICKS, TOPK_BLOCK), F32)],
        compiler_params=_params("arbitrary"),
        name="peer_topk",
    )(h2, wq, khi, klo)


def _pack_table(tab):
    e, d = tab.shape
    bits = lax.bitcast_convert_type(tab.astype(BF16), jnp.uint16).astype(U32)
    packed = bits[:e // 2] | (bits[e // 2:] << 16)
    return packed.reshape(e // 2 * (d // LANES), LANES)


def _table_tile(tab_ref, row8):
    return pltpu.bitcast(tab_ref[pl.ds(pl.multiple_of(row8, SUBLANES), SUBLANES), :], BF16)


def _peer_u_consts():
    k = jnp.arange(2 * LANES)
    col = jnp.arange(2 * PEER_PICKS)
    row = jnp.arange(PEER_PICKS)
    fold = ((k[None, :] // 16 == jnp.arange(2 * U_TILE_PICKS)[:, None] // 2)
            & (k[None, :] % 2 == jnp.arange(2 * U_TILE_PICKS)[:, None] % 2)).astype(BF16)
    q = jnp.arange(LANES)
    r = jnp.arange(SUBLANES)[:, None, None]
    rowq = 8 * ((q % 8) // 2)[None, :, None] + r
    colq = 2 * (32 * (q // 8)[None, :, None] + 2 * (rowq // 2) + (q % 2)[None, :, None]) + rowq % 2
    place = ((col[None, None, :] == colq) & (q < 32)[None, :, None]).astype(BF16)
    dup = ((col[None, :] // 2) == jnp.arange(PEER_PICKS)[:, None]).astype(BF16)
    return fold, place, dup


def _index_copy(row_hbm, idx_smem, sem, batch, slot):
    return pltpu.make_async_copy(row_hbm.at[pl.ds(batch * PEER_UNROLL, PEER_UNROLL)], idx_smem.at[slot], sem.at[slot])


def _for_each_token_batch(row_hbm, idx_smem, sem, block_tokens, body):
    step = pl.program_id(0)
    per_step = block_tokens // PEER_UNROLL
    assert per_step % 2 == 0
    total = pl.num_programs(0) * per_step

    @pl.when(step == 0)
    def _():
        _index_copy(row_hbm, idx_smem, sem, 0, 0).start()

    def pair(k, carry):
        for slot in range(2):
            local = 2 * k + slot
            batch = step * per_step + local
            _index_copy(row_hbm, idx_smem, sem, batch, slot).wait()

            @pl.when(batch + 1 < total)
            def _():
                _index_copy(row_hbm, idx_smem, sem, batch + 1, 1 - slot).start()

            body(local * PEER_UNROLL, idx_smem.at[slot])
        return carry

    lax.fori_loop(0, per_step // 2, pair, 0)


def _peer_u_kernel(row_hbm, x_ref, g_ref, hbit_ref, tab_ref, fold_ref, place_ref, dup_ref,
                   coefh_ref, sum_scr, idx_smem, sem):
    tb = x_ref.shape[0]
    ntiles = PEER_PICKS // (2 * U_TILE_PICKS)
    lane = lax.broadcasted_iota(I32, (SUBLANES, LANES), 1)
    sub = lax.broadcasted_iota(I32, (SUBLANES, LANES), 0)

    def tokens(t0, idx):
        by_row = None
        for u in range(PEER_UNROLL):
            t = t0 + u
            x = x_ref[t]
            xbits = pltpu.bitcast(x.astype(BF16).astype(F32), U32)
            xw = pltpu.bitcast(xbits | (xbits >> 16), BF16)
            sums = jnp.zeros((SUBLANES, LANES), F32)
            for j in range(ntiles):
                cols = []
                for ab in range(2):
                    prods = []
                    for mm in range(U_TILE_PICKS):
                        c = j * 2 * U_TILE_PICKS + 2 * mm + ab
                        prods.append(_table_tile(tab_ref, idx[u, c]) * xw)
                    cols.append(jnp.concatenate(prods, axis=0))
                res = _dot(fold_ref[...], jnp.concatenate(cols, axis=1))
                for rb in range(2 * U_TILE_PICKS // SUBLANES):
                    for ab in range(2):
                        part = res[rb * SUBLANES:(rb + 1) * SUBLANES, ab * LANES:(ab + 1) * LANES]
                        q = j * 8 + rb * 2 + ab
                        sums = jnp.where(lane == q, jnp.sum(part, axis=1, keepdims=True), sums)
            if u % SUBLANES == 0:
                by_row = [jnp.zeros((SUBLANES, LANES), F32)] * SUBLANES
            by_row = [jnp.where(sub == u % SUBLANES, jnp.broadcast_to(sums[r:r + 1], (SUBLANES, LANES)), by_row[r])
                      for r in range(SUBLANES)]
            if u % SUBLANES == SUBLANES - 1:
                first = pl.multiple_of(t0 + u - (SUBLANES - 1), SUBLANES)
                for r in range(SUBLANES):
                    sum_scr[r, pl.ds(first, SUBLANES), :] = by_row[r]

    _for_each_token_batch(row_hbm, idx_smem, sem, tb, tokens)

    a2 = jnp.zeros((tb, 2 * PEER_PICKS), F32)
    for r in range(SUBLANES):
        shi, slo = _split2(sum_scr[r])
        a2 = a2 + _dot(shi, place_ref[r]) + _dot(slo, place_ref[r])
    ghi, glo = _split2(g_ref[...])
    g2 = _dot(ghi, dup_ref[...]) + _dot(glo, dup_ref[...])
    hbit2 = _dot(hbit_ref[...].astype(BF16), dup_ref[...])
    parity = (lax.broadcasted_iota(I32, a2.shape, 1) % 2).astype(F32)
    coef = g2 * (0.5 * a2 * (1.0 + lax.erf(a2 * math.sqrt(0.5))))
    coefh_ref[...] = jnp.where(hbit2 == parity, coef, 0.0).astype(BF16)


def _peer_v_consts():
    col = jnp.arange(2 * PEER_PICKS * SUBLANES)
    ab, p, r = col // (PEER_PICKS * SUBLANES), (col // 16) % (PEER_PICKS // 2), col % 16
    src = jnp.arange(2 * PEER_PICKS)
    expand = ((src[:, None] // 2 == (2 * p + ab)[None, :]) & (src[:, None] % 2 == (r % 2)[None, :])).astype(BF16)
    kk = jnp.arange(PEER_PICKS * SUBLANES)
    diag = ((kk[None, :] % 16) // 2 == (jnp.arange(16) % SUBLANES)[:, None]).astype(F32)
    return expand, diag


def _peer_v_kernel(row_hbm, coefh_ref, x_ref, tab_ref, expand_ref, diag_ref, g_ref, b_ref, o_ref, lrow_scr, y_scr,
                   idx_smem, sem):
    tb = x_ref.shape[0]
    kdim = PEER_PICKS * SUBLANES
    lrow_scr[...] = _dot(coefh_ref[...], expand_ref[...])
    diag = diag_ref[...]
    sub = lax.broadcasted_iota(I32, (SUBLANES, LANES), 0)

    def tokens(t0, idx):
        by_row = None
        for u in range(PEER_UNROLL):
            t = t0 + u
            even = jnp.broadcast_to(lrow_scr[pl.ds(t, 1), 0:kdim], (SUBLANES, kdim))
            odd = jnp.broadcast_to(lrow_scr[pl.ds(t, 1), kdim:2 * kdim], (SUBLANES, kdim))
            lhs = (jnp.concatenate([even, odd], axis=0) * diag).astype(BF16)
            tiles = []
            for p in range(PEER_PICKS // 2):
                wa = _table_tile(tab_ref, idx[u, 2 * p])
                wb = _table_tile(tab_ref, idx[u, 2 * p + 1])
                tiles.append(jnp.concatenate([wa, wb], axis=1))
            res = _dot(lhs, jnp.concatenate(tiles, axis=0))
            y = res[0:SUBLANES, 0:LANES] + res[SUBLANES:, LANES:]
            if u % SUBLANES == 0:
                by_row = [jnp.zeros((SUBLANES, LANES), F32)] * SUBLANES
            by_row = [jnp.where(sub == u % SUBLANES, jnp.broadcast_to(y[r:r + 1], (SUBLANES, LANES)), by_row[r])
                      for r in range(SUBLANES)]
            if u % SUBLANES == SUBLANES - 1:
                first = pl.multiple_of(t0 + u - (SUBLANES - 1), SUBLANES)
                for r in range(SUBLANES):
                    y_scr[r, pl.ds(first, SUBLANES), :] = by_row[r]

    _for_each_token_batch(row_hbm, idx_smem, sem, tb, tokens)
    z = [DEEPNORM_ALPHA * x_ref[:, r * LANES:(r + 1) * LANES] + y_scr[r] for r in range(SUBLANES)]
    cnt = SUBLANES * LANES
    mu = sum(jnp.sum(zr, axis=1, keepdims=True) for zr in z) / cnt
    zc = [zr - mu for zr in z]
    var = sum(jnp.sum(c * c, axis=1, keepdims=True) for c in zc) / cnt
    rstd = lax.rsqrt(var + LN_EPS)
    for r in range(SUBLANES):
        o_ref[:, r * LANES:(r + 1) * LANES] = zc[r] * rstd * g_ref[r:r + 1, :] + b_ref[r:r + 1, :]


def _peer_ffn(h2, row, hbit, gate, expert_u, expert_v, ln_g, ln_b):
    n, d = h2.shape
    rows8 = d // LANES
    assert rows8 == SUBLANES and PEER_PICKS == LANES and expert_u.shape[0] == 2 * PEER_HALF_EXPERTS
    x3 = h2.reshape(n, rows8, LANES)
    u_pk = _pack_table(expert_u)
    v_pk = _pack_table(expert_v)
    hbm = pl.BlockSpec(memory_space=pl.ANY)
    staging = [pltpu.SMEM((2, PEER_UNROLL, PEER_PICKS), I32), pltpu.SemaphoreType.DMA((2,))]
    picks =pl.BlockSpec((PEER_BLOCK, PEER_PICKS), lambda i: (i, 0))
    picks2 = pl.BlockSpec((PEER_BLOCK, 2 * PEER_PICKS), lambda i: (i, 0))
    tok = pl.BlockSpec((PEER_BLOCK, rows8, LANES), lambda i: (i, 0, 0))
    flat = pl.BlockSpec((PEER_BLOCK, d), lambda i: (i, 0))
    table = pl.BlockSpec(u_pk.shape, lambda i: (0, 0), pipeline_mode=pl.Buffered(1))
    u_consts = _peer_u_consts()
    coefh = pl.pallas_call(
        _peer_u_kernel,
        grid=(n // PEER_BLOCK,),
        in_specs=[hbm, tok, picks, picks, table] + [_full(c.shape) for c in u_consts],
        out_specs=picks2,
        out_shape=jax.ShapeDtypeStruct((n, 2 * PEER_PICKS), BF16),
        scratch_shapes=[pltpu.VMEM((SUBLANES, PEER_BLOCK, LANES), F32)] + staging,
        compiler_params=_params("arbitrary"),
        name="peer_u",
    )(row, x3, gate, hbit, u_pk, *u_consts)
    v_consts = _peer_v_consts()
    out = pl.pallas_call(
        _peer_v_kernel,
        grid=(n // PEER_BLOCK,),
        in_specs=[hbm, picks2, flat, table] + [_full(c.shape) for c in v_consts]
        + [_full((rows8, LANES)), _full((rows8, LANES))],
        out_specs=flat,
        out_shape=jax.ShapeDtypeStruct((n, d), F32),
        scratch_shapes=[pltpu.VMEM((PEER_BLOCK, 2 * PEER_PICKS * SUBLANES), F32),
                        pltpu.VMEM((rows8, PEER_BLOCK, LANES), F32)] + staging,
        compiler_params=_params("arbitrary"),
        name="peer_v",
    )(row, coefh, h2, v_pk, *v_consts, ln_g.reshape(rows8, LANES), ln_b.reshape(rows8, LANES))
    return out


def kernel(x, mem, ln_in_g, ln_in_b, w_in, conv_w, conv_b, mlstm_i_bias, mlstm_f_bias, mlstm_norm_g, rel_bias, w_out, ln1_g, ln1_b, xattn_w_q, xattn_w_kv, xattn_w_o, ln2_g, ln2_b, peer_w_query, peer_sub_keys, peer_u, peer_v, ln3_g, ln3_b):
    batch, seq, d = x.shape
    n = batch * seq
    assert w_in.shape[0] == DEPTH
    h, qkv, mqk, mv, mo, gates = _ln_inproj(x.reshape(n, d), ln_in_g, ln_in_b, w_in[0])
    for l in range(DEPTH):
        if l > 0:
            raise NotImplementedError("input projection of deeper layers")
        att = _attention(qkv, rel_bias[l], batch, seq)
        hm = _mlstm(mqk, mv, mo, gates, conv_w[l], conv_b[l], mlstm_i_bias[l], mlstm_f_bias[l], mlstm_norm_g[l],
                    batch, seq)
        h = _outproj(att.reshape(n, ATT_WIDTH), hm.reshape(n, MLSTM_WIDTH), h, w_out[l], ln1_g[l], ln1_b[l])
        k2, v2 = _kvproj(mem.reshape(-1, d), xattn_w_kv[l])
        m = mem.shape[1]
        h = _xattn(h.reshape(batch, seq, d), k2.reshape(batch, m, d), v2.reshape(batch, m, d),
                   xattn_w_q[l], xattn_w_o[l], ln2_g[l], ln2_b[l]).reshape(n, d)
        row, hbit, gate = _peer_topk(h, peer_w_query[l], peer_sub_keys[l])
        h = _peer_ffn(h, row, hbit, gate, peer_u[l], peer_v[l], ln3_g[l], ln3_b[l])
    return h.reshape(batch, seq, d)
```

```python
import functools
import math

import jax
import jax.numpy as jnp
from jax import lax
from jax.experimental import pallas as pl
from jax.experimental.pallas import tpu as pltpu

F32, BF16, I32, U32 = jnp.float32, jnp.bfloat16, jnp.int32, jnp.uint32

DEPTH = 1
LN_EPS = 1e-5
DEEPNORM_ALPHA = (2.0 * DEPTH) ** 0.25
CHUNK = 64
ATT_HEADS, ATT_HEAD_DIM, ATT_LEFT_CHUNKS, MAX_REL = 8, 64, 8, 128
ATT_WIDTH = ATT_HEADS * ATT_HEAD_DIM
MLSTM_HEADS, MLSTM_HEAD_DIM, CONV_WIDTH = 4, 128, 4
MLSTM_WIDTH = MLSTM_HEADS * MLSTM_HEAD_DIM
XATT_HEADS = 4
PEER_HEADS, N_KEYS, PEER_TOPK = 8, 128, 16
PEER_PICKS = PEER_HEADS * PEER_TOPK
PEER_HALF_EXPERTS = N_KEYS * N_KEYS // 2
U_TILE_PICKS = 16
PEER_UNROLL = 16

LANES = 128
SUBLANES = 8
VMEM_LIMIT_BYTES = 56 * 1024 * 1024

NEG_BIG = -1e30

ROW_BLOCK = 512
ATT_BLOCK = 256
ATT_KEY_BLOCKS = 3
MLSTM_BLOCK = 256
MLSTM_SEQS = 1
TOPK_BLOCK = 256
PEER_BLOCK = 128
PACK_BLOCK = 256


def _params(*semantics):
    return pltpu.CompilerParams(dimension_semantics=semantics, vmem_limit_bytes=VMEM_LIMIT_BYTES)


def _layer_norm(x, g, b):
    mu = jnp.mean(x, -1, keepdims=True)
    xc = x - mu
    var = jnp.mean(xc * xc, -1, keepdims=True)
    return xc * lax.rsqrt(var + LN_EPS) * g + b


def _split2(x):
    hi = x.astype(BF16)
    lo = (x - hi.astype(F32)).astype(BF16)
    return hi, lo


def _split3(x):
    hi = x.astype(BF16)
    r = x - hi.astype(F32)
    mid = r.astype(BF16)
    lo = (r - mid.astype(F32)).astype(BF16)
    return hi, mid, lo


def _dot(a, b):
    return jnp.dot(a, b, preferred_element_type=F32)


def _dot_nt(a, b):
    return lax.dot_general(a, b, (((1,), (1,)), ((), ())), preferred_element_type=F32)


def _dot_tn(a, b):
    return lax.dot_general(a, b, (((0,), (0,)), ((), ())), preferred_element_type=F32)


def _full(shape):
    return pl.BlockSpec(shape, lambda *_: (0,) * len(shape))


def _ln_inproj_kernel(x_ref, g_ref, b_ref, wqkv_ref, wmqk_ref, wmv_ref, wmo_ref, wghi_ref, wglo_ref,
                      h_ref, qkv_ref, mqk_ref, mv_ref, mo_ref, gate_ref):
    h = _layer_norm(x_ref[...], g_ref[...], b_ref[...])
    h_ref[...] = h
    hb, hlo = _split2(h)
    qkv_ref[...] = _dot(hb, wqkv_ref[...]).astype(BF16)
    mqk_ref[...] = _dot(hb, wmqk_ref[...]).astype(BF16)
    mv_ref[...] = _dot(hb, wmv_ref[...]).astype(BF16)
    mo_ref[...] = _dot(hb, wmo_ref[...]).astype(BF16)
    gate_ref[...] = _dot(hb, wghi_ref[...]) + _dot(hlo, wghi_ref[...]) + _dot(hb, wglo_ref[...])


def _ln_inproj(x2, g, b, w_in):
    n, d = x2.shape
    a3 = 3 * ATT_WIDTH
    wqkv = w_in[:, :a3].astype(BF16)
    wmqk = w_in[:, a3:a3 + 2 * MLSTM_WIDTH].astype(BF16)
    wmv = w_in[:, a3 + 2 * MLSTM_WIDTH:a3 + 3 * MLSTM_WIDTH].astype(BF16)
    wmo = w_in[:, a3 + 3 * MLSTM_WIDTH:a3 + 4 * MLSTM_WIDTH].astype(BF16)
    wg = jnp.pad(w_in[:, a3 + 4 * MLSTM_WIDTH:], ((0, 0), (0, LANES - 2 * MLSTM_HEADS)))
    wghi = wg.astype(BF16)
    wglo = (wg - wghi.astype(F32)).astype(BF16)
    rows = lambda w: pl.BlockSpec((ROW_BLOCK, w), lambda i: (i, 0))
    return pl.pallas_call(
        _ln_inproj_kernel,
        grid=(n // ROW_BLOCK,),
        in_specs=[rows(d), _full((1, d)), _full((1, d)), _full(wqkv.shape), _full(wmqk.shape),
                  _full(wmv.shape), _full(wmo.shape), _full(wghi.shape), _full(wglo.shape)],
        out_specs=[rows(d), rows(a3), rows(2 * MLSTM_WIDTH), rows(MLSTM_WIDTH), rows(MLSTM_WIDTH), rows(LANES)],
        out_shape=[jax.ShapeDtypeStruct((n, d), F32), jax.ShapeDtypeStruct((n, a3), BF16),
                   jax.ShapeDtypeStruct((n, 2 * MLSTM_WIDTH), BF16), jax.ShapeDtypeStruct((n, MLSTM_WIDTH), BF16),
                   jax.ShapeDtypeStruct((n, MLSTM_WIDTH), BF16), jax.ShapeDtypeStruct((n, LANES), F32)],
        compiler_params=_params("arbitrary"),
        name="ln_inproj",
    )(x2, g.reshape(1, d), b.reshape(1, d), wqkv, wmqk, wmv, wmo, wghi, wglo)


def _attn_kernel(q_ref, k0_ref, k1_ref, k2_ref, v0_ref, v1_ref, v2_ref, bias_ref, o_ref):
    i = pl.program_id(1)
    nkeys = ATT_KEY_BLOCKS * ATT_BLOCK
    q = q_ref[0] * (ATT_HEAD_DIM ** -0.5)
    kcat = jnp.concatenate([k0_ref[0], k1_ref[0], k2_ref[0]], axis=0)
    vcat = jnp.concatenate([v0_ref[0], v1_ref[0], v2_ref[0]], axis=0)
    col = lax.broadcasted_iota(I32, (1, nkeys), 1)
    in_seq = col >= (ATT_KEY_BLOCKS - 1 - i) * ATT_BLOCK
    lane = lax.broadcasted_iota(I32, (1, LANES), 1)
    low = lane < ATT_HEAD_DIM
    outs = []
    for pair in range(ATT_HEADS // 2):
        sl = slice(pair * LANES, (pair + 1) * LANES)
        qp, kp, vp = q[:, sl], kcat[:, sl], vcat[:, sl]
        halves = []
        for half in range(2):
            keep = low if half == 0 else jnp.logical_not(low)
            qh = jnp.where(keep, qp, jnp.zeros_like(qp))
            s = _dot_nt(qh, kp) + bias_ref[2 * pair + half]
            s = jnp.where(in_seq, s, NEG_BIG)
            m = jnp.max(s, -1, keepdims=True)
            p = jnp.exp(s - m)
            l = jnp.sum(p, -1, keepdims=True)
            halves.append(_dot(p.astype(BF16), vp) / l)
        outs.append(jnp.where(low, halves[0], halves[1]))
    o_ref[0] = jnp.concatenate(outs, axis=-1).astype(BF16)


def _attn_bias(rel_bias):
    nq, nk = ATT_BLOCK, ATT_KEY_BLOCKS * ATT_BLOCK
    r = jnp.arange(nq)[:, None]
    j = jnp.arange(nk)[None, :]
    allowed = (j // CHUNK >= r // CHUNK) & (j // CHUNK <= r // CHUNK + ATT_LEFT_CHUNKS)
    period = nq + nk - 1
    i = jnp.arange(period)
    d = jnp.where(i < nk, i, i - period)
    rel = d - (ATT_KEY_BLOCKS - 1) * ATT_BLOCK
    line = rel_bias[:, jnp.clip(rel, -MAX_REL, MAX_REL) + MAX_REL].astype(F32)
    rep = jnp.tile(line, (1, nq + 1))[:, :nq * (period - 1)]
    tab = rep.reshape(-1, nq, period - 1)[:, :, :nk]
    return jnp.where(allowed[None], tab, NEG_BIG)


def _attention(qkv, rel_bias, batch, seq):
    assert ATT_LEFT_CHUNKS * CHUNK == (ATT_KEY_BLOCKS - 1) * ATT_BLOCK
    qkv3 = qkv.reshape(batch, seq, 3 * ATT_WIDTH)
    bias = _attn_bias(rel_bias)
    blk = (1, ATT_BLOCK, ATT_WIDTH)

    def kv_spec(col, j):
        return pl.BlockSpec(blk, lambda b, i: (b, jnp.maximum(i - (ATT_KEY_BLOCKS - 1) + j, 0), col))

    return pl.pallas_call(
        _attn_kernel,
        grid=(batch, seq // ATT_BLOCK),
        in_specs=[pl.BlockSpec(blk, lambda b, i: (b, i, 0))]
        + [kv_spec(1, j) for j in range(ATT_KEY_BLOCKS)] + [kv_spec(2, j) for j in range(ATT_KEY_BLOCKS)]
        + [_full(bias.shape)],
        out_specs=pl.BlockSpec(blk, lambda b, i: (b, i, 0)),
        out_shape=jax.ShapeDtypeStruct((batch, seq, ATT_WIDTH), BF16),
        compiler_params=_params("arbitrary", "arbitrary"),
        name="attn",
    )(qkv3, qkv3, qkv3, qkv3, qkv3, qkv3, qkv3, bias)


def _mlstm_kernel(mqk_ref, mv_ref, mo_ref, gate_ref, convw_ref, convb_ref, gbias_ref, ng_ref, out_ref,
                  xpad, q_scr, k_scr, gate_scr, logf_scr, c_st, n_st, m_st):
    j = pl.program_id(1)
    rows = MLSTM_BLOCK
    d = MLSTM_HEAD_DIM

    @pl.when(j == 0)
    def _():
        xpad[:, 0:SUBLANES, :] = jnp.zeros((MLSTM_SEQS, SUBLANES, 2 * MLSTM_WIDTH), F32)
        c_st[...] = jnp.zeros_like(c_st)
        n_st[...] = jnp.zeros_like(n_st)
        m_st[...] = jnp.zeros_like(m_st)

    for bb in range(MLSTM_SEQS):
        xpad[bb, SUBLANES:SUBLANES + rows, :] = mqk_ref[bb].astype(F32)
        acc = jnp.broadcast_to(convb_ref[...], (rows, 2 * MLSTM_WIDTH))
        for t in range(CONV_WIDTH):
            acc = acc + convw_ref[t:t + 1, :] * xpad[bb, pl.ds(SUBLANES - (CONV_WIDTH - 1) + t, rows), :]
        xpad[bb, 0:SUBLANES, :] = xpad[bb, rows:rows + SUBLANES, :]
        qk = acc * jax.nn.sigmoid(acc)
        q_scr[bb] = qk[:, :MLSTM_WIDTH].astype(BF16)
        k_scr[bb] = qk[:, MLSTM_WIDTH:] * (d ** -0.5)

        gates = gate_ref[bb] + gbias_ref[...]
        gate_scr[bb] = gates
        logf_scr[bb] = jax.nn.log_sigmoid(gates)

    ri = lax.broadcasted_iota(I32, (CHUNK, CHUNK), 0)
    ci = lax.broadcasted_iota(I32, (CHUNK, CHUNK), 1)
    causal = ci <= ri
    tri = causal.astype(BF16)

    def chunk_body(c, carry):
        for bb in range(MLSTM_SEQS):
            _mlstm_chunk(bb, pl.multiple_of(c * CHUNK, CHUNK), causal, tri, mv_ref, mo_ref, ng_ref, out_ref,
                         q_scr, k_scr, gate_scr, logf_scr, c_st, n_st, m_st)
        return carry

    lax.fori_loop(0, rows // CHUNK, chunk_body, 0)


def _mlstm_chunk(bb, r0, causal, tri, mv_ref, mo_ref, ng_ref, out_ref, q_scr, k_scr, gate_scr, logf_scr,
                 c_st, n_st, m_st):
    d = MLSTM_HEAD_DIM
    g = gate_scr[bb, pl.ds(r0, CHUNK), :]
    lf = logf_scr[bb, pl.ds(r0, CHUNK), :]
    b3 = _dot(tri, jnp.concatenate(_split3(lf), axis=1))
    bcol = b3[:, :LANES] + b3[:, LANES:2 * LANES] + b3[:, 2 * LANES:]
    g_t = g.T
    b_t = bcol.T
    m_all = m_st[bb]
    n_all = n_st[bb]
    for h in range(MLSTM_HEADS):
        hs = slice(h * d, (h + 1) * d)
        ic = g[:, h:h + 1]
        bc = bcol[:, MLSTM_HEADS + h:MLSTM_HEADS + h + 1]
        ir = g_t[h:h + 1, :]
        br = b_t[MLSTM_HEADS + h:MLSTM_HEADS + h + 1, :]
        m_prev = m_all[h:h + 1, 0:1]
        n_prev = n_all[h:h + 1, :]
        c_prev = c_st[bb, h]
        qh = q_scr[bb, pl.ds(r0, CHUNK), hs]
        kh = k_scr[bb, pl.ds(r0, CHUNK), hs]
        vh = mv_ref[bb, pl.ds(r0, CHUNK), hs]

        log_d = jnp.where(causal, bc - br + ir, NEG_BIG)
        inter = bc + m_prev
        m_t = jnp.maximum(inter, jnp.max(log_d, -1, keepdims=True))
        d_mat = jnp.exp(log_d - m_t)
        w_inter = jnp.exp(inter - m_t)
        qk_d = _dot_nt(qh, kh.astype(BF16)) * d_mat
        lhs = jnp.concatenate([(w_inter * qh.astype(F32)).astype(BF16), qk_d.astype(BF16)], axis=1)
        num = _dot(lhs, jnp.concatenate([c_prev.astype(BF16), vh], axis=0))
        den = (w_inter * jnp.sum(qh.astype(F32) * n_prev, -1, keepdims=True)
               + jnp.sum(qk_d, -1, keepdims=True))
        hh = num / jnp.maximum(jnp.abs(den), jnp.exp(-m_t))

        b_last = bc[CHUNK - 1:CHUNK, :]
        log_in = b_last - bc + ic
        m_new = jnp.maximum(b_last + m_prev, jnp.max(log_in, 0, keepdims=True))
        w_prev = jnp.exp(b_last + m_prev - m_new)
        kw = kh * jnp.exp(log_in - m_new)
        c_st[bb, h] = w_prev * c_prev + _dot_tn(kw.astype(BF16), vh)
        n_st[bb, h:h + 1, :] = w_prev * n_prev + jnp.sum(kw, 0, keepdims=True)
        m_st[bb, h:h + 1, :] = jnp.broadcast_to(m_new, (1, LANES))

        mu = jnp.mean(hh, -1, keepdims=True)
        hc = hh - mu
        var = jnp.mean(hc * hc, -1, keepdims=True)
        hn = hc * lax.rsqrt(var + LN_EPS) * ng_ref[:, hs]
        og = jax.nn.sigmoid(mo_ref[bb, pl.ds(r0, CHUNK), hs].astype(F32))
        out_ref[bb, pl.ds(r0, CHUNK), hs] = (og * hn).astype(BF16)


def _mlstm(mqk, mv, mo, gates, conv_w, conv_b, i_bias, f_bias, norm_g, batch, seq):
    w2 = 2 * MLSTM_WIDTH
    gbias = jnp.pad(jnp.concatenate([i_bias, f_bias]).astype(F32), (0, LANES - 2 * MLSTM_HEADS)).reshape(1, LANES)
    ns = MLSTM_SEQS
    assert batch % ns == 0
    blk = lambda w: pl.BlockSpec((ns, MLSTM_BLOCK, w), lambda b, i: (b, i, 0))
    return pl.pallas_call(
        _mlstm_kernel,
        grid=(batch // ns, seq // MLSTM_BLOCK),
        in_specs=[blk(w2), blk(MLSTM_WIDTH), blk(MLSTM_WIDTH), blk(LANES),
                  _full((CONV_WIDTH, w2)), _full((1, w2)), _full((1, LANES)), _full((1, MLSTM_WIDTH))],
        out_specs=blk(MLSTM_WIDTH),
        out_shape=jax.ShapeDtypeStruct((batch, seq, MLSTM_WIDTH), BF16),
        scratch_shapes=[pltpu.VMEM((ns, MLSTM_BLOCK + SUBLANES, w2), F32),
                        pltpu.VMEM((ns, MLSTM_BLOCK, MLSTM_WIDTH), BF16),
                        pltpu.VMEM((ns, MLSTM_BLOCK, MLSTM_WIDTH), F32),
                        pltpu.VMEM((ns, MLSTM_BLOCK, LANES), F32),
                        pltpu.VMEM((ns, MLSTM_BLOCK, LANES), F32),
                        pltpu.VMEM((ns, MLSTM_HEADS, MLSTM_HEAD_DIM, MLSTM_HEAD_DIM), F32),
                        pltpu.VMEM((ns, SUBLANES, MLSTM_HEAD_DIM), F32),
                        pltpu.VMEM((ns, SUBLANES, LANES), F32)],
        compiler_params=_params("arbitrary", "arbitrary"),
        name="mlstm",
    )(mqk.reshape(batch, seq, w2), mv.reshape(batch, seq, MLSTM_WIDTH), mo.reshape(batch, seq, MLSTM_WIDTH),
      gates.reshape(batch, seq, LANES), conv_w.astype(F32), conv_b.reshape(1, w2).astype(F32), gbias,
      norm_g.reshape(1, MLSTM_WIDTH).astype(F32))


def _outproj_kernel(att_ref, hm_ref, h_ref, wa_ref, wm_ref, g_ref, b_ref, o_ref):
    y = _dot(att_ref[...], wa_ref[...]) + _dot(hm_ref[...], wm_ref[...])
    o_ref[...] = _layer_norm(DEEPNORM_ALPHA * h_ref[...] + y, g_ref[...], b_ref[...])


def _outproj(att, hm, h, w_out, g, b):
    n, d = h.shape
    wa = w_out[:ATT_WIDTH].astype(BF16)
    wm = w_out[ATT_WIDTH:].astype(BF16)
    rows = lambda w: pl.BlockSpec((ROW_BLOCK, w), lambda i: (i, 0))
    return pl.pallas_call(
        _outproj_kernel,
        grid=(n // ROW_BLOCK,),
        in_specs=[rows(ATT_WIDTH), rows(MLSTM_WIDTH), rows(d), _full(wa.shape), _full(wm.shape),
                  _full((1, d)), _full((1, d))],
        out_specs=rows(d),
        out_shape=jax.ShapeDtypeStruct((n, d), F32),
        compiler_params=_params("arbitrary"),
        name="outproj",
    )(att, hm, h, wa, wm, g.reshape(1, d), b.reshape(1, d))


def _kvproj_kernel(mem_ref, w_ref, k_ref, v_ref):
    kv = _dot(mem_ref[...].astype(BF16), w_ref[...])
    d = k_ref.shape[-1]
    k_ref[...] = kv[:, :d].astype(BF16)
    v_ref[...] = kv[:, d:].astype(BF16)


def _kvproj(mem2, w_kv):
    n, d = mem2.shape
    w = w_kv.astype(BF16)
    blk = min(ROW_BLOCK, n)
    rows = pl.BlockSpec((blk, d), lambda i: (i, 0))
    return pl.pallas_call(
        _kvproj_kernel,
        grid=(n // blk,),
        in_specs=[rows, _full(w.shape)],
        out_specs=[rows, rows],
        out_shape=[jax.ShapeDtypeStruct((n, d), BF16)] * 2,
        compiler_params=_params("arbitrary"),
        name="kvproj",
    )(mem2, w)


def _xattn_kernel(h_ref, k_ref, v_ref, wq_ref, wo_ref, g_ref, b_ref, o_ref):
    h = h_ref[0]
    d = h.shape[-1]
    dh = d // XATT_HEADS
    q = (_dot(h.astype(BF16), wq_ref[...]) * (dh ** -0.5)).astype(BF16)
    outs = []
    for hd in range(XATT_HEADS):
        sl = slice(hd * dh, (hd + 1) * dh)
        s = _dot_nt(q[:, sl], k_ref[0, :, sl])
        m = jnp.max(s, -1, keepdims=True)
        p = jnp.exp(s - m)
        l = jnp.sum(p, -1, keepdims=True)
        outs.append((_dot(p.astype(BF16), v_ref[0, :, sl]) / l).astype(BF16))
    y = _dot(jnp.concatenate(outs, axis=-1), wo_ref[...])
    o_ref[0] = _layer_norm(DEEPNORM_ALPHA * h + y, g_ref[...], b_ref[...])


def _xattn(h3, k3, v3, w_q, w_o, g, b):
    batch, seq, d = h3.shape
    m = k3.shape[1]
    wq = w_q.astype(BF16)
    wo = w_o.astype(BF16)
    blk = pl.BlockSpec((1, ROW_BLOCK, d), lambda bb, i: (bb, i, 0))
    mem = pl.BlockSpec((1, m, d), lambda bb, i: (bb, 0, 0))
    return pl.pallas_call(
        _xattn_kernel,
        grid=(batch, seq // ROW_BLOCK),
        in_specs=[blk, mem, mem, _full(wq.shape), _full(wo.shape), _full((1, d)), _full((1, d))],
        out_specs=blk,
        out_shape=jax.ShapeDtypeStruct((batch, seq, d), F32),
        compiler_params=_params("arbitrary", "arbitrary"),
        name="xattn",
    )(h3, k3, v3, wq, wo, g.reshape(1, d), b.reshape(1, d))


def _topk_rows(s, k):
    groups = s.shape[0] // SUBLANES
    t = s.shape[1]
    sg = [s[g * SUBLANES:(g + 1) * SUBLANES] for g in range(groups)]
    sub = lax.broadcasted_iota(I32, (SUBLANES, t), 0)
    vals, ids = [], []
    for _ in range(k):
        m8, a8 = sg[0], jnp.zeros((SUBLANES, t), I32)
        for g in range(1, groups):
            upd = sg[g] > m8
            m8 = jnp.where(upd, sg[g], m8)
            a8 = jnp.where(upd, g, a8)
        m = jnp.max(m8, axis=0, keepdims=True)
        kid = jnp.min(jnp.where(m8 == m, sub * groups + a8, SUBLANES * groups), axis=0, keepdims=True)
        vals.append(m)
        ids.append(kid)
        gone = jnp.where(sub * groups + a8 == kid, a8, -1)
        sg = [jnp.where(gone == g, -jnp.inf, sg[g]) for g in range(groups)]
    return vals, ids


def _stack_rows(rows_list):
    k = len(rows_list)
    t = rows_list[0].shape[-1]
    iota = lax.broadcasted_iota(I32, (k, t), 0)
    out = jnp.broadcast_to(rows_list[0], (k, t))
    for r in range(1, k):
        out = jnp.where(iota == r, jnp.broadcast_to(rows_list[r], (k, t)), out)
    return out


def _peer_topk_kernel(h_ref, wq_ref, khi_ref, klo_ref, row_ref, hbit_ref, g_ref, e_scr, g_scr):
    t = h_ref.shape[0]
    kk = PEER_TOPK
    q = _dot(h_ref[...].astype(BF16), wq_ref[...])
    half8 = kk // 2
    sub = lax.broadcasted_iota(I32, (half8, t), 0)
    for hh in range(PEER_HEADS):
        tops = []
        for p in range(2):
            c0 = (hh * 2 + p) * N_KEYS
            qhi, qlo = _split2(q[:, c0:c0 + N_KEYS])
            s = _dot_nt(khi_ref[p], qhi) + _dot_nt(khi_ref[p], qlo) + _dot_nt(klo_ref[p], qhi)
            vals, idxs = _topk_rows(s, kk)
            tops.append((_stack_rows(vals), _stack_rows(idxs)))
        (s0, i0), (s1, i1) = tops
        e0 = i0 * N_KEYS
        cand, cexp, cflat = [], [], []
        for b in range(half8):
            cand.append(s0[:half8] + s1[b:b + 1])
            cexp.append(e0[:half8] + i1[b:b + 1])
            cflat.append(sub * kk + b)
        cand.append(s0[half8:] + s1[0:1])
        cexp.append(e0[half8:] + i1[0:1])
        cflat.append((sub + half8) * kk)
        cand.append(s0[0:1] + s1[half8:])
        cexp.append(e0[0:1] + i1[half8:])
        cflat.append(sub + half8)
        cand = jnp.concatenate(cand, axis=0)
        cexp = jnp.concatenate(cexp, axis=0)
        cflat = jnp.concatenate(cflat, axis=0)
        best_s, best_e = [], []
        for _ in range(kk):
            m = jnp.max(cand, axis=0, keepdims=True)
            jsel = jnp.min(jnp.where(cand == m, cflat, kk * kk), axis=0, keepdims=True)
            hit = cflat == jsel
            best_s.append(m)
            best_e.append(jnp.max(jnp.where(hit, cexp, 0), axis=0, keepdims=True))
            cand = jnp.where(hit, -jnp.inf, cand)
        bs = _stack_rows(best_s)
        ex = jnp.exp(bs - bs[0:1])
        g_scr[hh * kk:(hh + 1) * kk, :] = ex / jnp.sum(ex, axis=0, keepdims=True)
        e_scr[hh * kk:(hh + 1) * kk, :] = _stack_rows(best_e)
    e = e_scr[...]
    hb = e >> (PEER_HALF_EXPERTS.bit_length() - 1)
    row_ref[...] = ((e & (PEER_HALF_EXPERTS - 1)) * SUBLANES).T
    hbit_ref[...] = hb.astype(F32).T
    g_ref[...] = g_scr[...].T


def _peer_topk(h2, w_query, sub_keys):
    n, d = h2.shape
    wq = w_query.astype(BF16)
    keys = sub_keys.reshape(2, SUBLANES, N_KEYS // SUBLANES, -1).swapaxes(1, 2).reshape(sub_keys.shape)
    khi = keys.astype(BF16)
    klo = (keys - khi.astype(F32)).astype(BF16)
    rows = pl.BlockSpec((TOPK_BLOCK, PEER_PICKS), lambda i: (i, 0))
    return pl.pallas_call(
        _peer_topk_kernel,
        grid=(n // TOPK_BLOCK,),
        in_specs=[pl.BlockSpec((TOPK_BLOCK, d), lambda i: (i, 0)), _full(wq.shape), _full(khi.shape), _full(klo.shape)],
        out_specs=[rows, rows, rows],
        out_shape=[jax.ShapeDtypeStruct((n, PEER_PICKS), I32),
                   jax.ShapeDtypeStruct((n, PEER_PICKS), F32), jax.ShapeDtypeStruct((n, PEER_PICKS), F32)],
        scratch_shapes=[pltpu.VMEM((PEER_PICKS, TOPK_BLOCK), I32), pltpu.VMEM((PEER_PICKS, TOPK_BLOCK), F32)],
        compiler_params=_params("arbitrary"),
        name="peer_topk",
    )(h2, wq, khi, klo)


def _pack_kernel(lo_ref, hi_ref, o_ref):
    rows = lo_ref.shape[0]
    lo = pltpu.bitcast(lo_ref[...].astype(BF16).astype(F32), U32) >> 16
    hi = pltpu.bitcast(hi_ref[...].astype(BF16).astype(F32), U32) & jnp.uint32(0xFFFF0000)
    word = lo | hi
    for s in range(word.shape[1] // LANES):
        o_ref[pl.ds(s, rows, stride=SUBLANES), :] = word[:, s * LANES:(s + 1) * LANES]


def _pack_table(tab):
    e, d = tab.shape
    assert d == SUBLANES * LANES
    half_blocks = e // 2 // PACK_BLOCK
    return pl.pallas_call(
        _pack_kernel,
        grid=(half_blocks,),
        in_specs=[pl.BlockSpec((PACK_BLOCK, d), lambda i: (i, 0)),
                  pl.BlockSpec((PACK_BLOCK, d), lambda i: (i + half_blocks, 0))],
        out_specs=pl.BlockSpec((PACK_BLOCK * SUBLANES, LANES), lambda i: (i, 0)),
        out_shape=jax.ShapeDtypeStruct((e // 2 * SUBLANES, LANES), U32),
        compiler_params=_params("arbitrary"),
        name="pack_table",
    )(tab, tab)


def _table_tile(tab_ref, row8):
    return pltpu.bitcast(tab_ref[pl.ds(pl.multiple_of(row8, SUBLANES), SUBLANES), :], BF16)


def _peer_u_consts():
    k = jnp.arange(2 * LANES)
    col = jnp.arange(2 * PEER_PICKS)
    row = jnp.arange(PEER_PICKS)
    fold = ((k[None, :] // 16 == jnp.arange(2 * U_TILE_PICKS)[:, None] // 2)
            & (k[None, :] % 2 == jnp.arange(2 * U_TILE_PICKS)[:, None] % 2)).astype(BF16)
    q = jnp.arange(LANES)
    r = jnp.arange(SUBLANES)[:, None, None]
    rowq = 8 * ((q % 8) // 2)[None, :, None] + r
    colq = 2 * (32 * (q // 8)[None, :, None] + 2 * (rowq // 2) + (q % 2)[None, :, None]) + rowq % 2
    place = ((col[None, None, :] == colq) & (q < 32)[None, :, None]).astype(BF16)
    dup = ((col[None, :] // 2) == jnp.arange(PEER_PICKS)[:, None]).astype(BF16)
    return fold, place, dup


def _index_copy(row_hbm, idx_smem, sem, batch, slot):
    return pltpu.make_async_copy(row_hbm.at[pl.ds(batch * PEER_UNROLL, PEER_UNROLL)], idx_smem.at[slot], sem.at[slot])


def _for_each_token_batch(row_hbm, idx_smem, sem, block_tokens, body):
    step = pl.program_id(0)
    per_step = block_tokens // PEER_UNROLL
    assert per_step % 2 == 0
    total = pl.num_programs(0) * per_step

    @pl.when(step == 0)
    def _():
        _index_copy(row_hbm, idx_smem, sem, 0, 0).start()

    def pair(k, carry):
        for slot in range(2):
            local = 2 * k + slot
            batch = step * per_step + local
            _index_copy(row_hbm, idx_smem, sem, batch, slot).wait()

            @pl.when(batch + 1 < total)
            def _():
                _index_copy(row_hbm, idx_smem, sem, batch + 1, 1 - slot).start()

            body(local * PEER_UNROLL, idx_smem.at[slot])
        return carry

    lax.fori_loop(0, per_step // 2, pair, 0)


def _peer_u_kernel(row_hbm, x_ref, g_ref, hbit_ref, tab_ref, fold_ref, place_ref, dup_ref,
                   coefh_ref, sum_scr, idx_smem, sem):
    tb = x_ref.shape[0]
    ntiles = PEER_PICKS // (2 * U_TILE_PICKS)
    lane = lax.broadcasted_iota(I32, (SUBLANES, LANES), 1)
    sub = lax.broadcasted_iota(I32, (SUBLANES, LANES), 0)

    def tokens(t0, idx):
        by_row = None
        for u in range(PEER_UNROLL):
            t = t0 + u
            x = x_ref[t]
            xbits = pltpu.bitcast(x.astype(BF16).astype(F32), U32)
            xw = pltpu.bitcast(xbits | (xbits >> 16), BF16)
            sums = jnp.zeros((SUBLANES, LANES), F32)
            for j in range(ntiles):
                cols = []
                for ab in range(2):
                    prods = []
                    for mm in range(U_TILE_PICKS):
                        c = j * 2 * U_TILE_PICKS + 2 * mm + ab
                        prods.append(_table_tile(tab_ref, idx[u, c]) * xw)
                    cols.append(jnp.concatenate(prods, axis=0))
                res = _dot(fold_ref[...], jnp.concatenate(cols, axis=1))
                for rb in range(2 * U_TILE_PICKS // SUBLANES):
                    for ab in range(2):
                        part = res[rb * SUBLANES:(rb + 1) * SUBLANES, ab * LANES:(ab + 1) * LANES]
                        q = j * 8 + rb * 2 + ab
                        sums = jnp.where(lane == q, jnp.sum(part, axis=1, keepdims=True), sums)
            if u % SUBLANES == 0:
                by_row = [jnp.zeros((SUBLANES, LANES), F32)] * SUBLANES
            by_row = [jnp.where(sub == u % SUBLANES, jnp.broadcast_to(sums[r:r + 1], (SUBLANES, LANES)), by_row[r])
                      for r in range(SUBLANES)]
            if u % SUBLANES == SUBLANES - 1:
                first = pl.multiple_of(t0 + u - (SUBLANES - 1), SUBLANES)
                for r in range(SUBLANES):
                    sum_scr[r, pl.ds(first, SUBLANES), :] = by_row[r]

    _for_each_token_batch(row_hbm, idx_smem, sem, tb, tokens)

    a2 = jnp.zeros((tb, 2 * PEER_PICKS), F32)
    for r in range(SUBLANES):
        shi, slo = _split2(sum_scr[r])
        a2 = a2 + _dot(shi, place_ref[r]) + _dot(slo, place_ref[r])
    ghi, glo = _split2(g_ref[...])
    g2 = _dot(ghi, dup_ref[...]) + _dot(glo, dup_ref[...])
    hbit2 = _dot(hbit_ref[...].astype(BF16), dup_ref[...])
    parity = (lax.broadcasted_iota(I32, a2.shape, 1) % 2).astype(F32)
    coef = g2 * (0.5 * a2 * (1.0 + lax.erf(a2 * math.sqrt(0.5))))
    coefh_ref[...] = jnp.where(hbit2 == parity, coef, 0.0).astype(BF16)


def _peer_v_consts():
    col = jnp.arange(2 * PEER_PICKS * SUBLANES)
    ab, p, r = col // (PEER_PICKS * SUBLANES), (col // 16) % (PEER_PICKS // 2), col % 16
    src = jnp.arange(2 * PEER_PICKS)
    expand = ((src[:, None] // 2 == (2 * p + ab)[None, :]) & (src[:, None] % 2 == (r % 2)[None, :])).astype(BF16)
    kk = jnp.arange(PEER_PICKS * SUBLANES)
    diag = ((kk[None, :] % 16) // 2 == (jnp.arange(16) % SUBLANES)[:, None]).astype(F32)
    return expand, diag


def _peer_v_kernel(row_hbm, coefh_ref, x_ref, tab_ref, expand_ref, diag_ref, g_ref, b_ref, o_ref, lrow_scr, y_scr,
                   idx_smem, sem):
    tb = x_ref.shape[0]
    kdim = PEER_PICKS * SUBLANES
    lrow_scr[...] = _dot(coefh_ref[...], expand_ref[...])
    diag = diag_ref[...]
    sub = lax.broadcasted_iota(I32, (SUBLANES, LANES), 0)

    def tokens(t0, idx):
        by_row = None
        for u in range(PEER_UNROLL):
            t = t0 + u
            even = jnp.broadcast_to(lrow_scr[pl.ds(t, 1), 0:kdim], (SUBLANES, kdim))
            odd = jnp.broadcast_to(lrow_scr[pl.ds(t, 1), kdim:2 * kdim], (SUBLANES, kdim))
            lhs = (jnp.concatenate([even, odd], axis=0) * diag).astype(BF16)
            tiles = []
            for p in range(PEER_PICKS // 2):
                wa = _table_tile(tab_ref, idx[u, 2 * p])
                wb = _table_tile(tab_ref, idx[u, 2 * p + 1])
                tiles.append(jnp.concatenate([wa, wb], axis=1))
            res = _dot(lhs, jnp.concatenate(tiles, axis=0))
            y = res[0:SUBLANES, 0:LANES] + res[SUBLANES:, LANES:]
            if u % SUBLANES == 0:
                by_row = [jnp.zeros((SUBLANES, LANES), F32)] * SUBLANES
            by_row = [jnp.where(sub == u % SUBLANES, jnp.broadcast_to(y[r:r + 1], (SUBLANES, LANES)), by_row[r])
                      for r in range(SUBLANES)]
            if u % SUBLANES == SUBLANES - 1:
                first = pl.multiple_of(t0 + u - (SUBLANES - 1), SUBLANES)
                for r in range(SUBLANES):
                    y_scr[r, pl.ds(first, SUBLANES), :] = by_row[r]

    _for_each_token_batch(row_hbm, idx_smem, sem, tb, tokens)
    z = [DEEPNORM_ALPHA * x_ref[:, r * LANES:(r + 1) * LANES] + y_scr[r] for r in range(SUBLANES)]
    cnt = SUBLANES * LANES
    mu = sum(jnp.sum(zr, axis=1, keepdims=True) for zr in z) / cnt
    zc = [zr - mu for zr in z]
    var = sum(jnp.sum(c * c, axis=1, keepdims=True) for c in zc) / cnt
    rstd = lax.rsqrt(var + LN_EPS)
    for r in range(SUBLANES):
        o_ref[:, r * LANES:(r + 1) * LANES] = zc[r] * rstd * g_ref[r:r + 1, :] + b_ref[r:r + 1, :]


def _peer_ffn(h2, row, hbit, gate, expert_u, expert_v, ln_g, ln_b):
    n, d = h2.shape
    rows8 = d // LANES
    assert rows8 == SUBLANES and PEER_PICKS == LANES and expert_u.shape[0] == 2 * PEER_HALF_EXPERTS
    x3 = h2.reshape(n, rows8, LANES)
    u_pk = _pack_table(expert_u)
    v_pk = _pack_table(expert_v)
    hbm = pl.BlockSpec(memory_space=pl.ANY)
    staging = [pltpu.SMEM((2, PEER_UNROLL, PEER_PICKS), I32), pltpu.SemaphoreType.DMA((2,))]
    picks =pl.BlockSpec((PEER_BLOCK, PEER_PICKS), lambda i: (i, 0))
    picks2 = pl.BlockSpec((PEER_BLOCK, 2 * PEER_PICKS), lambda i: (i, 0))
    tok = pl.BlockSpec((PEER_BLOCK, rows8, LANES), lambda i: (i, 0, 0))
    flat = pl.BlockSpec((PEER_BLOCK, d), lambda i: (i, 0))
    table = pl.BlockSpec(u_pk.shape, lambda i: (0, 0), pipeline_mode=pl.Buffered(1))
    u_consts = _peer_u_consts()
    coefh = pl.pallas_call(
        _peer_u_kernel,
        grid=(n // PEER_BLOCK,),
        in_specs=[hbm, tok, picks, picks, table] + [_full(c.shape) for c in u_consts],
        out_specs=picks2,
        out_shape=jax.ShapeDtypeStruct((n, 2 * PEER_PICKS), BF16),
        scratch_shapes=[pltpu.VMEM((SUBLANES, PEER_BLOCK, LANES), F32)] + staging,
        compiler_params=_params("arbitrary"),
        name="peer_u",
    )(row, x3, gate, hbit, u_pk, *u_consts)
    v_consts = _peer_v_consts()
    out = pl.pallas_call(
        _peer_v_kernel,
        grid=(n // PEER_BLOCK,),
        in_specs=[hbm, picks2, flat, table] + [_full(c.shape) for c in v_consts]
        + [_full((rows8, LANES)), _full((rows8, LANES))],
        out_specs=flat,
        out_shape=jax.ShapeDtypeStruct((n, d), F32),
        scratch_shapes=[pltpu.VMEM((PEER_BLOCK, 2 * PEER_PICKS * SUBLANES), F32),
                        pltpu.VMEM((rows8, PEER_BLOCK, LANES), F32)] + staging,
        compiler_params=_params("arbitrary"),
        name="peer_v",
    )(row, coefh, h2, v_pk, *v_consts, ln_g.reshape(rows8, LANES), ln_b.reshape(rows8, LANES))
    return out


def kernel(x, mem, ln_in_g, ln_in_b, w_in, conv_w, conv_b, mlstm_i_bias, mlstm_f_bias, mlstm_norm_g, rel_bias, w_out, ln1_g, ln1_b, xattn_w_q, xattn_w_kv, xattn_w_o, ln2_g, ln2_b, peer_w_query, peer_sub_keys, peer_u, peer_v, ln3_g, ln3_b):
    batch, seq, d = x.shape
    n = batch * seq
    assert w_in.shape[0] == DEPTH
    h, qkv, mqk, mv, mo, gates = _ln_inproj(x.reshape(n, d), ln_in_g, ln_in_b, w_in[0])
    for l in range(DEPTH):
        if l > 0:
            raise NotImplementedError("input projection of deeper layers")
        att = _attention(qkv, rel_bias[l], batch, seq)
        hm = _mlstm(mqk, mv, mo, gates, conv_w[l], conv_b[l], mlstm_i_bias[l], mlstm_f_bias[l], mlstm_norm_g[l],
                    batch, seq)
        h = _outproj(att.reshape(n, ATT_WIDTH), hm.reshape(n, MLSTM_WIDTH), h, w_out[l], ln1_g[l], ln1_b[l])
        k2, v2 = _kvproj(mem.reshape(-1, d), xattn_w_kv[l])
        m = mem.shape[1]
        h = _xattn(h.reshape(batch, seq, d), k2.reshape(batch, m, d), v2.reshape(batch, m, d),
                   xattn_w_q[l], xattn_w_o[l], ln2_g[l], ln2_b[l]).reshape(n, d)
        row, hbit, gate = _peer_topk(h, peer_w_query[l], peer_sub_keys[l])
        h = _peer_ffn(h, row, hbit, gate, peer_u[l], peer_v[l], ln3_g[l], ln3_b[l])
    return h.reshape(batch, seq, d)
```

```python
import functools
import math

import jax
import jax.numpy as jnp
from jax import lax
from jax.experimental import pallas as pl
from jax.experimental.pallas import tpu as pltpu

F32, BF16, I32, U32 = jnp.float32, jnp.bfloat16, jnp.int32, jnp.uint32

DEPTH = 1
LN_EPS = 1e-5
DEEPNORM_ALPHA = (2.0 * DEPTH) ** 0.25
CHUNK = 64
ATT_HEADS, ATT_HEAD_DIM, ATT_LEFT_CHUNKS, MAX_REL = 8, 64, 8, 128
ATT_WIDTH = ATT_HEADS * ATT_HEAD_DIM
MLSTM_HEADS, MLSTM_HEAD_DIM, CONV_WIDTH = 4, 128, 4
MLSTM_WIDTH = MLSTM_HEADS * MLSTM_HEAD_DIM
XATT_HEADS = 4
PEER_HEADS, N_KEYS, PEER_TOPK = 8, 128, 16
PEER_PICKS = PEER_HEADS * PEER_TOPK
PEER_HALF_EXPERTS = N_KEYS * N_KEYS // 2
U_TILE_PICKS = 16
PEER_UNROLL = 16

LANES = 128
SUBLANES = 8
VMEM_LIMIT_BYTES = 56 * 1024 * 1024

NEG_BIG = -1e30

ROW_BLOCK = 512
ATT_BLOCK = 256
ATT_KEY_BLOCKS = 3
MLSTM_BLOCK = 256
MLSTM_SEQS = 1
TOPK_BLOCK = 256
PEER_BLOCK = 128
PACK_BLOCK = 256


def _params(*semantics):
    return pltpu.CompilerParams(dimension_semantics=semantics, vmem_limit_bytes=VMEM_LIMIT_BYTES)


def _layer_norm(x, g, b):
    mu = jnp.mean(x, -1, keepdims=True)
    xc = x - mu
    var = jnp.mean(xc * xc, -1, keepdims=True)
    return xc * lax.rsqrt(var + LN_EPS) * g + b


def _split2(x):
    hi = x.astype(BF16)
    lo = (x - hi.astype(F32)).astype(BF16)
    return hi, lo


def _split3(x):
    hi = x.astype(BF16)
    r = x - hi.astype(F32)
    mid = r.astype(BF16)
    lo = (r - mid.astype(F32)).astype(BF16)
    return hi, mid, lo


def _dot(a, b):
    return jnp.dot(a, b, preferred_element_type=F32)


def _dot_nt(a, b):
    return lax.dot_general(a, b, (((1,), (1,)), ((), ())), preferred_element_type=F32)


def _dot_tn(a, b):
    return lax.dot_general(a, b, (((0,), (0,)), ((), ())), preferred_element_type=F32)


def _full(shape):
    return pl.BlockSpec(shape, lambda *_: (0,) * len(shape))


def _ln_inproj_kernel(x_ref, g_ref, b_ref, wqkv_ref, wmqk_ref, wmv_ref, wmo_ref, wghi_ref, wglo_ref,
                      h_ref, qkv_ref, mqk_ref, mv_ref, mo_ref, gate_ref):
    h = _layer_norm(x_ref[...], g_ref[...], b_ref[...])
    h_ref[...] = h
    hb, hlo = _split2(h)
    qkv_ref[...] = _dot(hb, wqkv_ref[...]).astype(BF16)
    mqk_ref[...] = _dot(hb, wmqk_ref[...]).astype(BF16)
    mv_ref[...] = _dot(hb, wmv_ref[...]).astype(BF16)
    mo_ref[...] = _dot(hb, wmo_ref[...]).astype(BF16)
    gate_ref[...] = _dot(hb, wghi_ref[...]) + _dot(hlo, wghi_ref[...]) + _dot(hb, wglo_ref[...])


def _ln_inproj(x2, g, b, w_in):
    n, d = x2.shape
    a3 = 3 * ATT_WIDTH
    wqkv = w_in[:, :a3].astype(BF16)
    wmqk = w_in[:, a3:a3 + 2 * MLSTM_WIDTH].astype(BF16)
    wmv = w_in[:, a3 + 2 * MLSTM_WIDTH:a3 + 3 * MLSTM_WIDTH].astype(BF16)
    wmo = w_in[:, a3 + 3 * MLSTM_WIDTH:a3 + 4 * MLSTM_WIDTH].astype(BF16)
    wg = jnp.pad(w_in[:, a3 + 4 * MLSTM_WIDTH:], ((0, 0), (0, LANES - 2 * MLSTM_HEADS)))
    wghi = wg.astype(BF16)
    wglo = (wg - wghi.astype(F32)).astype(BF16)
    rows = lambda w: pl.BlockSpec((ROW_BLOCK, w), lambda i: (i, 0))
    return pl.pallas_call(
        _ln_inproj_kernel,
        grid=(n // ROW_BLOCK,),
        in_specs=[rows(d), _full((1, d)), _full((1, d)), _full(wqkv.shape), _full(wmqk.shape),
                  _full(wmv.shape), _full(wmo.shape), _full(wghi.shape), _full(wglo.shape)],
        out_specs=[rows(d), rows(a3), rows(2 * MLSTM_WIDTH), rows(MLSTM_WIDTH), rows(MLSTM_WIDTH), rows(LANES)],
        out_shape=[jax.ShapeDtypeStruct((n, d), F32), jax.ShapeDtypeStruct((n, a3), BF16),
                   jax.ShapeDtypeStruct((n, 2 * MLSTM_WIDTH), BF16), jax.ShapeDtypeStruct((n, MLSTM_WIDTH), BF16),
                   jax.ShapeDtypeStruct((n, MLSTM_WIDTH), BF16), jax.ShapeDtypeStruct((n, LANES), F32)],
        compiler_params=_params("arbitrary"),
        name="ln_inproj",
    )(x2, g.reshape(1, d), b.reshape(1, d), wqkv, wmqk, wmv, wmo, wghi, wglo)


def _attn_kernel(q_ref, k0_ref, k1_ref, k2_ref, v0_ref, v1_ref, v2_ref, bias_ref, o_ref):
    i = pl.program_id(1)
    nkeys = ATT_KEY_BLOCKS * ATT_BLOCK
    q = q_ref[0] * (ATT_HEAD_DIM ** -0.5)
    kcat = jnp.concatenate([k0_ref[0], k1_ref[0], k2_ref[0]], axis=0)
    vcat = jnp.concatenate([v0_ref[0], v1_ref[0], v2_ref[0]], axis=0)
    col = lax.broadcasted_iota(I32, (1, nkeys), 1)
    in_seq = col >= (ATT_KEY_BLOCKS - 1 - i) * ATT_BLOCK
    lane = lax.broadcasted_iota(I32, (1, LANES), 1)
    low = lane < ATT_HEAD_DIM
    outs = []
    for pair in range(ATT_HEADS // 2):
        sl = slice(pair * LANES, (pair + 1) * LANES)
        qp, kp, vp = q[:, sl], kcat[:, sl], vcat[:, sl]
        halves = []
        for half in range(2):
            keep = low if half == 0 else jnp.logical_not(low)
            qh = jnp.where(keep, qp, jnp.zeros_like(qp))
            s = _dot_nt(qh, kp) + bias_ref[2 * pair + half]
            s = jnp.where(in_seq, s, NEG_BIG)
            m = jnp.max(s, -1, keepdims=True)
            p = jnp.exp(s - m)
            l = jnp.sum(p, -1, keepdims=True)
            halves.append(_dot(p.astype(BF16), vp) / l)
        outs.append(jnp.where(low, halves[0], halves[1]))
    o_ref[0] = jnp.concatenate(outs, axis=-1).astype(BF16)


def _attn_bias(rel_bias):
    nq, nk = ATT_BLOCK, ATT_KEY_BLOCKS * ATT_BLOCK
    r = jnp.arange(nq)[:, None]
    j = jnp.arange(nk)[None, :]
    allowed = (j // CHUNK >= r // CHUNK) & (j // CHUNK <= r // CHUNK + ATT_LEFT_CHUNKS)
    period = nq + nk - 1
    i = jnp.arange(period)
    d = jnp.where(i < nk, i, i - period)
    rel = d - (ATT_KEY_BLOCKS - 1) * ATT_BLOCK
    line = rel_bias[:, jnp.clip(rel, -MAX_REL, MAX_REL) + MAX_REL].astype(F32)
    rep = jnp.tile(line, (1, nq + 1))[:, :nq * (period - 1)]
    tab = rep.reshape(-1, nq, period - 1)[:, :, :nk]
    return jnp.where(allowed[None], tab, NEG_BIG)


def _attention(qkv, rel_bias, batch, seq):
    assert ATT_LEFT_CHUNKS * CHUNK == (ATT_KEY_BLOCKS - 1) * ATT_BLOCK
    qkv3 = qkv.reshape(batch, seq, 3 * ATT_WIDTH)
    bias = _attn_bias(rel_bias)
    blk = (1, ATT_BLOCK, ATT_WIDTH)

    def kv_spec(col, j):
        return pl.BlockSpec(blk, lambda b, i: (b, jnp.maximum(i - (ATT_KEY_BLOCKS - 1) + j, 0), col))

    return pl.pallas_call(
        _attn_kernel,
        grid=(batch, seq // ATT_BLOCK),
        in_specs=[pl.BlockSpec(blk, lambda b, i: (b, i, 0))]
        + [kv_spec(1, j) for j in range(ATT_KEY_BLOCKS)] + [kv_spec(2, j) for j in range(ATT_KEY_BLOCKS)]
        + [_full(bias.shape)],
        out_specs=pl.BlockSpec(blk, lambda b, i: (b, i, 0)),
        out_shape=jax.ShapeDtypeStruct((batch, seq, ATT_WIDTH), BF16),
        compiler_params=_params("arbitrary", "arbitrary"),
        name="attn",
    )(qkv3, qkv3, qkv3, qkv3, qkv3, qkv3, qkv3, bias)


def _mlstm_kernel(mqk_ref, mv_ref, mo_ref, gate_ref, convw_ref, convb_ref, gbias_ref, ng_ref, out_ref,
                  xpad, q_scr, k_scr, gate_scr, logf_scr, c_st, n_st, m_st):
    j = pl.program_id(1)
    rows = MLSTM_BLOCK
    d = MLSTM_HEAD_DIM

    @pl.when(j == 0)
    def _():
        xpad[:, 0:SUBLANES, :] = jnp.zeros((MLSTM_SEQS, SUBLANES, 2 * MLSTM_WIDTH), F32)
        c_st[...] = jnp.zeros_like(c_st)
        n_st[...] = jnp.zeros_like(n_st)
        m_st[...] = jnp.zeros_like(m_st)

    for bb in range(MLSTM_SEQS):
        xpad[bb, SUBLANES:SUBLANES + rows, :] = mqk_ref[bb].astype(F32)
        acc = jnp.broadcast_to(convb_ref[...], (rows, 2 * MLSTM_WIDTH))
        for t in range(CONV_WIDTH):
            acc = acc + convw_ref[t:t + 1, :] * xpad[bb, pl.ds(SUBLANES - (CONV_WIDTH - 1) + t, rows), :]
        xpad[bb, 0:SUBLANES, :] = xpad[bb, rows:rows + SUBLANES, :]
        qk = acc * jax.nn.sigmoid(acc)
        q_scr[bb] = qk[:, :MLSTM_WIDTH].astype(BF16)
        k_scr[bb] = qk[:, MLSTM_WIDTH:] * (d ** -0.5)

        gates = gate_ref[bb] + gbias_ref[...]
        gate_scr[bb] = gates
        logf_scr[bb] = jax.nn.log_sigmoid(gates)

    ri = lax.broadcasted_iota(I32, (CHUNK, CHUNK), 0)
    ci = lax.broadcasted_iota(I32, (CHUNK, CHUNK), 1)
    causal = ci <= ri
    tri = causal.astype(BF16)

    def chunk_body(c, carry):
        for bb in range(MLSTM_SEQS):
            _mlstm_chunk(bb, pl.multiple_of(c * CHUNK, CHUNK), causal, tri, mv_ref, mo_ref, ng_ref, out_ref,
                         q_scr, k_scr, gate_scr, logf_scr, c_st, n_st, m_st)
        return carry

    lax.fori_loop(0, rows // CHUNK, chunk_body, 0)


def _mlstm_chunk(bb, r0, causal, tri, mv_ref, mo_ref, ng_ref, out_ref, q_scr, k_scr, gate_scr, logf_scr,
                 c_st, n_st, m_st):
    d = MLSTM_HEAD_DIM
    g = gate_scr[bb, pl.ds(r0, CHUNK), :]
    lf = logf_scr[bb, pl.ds(r0, CHUNK), :]
    b3 = _dot(tri, jnp.concatenate(_split3(lf), axis=1))
    bcol = b3[:, :LANES] + b3[:, LANES:2 * LANES] + b3[:, 2 * LANES:]
    g_t = g.T
    b_t = bcol.T
    m_all = m_st[bb]
    n_all = n_st[bb]
    for h in range(MLSTM_HEADS):
        hs = slice(h * d, (h + 1) * d)
        ic = g[:, h:h + 1]
        bc = bcol[:, MLSTM_HEADS + h:MLSTM_HEADS + h + 1]
        ir = g_t[h:h + 1, :]
        br = b_t[MLSTM_HEADS + h:MLSTM_HEADS + h + 1, :]
        m_prev = m_all[h:h + 1, 0:1]
        n_prev = n_all[h:h + 1, :]
        c_prev = c_st[bb, h]
        qh = q_scr[bb, pl.ds(r0, CHUNK), hs]
        kh = k_scr[bb, pl.ds(r0, CHUNK), hs]
        vh = mv_ref[bb, pl.ds(r0, CHUNK), hs]

        log_d = jnp.where(causal, bc - br + ir, NEG_BIG)
        inter = bc + m_prev
        m_t = jnp.maximum(inter, jnp.max(log_d, -1, keepdims=True))
        d_mat = jnp.exp(log_d - m_t)
        w_inter = jnp.exp(inter - m_t)
        qk_d = _dot_nt(qh, kh.astype(BF16)) * d_mat
        lhs = jnp.concatenate([(w_inter * qh.astype(F32)).astype(BF16), qk_d.astype(BF16)], axis=1)
        num = _dot(lhs, jnp.concatenate([c_prev.astype(BF16), vh], axis=0))
        den = (w_inter * jnp.sum(qh.astype(F32) * n_prev, -1, keepdims=True)
               + jnp.sum(qk_d, -1, keepdims=True))
        hh = num / jnp.maximum(jnp.abs(den), jnp.exp(-m_t))

        b_last = bc[CHUNK - 1:CHUNK, :]
        log_in = b_last - bc + ic
        m_new = jnp.maximum(b_last + m_prev, jnp.max(log_in, 0, keepdims=True))
        w_prev = jnp.exp(b_last + m_prev - m_new)
        kw = kh * jnp.exp(log_in - m_new)
        c_st[bb, h] = w_prev * c_prev + _dot_tn(kw.astype(BF16), vh)
        n_st[bb, h:h + 1, :] = w_prev * n_prev + jnp.sum(kw, 0, keepdims=True)
        m_st[bb, h:h + 1, :] = jnp.broadcast_to(m_new, (1, LANES))

        mu = jnp.mean(hh, -1, keepdims=True)
        hc = hh - mu
        var = jnp.mean(hc * hc, -1, keepdims=True)
        hn = hc * lax.rsqrt(var + LN_EPS) * ng_ref[:, hs]
        og = jax.nn.sigmoid(mo_ref[bb, pl.ds(r0, CHUNK), hs].astype(F32))
        out_ref[bb, pl.ds(r0, CHUNK), hs] = (og * hn).astype(BF16)


def _mlstm(mqk, mv, mo, gates, conv_w, conv_b, i_bias, f_bias, norm_g, batch, seq):
    w2 = 2 * MLSTM_WIDTH
    gbias = jnp.pad(jnp.concatenate([i_bias, f_bias]).astype(F32), (0, LANES - 2 * MLSTM_HEADS)).reshape(1, LANES)
    ns = MLSTM_SEQS
    assert batch % ns == 0
    blk = lambda w: pl.BlockSpec((ns, MLSTM_BLOCK, w), lambda b, i: (b, i, 0))
    return pl.pallas_call(
        _mlstm_kernel,
        grid=(batch // ns, seq // MLSTM_BLOCK),
        in_specs=[blk(w2), blk(MLSTM_WIDTH), blk(MLSTM_WIDTH), blk(LANES),
                  _full((CONV_WIDTH, w2)), _full((1, w2)), _full((1, LANES)), _full((1, MLSTM_WIDTH))],
        out_specs=blk(MLSTM_WIDTH),
        out_shape=jax.ShapeDtypeStruct((batch, seq, MLSTM_WIDTH), BF16),
        scratch_shapes=[pltpu.VMEM((ns, MLSTM_BLOCK + SUBLANES, w2), F32),
                        pltpu.VMEM((ns, MLSTM_BLOCK, MLSTM_WIDTH), BF16),
                        pltpu.VMEM((ns, MLSTM_BLOCK, MLSTM_WIDTH), F32),
                        pltpu.VMEM((ns, MLSTM_BLOCK, LANES), F32),
                        pltpu.VMEM((ns, MLSTM_BLOCK, LANES), F32),
                        pltpu.VMEM((ns, MLSTM_HEADS, MLSTM_HEAD_DIM, MLSTM_HEAD_DIM), F32),
                        pltpu.VMEM((ns, SUBLANES, MLSTM_HEAD_DIM), F32),
                        pltpu.VMEM((ns, SUBLANES, LANES), F32)],
        compiler_params=_params("arbitrary", "arbitrary"),
        name="mlstm",
    )(mqk.reshape(batch, seq, w2), mv.reshape(batch, seq, MLSTM_WIDTH), mo.reshape(batch, seq, MLSTM_WIDTH),
      gates.reshape(batch, seq, LANES), conv_w.astype(F32), conv_b.reshape(1, w2).astype(F32), gbias,
      norm_g.reshape(1, MLSTM_WIDTH).astype(F32))


def _outproj_kernel(att_ref, hm_ref, h_ref, wa_ref, wm_ref, g_ref, b_ref, o_ref):
    y = _dot(att_ref[...], wa_ref[...]) + _dot(hm_ref[...], wm_ref[...])
    o_ref[...] = _layer_norm(DEEPNORM_ALPHA * h_ref[...] + y, g_ref[...], b_ref[...])


def _outproj(att, hm, h, w_out, g, b):
    n, d = h.shape
    wa = w_out[:ATT_WIDTH].astype(BF16)
    wm = w_out[ATT_WIDTH:].astype(BF16)
    rows = lambda w: pl.BlockSpec((ROW_BLOCK, w), lambda i: (i, 0))
    return pl.pallas_call(
        _outproj_kernel,
        grid=(n // ROW_BLOCK,),
        in_specs=[rows(ATT_WIDTH), rows(MLSTM_WIDTH), rows(d), _full(wa.shape), _full(wm.shape),
                  _full((1, d)), _full((1, d))],
        out_specs=rows(d),
        out_shape=jax.ShapeDtypeStruct((n, d), F32),
        compiler_params=_params("arbitrary"),
        name="outproj",
    )(att, hm, h, wa, wm, g.reshape(1, d), b.reshape(1, d))


def _kvproj_kernel(mem_ref, w_ref, k_ref, v_ref):
    kv = _dot(mem_ref[...].astype(BF16), w_ref[...])
    d = k_ref.shape[-1]
    k_ref[...] = kv[:, :d].astype(BF16)
    v_ref[...] = kv[:, d:].astype(BF16)


def _kvproj(mem2, w_kv):
    n, d = mem2.shape
    w = w_kv.astype(BF16)
    blk = min(ROW_BLOCK, n)
    rows = pl.BlockSpec((blk, d), lambda i: (i, 0))
    return pl.pallas_call(
        _kvproj_kernel,
        grid=(n // blk,),
        in_specs=[rows, _full(w.shape)],
        out_specs=[rows, rows],
        out_shape=[jax.ShapeDtypeStruct((n, d), BF16)] * 2,
        compiler_params=_params("arbitrary"),
        name="kvproj",
    )(mem2, w)


def _xattn_kernel(h_ref, k_ref, v_ref, wq_ref, wo_ref, g_ref, b_ref, o_ref):
    h = h_ref[0]
    d = h.shape[-1]
    dh = d // XATT_HEADS
    q = (_dot(h.astype(BF16), wq_ref[...]) * (dh ** -0.5)).astype(BF16)
    outs = []
    for hd in range(XATT_HEADS):
        sl = slice(hd * dh, (hd + 1) * dh)
        s = _dot_nt(q[:, sl], k_ref[0, :, sl])
        m = jnp.max(s, -1, keepdims=True)
        p = jnp.exp(s - m)
        l = jnp.sum(p, -1, keepdims=True)
        outs.append((_dot(p.astype(BF16), v_ref[0, :, sl]) / l).astype(BF16))
    y = _dot(jnp.concatenate(outs, axis=-1), wo_ref[...])
    o_ref[0] = _layer_norm(DEEPNORM_ALPHA * h + y, g_ref[...], b_ref[...])


def _xattn(h3, k3, v3, w_q, w_o, g, b):
    batch, seq, d = h3.shape
    m = k3.shape[1]
    wq = w_q.astype(BF16)
    wo = w_o.astype(BF16)
    blk = pl.BlockSpec((1, ROW_BLOCK, d), lambda bb, i: (bb, i, 0))
    mem = pl.BlockSpec((1, m, d), lambda bb, i: (bb, 0, 0))
    return pl.pallas_call(
        _xattn_kernel,
        grid=(batch, seq // ROW_BLOCK),
        in_specs=[blk, mem, mem, _full(wq.shape), _full(wo.shape), _full((1, d)), _full((1, d))],
        out_specs=blk,
        out_shape=jax.ShapeDtypeStruct((batch, seq, d), F32),
        compiler_params=_params("arbitrary", "arbitrary"),
        name="xattn",
    )(h3, k3, v3, wq, wo, g.reshape(1, d), b.reshape(1, d))


def _topk_rows(s, k):
    groups = s.shape[0] // SUBLANES
    t = s.shape[1]
    sg = [s[g * SUBLANES:(g + 1) * SUBLANES] for g in range(groups)]
    sub = lax.broadcasted_iota(I32, (SUBLANES, t), 0)
    vals, ids = [], []
    for _ in range(k):
        m8, a8 = sg[0], jnp.zeros((SUBLANES, t), I32)
        for g in range(1, groups):
            upd = sg[g] > m8
            m8 = jnp.where(upd, sg[g], m8)
            a8 = jnp.where(upd, g, a8)
        m = jnp.max(m8, axis=0, keepdims=True)
        kid = jnp.min(jnp.where(m8 == m, sub * groups + a8, SUBLANES * groups), axis=0, keepdims=True)
        vals.append(m)
        ids.append(kid)
        gone = jnp.where(sub * groups + a8 == kid, a8, -1)
        sg = [jnp.where(gone == g, -jnp.inf, sg[g]) for g in range(groups)]
    return vals, ids


def _stack_rows(rows_list):
    k = len(rows_list)
    t = rows_list[0].shape[-1]
    iota = lax.broadcasted_iota(I32, (k, t), 0)
    out = jnp.broadcast_to(rows_list[0], (k, t))
    for r in range(1, k):
        out = jnp.where(iota == r, jnp.broadcast_to(rows_list[r], (k, t)), out)
    return out


def _peer_scores_kernel(h_ref, wq_ref, khi_ref, klo_ref, s_ref):
    q = _dot(h_ref[...].astype(BF16), wq_ref[...])
    for hp in range(2 * PEER_HEADS):
        p = hp % 2
        qhi, qlo = _split2(q[:, hp * N_KEYS:(hp + 1) * N_KEYS])
        s_ref[0, hp] = _dot_nt(khi_ref[p], qhi) + _dot_nt(khi_ref[p], qlo) + _dot_nt(klo_ref[p], qhi)


def _select_experts(sc0, sc1):
    t = sc0.shape[1]
    kk = PEER_TOPK
    half8 = kk // 2
    sub = lax.broadcasted_iota(I32, (half8, t), 0)
    if True:
        tops = []
        for s in (sc0, sc1):
            vals, idxs = _topk_rows(s, kk)
            tops.append((_stack_rows(vals), _stack_rows(idxs)))
        (s0, i0), (s1, i1) = tops
        e0 = i0 * N_KEYS
        cand, cexp, cflat = [], [], []
        for b in range(half8):
            cand.append(s0[:half8] + s1[b:b + 1])
            cexp.append(e0[:half8] + i1[b:b + 1])
            cflat.append(sub * kk + b)
        cand.append(s0[half8:] + s1[0:1])
        cexp.append(e0[half8:] + i1[0:1])
        cflat.append((sub + half8) * kk)
        cand.append(s0[0:1] + s1[half8:])
        cexp.append(e0[0:1] + i1[half8:])
        cflat.append(sub + half8)
        cand = jnp.concatenate(cand, axis=0)
        cexp = jnp.concatenate(cexp, axis=0)
        cflat = jnp.concatenate(cflat, axis=0)
        best_s, best_e = [], []
        for _ in range(kk):
            m = jnp.max(cand, axis=0, keepdims=True)
            jsel = jnp.min(jnp.where(cand == m, cflat, kk * kk), axis=0, keepdims=True)
            hit = cflat == jsel
            best_s.append(m)
            best_e.append(jnp.max(jnp.where(hit, cexp, 0), axis=0, keepdims=True))
            cand = jnp.where(hit, -jnp.inf, cand)
        bs = _stack_rows(best_s)
        ex = jnp.exp(bs - bs[0:1])
        return _stack_rows(best_e), ex / jnp.sum(ex, axis=0, keepdims=True)


def _peer_scores(h2, w_query, sub_keys):
    n, d = h2.shape
    wq = w_query.astype(BF16)
    keys = sub_keys.reshape(2, SUBLANES, N_KEYS // SUBLANES, -1).swapaxes(1, 2).reshape(sub_keys.shape)
    khi = keys.astype(BF16)
    klo = (keys - khi.astype(F32)).astype(BF16)
    blk = (1, 2 * PEER_HEADS, N_KEYS, PEER_BLOCK)
    return pl.pallas_call(
        _peer_scores_kernel,
        grid=(n // PEER_BLOCK,),
        in_specs=[pl.BlockSpec((PEER_BLOCK, d), lambda i: (i, 0)), _full(wq.shape), _full(khi.shape), _full(klo.shape)],
        out_specs=pl.BlockSpec(blk, lambda i: (i, 0, 0, 0)),
        out_shape=jax.ShapeDtypeStruct((n // PEER_BLOCK,) + blk[1:], F32),
        compiler_params=_params("arbitrary"),
        name="peer_scores",
    )(h2, wq, khi, klo)


def _pack_kernel(lo_ref, hi_ref, o_ref):
    rows = lo_ref.shape[0]
    lo = pltpu.bitcast(lo_ref[...].astype(BF16).astype(F32), U32) >> 16
    hi = pltpu.bitcast(hi_ref[...].astype(BF16).astype(F32), U32) & jnp.uint32(0xFFFF0000)
    word = lo | hi
    for s in range(word.shape[1] // LANES):
        o_ref[pl.ds(s, rows, stride=SUBLANES), :] = word[:, s * LANES:(s + 1) * LANES]


def _pack_table(tab):
    e, d = tab.shape
    assert d == SUBLANES * LANES
    half_blocks = e // 2 // PACK_BLOCK
    return pl.pallas_call(
        _pack_kernel,
        grid=(half_blocks,),
        in_specs=[pl.BlockSpec((PACK_BLOCK, d), lambda i: (i, 0)),
                  pl.BlockSpec((PACK_BLOCK, d), lambda i: (i + half_blocks, 0))],
        out_specs=pl.BlockSpec((PACK_BLOCK * SUBLANES, LANES), lambda i: (i, 0)),
        out_shape=jax.ShapeDtypeStruct((e // 2 * SUBLANES, LANES), U32),
        compiler_params=_params("arbitrary"),
        name="pack_table",
    )(tab, tab)


def _table_tile(tab_ref, row8):
    return pltpu.bitcast(tab_ref[pl.ds(pl.multiple_of(row8, SUBLANES), SUBLANES), :], BF16)


def _peer_u_consts():
    k = jnp.arange(2 * LANES)
    col = jnp.arange(2 * PEER_PICKS)
    row = jnp.arange(PEER_PICKS)
    fold = ((k[None, :] // 16 == jnp.arange(2 * U_TILE_PICKS)[:, None] // 2)
            & (k[None, :] % 2 == jnp.arange(2 * U_TILE_PICKS)[:, None] % 2)).astype(BF16)
    q = jnp.arange(LANES)
    r = jnp.arange(SUBLANES)[:, None, None]
    rowq = 8 * ((q % 8) // 2)[None, :, None] + r
    colq = 2 * (32 * (q // 8)[None, :, None] + 2 * (rowq // 2) + (q % 2)[None, :, None]) + rowq % 2
    place = ((col[None, None, :] == colq) & (q < 32)[None, :, None]).astype(BF16)
    dup = ((col[None, :] // 2) == jnp.arange(PEER_PICKS)[:, None]).astype(BF16)
    return fold, place, dup


def _index_copy(row_hbm, idx_smem, sem, batch, slot):
    return pltpu.make_async_copy(row_hbm.at[pl.ds(batch * PEER_UNROLL, PEER_UNROLL)], idx_smem.at[slot], sem.at[slot])


def _for_each_token_batch(row_hbm, idx_smem, sem, block_tokens, body):
    step = pl.program_id(0)
    per_step = block_tokens // PEER_UNROLL
    assert per_step % 2 == 0
    total = pl.num_programs(0) * per_step

    @pl.when(step == 0)
    def _():
        _index_copy(row_hbm, idx_smem, sem, 0, 0).start()

    def pair(k, carry):
        for slot in range(2):
            local = 2 * k + slot
            batch = step * per_step + local
            _index_copy(row_hbm, idx_smem, sem, batch, slot).wait()

            @pl.when(batch + 1 < total)
            def _():
                _index_copy(row_hbm, idx_smem, sem, batch + 1, 1 - slot).start()

            body(local * PEER_UNROLL, idx_smem.at[slot])
        return carry

    lax.fori_loop(0, per_step // 2, pair, 0)


def _peer_u_kernel(sfirst_ref, snext_ref, x_ref, tab_ref, fold_ref, place_ref, dup_ref,
                   coefh_ref, rows_ref, sum_scr, e_scr, g_scr, rows_buf, hbit_buf, gate_buf, idx_smem, sem):
    step = pl.program_id(0)
    cur = step % 2
    tb = x_ref.shape[0]
    per_step = tb // PEER_UNROLL
    assert per_step == PEER_HEADS and per_step % 2 == 0
    ntiles = PEER_PICKS // (2 * U_TILE_PICKS)
    lane = lax.broadcasted_iota(I32, (SUBLANES, LANES), 1)
    sub = lax.broadcasted_iota(I32, (SUBLANES, LANES), 0)

    def select_head(score_ref, head):
        ids, gates = _select_experts(score_ref[0, 2 * head], score_ref[0, 2 * head + 1])
        r0 = pl.multiple_of(head * PEER_TOPK, PEER_TOPK)
        e_scr[pl.ds(r0, PEER_TOPK), :] = ids
        g_scr[pl.ds(r0, PEER_TOPK), :] = gates

    def publish(buf):
        e = e_scr[...]
        rows_buf[buf] = ((e & (PEER_HALF_EXPERTS - 1)) * SUBLANES).T
        hbit_buf[buf] = (e >> (PEER_HALF_EXPERTS.bit_length() - 1)).astype(F32).T
        gate_buf[buf] = g_scr[...].T

    def index_copy(buf, batch, slot):
        return pltpu.make_async_copy(rows_buf.at[buf, pl.ds(batch * PEER_UNROLL, PEER_UNROLL)], idx_smem.at[slot],
                                     sem.at[slot])

    @pl.when(step == 0)
    def _():
        for head in range(PEER_HEADS):
            select_head(sfirst_ref, head)
        publish(0)
        index_copy(0, 0, 0).start()

    def tokens(t0, idx):
        by_row = None
        for u in range(PEER_UNROLL):
            t = t0 + u
            x = x_ref[t]
            xbits = pltpu.bitcast(x.astype(BF16).astype(F32), U32)
            xw = pltpu.bitcast(xbits | (xbits >> 16), BF16)
            sums = jnp.zeros((SUBLANES, LANES), F32)
            for j in range(ntiles):
                cols = []
                for ab in range(2):
                    prods = []
                    for mm in range(U_TILE_PICKS):
                        c = j * 2 * U_TILE_PICKS + 2 * mm + ab
                        prods.append(_table_tile(tab_ref, idx[u, c]) * xw)
                    cols.append(jnp.concatenate(prods, axis=0))
                res = _dot(fold_ref[...], jnp.concatenate(cols, axis=1))
                for rb in range(2 * U_TILE_PICKS // SUBLANES):
                    for ab in range(2):
                        part = res[rb * SUBLANES:(rb + 1) * SUBLANES, ab * LANES:(ab + 1) * LANES]
                        q = j * 8 + rb * 2 + ab
                        sums = jnp.where(lane == q, jnp.sum(part, axis=1, keepdims=True), sums)
            if u % SUBLANES == 0:
                by_row = [jnp.zeros((SUBLANES, LANES), F32)] * SUBLANES
            by_row = [jnp.where(sub == u % SUBLANES, jnp.broadcast_to(sums[r:r + 1], (SUBLANES, LANES)), by_row[r])
                      for r in range(SUBLANES)]
            if u % SUBLANES == SUBLANES - 1:
                first = pl.multiple_of(t0 + u - (SUBLANES - 1), SUBLANES)
                for r in range(SUBLANES):
                    sum_scr[r, pl.ds(first, SUBLANES), :] = by_row[r]

    def pair(k, carry):
        for slot in range(2):
            batch = 2 * k + slot
            index_copy(cur, batch, slot).wait()

            @pl.when(batch + 1 < per_step)
            def _():
                index_copy(cur, batch + 1, 1 - slot).start()

            tokens(batch * PEER_UNROLL, idx_smem.at[slot])
            select_head(snext_ref, batch)
        return carry

    lax.fori_loop(0, per_step // 2, pair, 0)
    publish(1 - cur)

    @pl.when(step + 1 < pl.num_programs(0))
    def _():
        index_copy(1 - cur, 0, 0).start()

    rows_ref[...] = rows_buf[cur]
    a2 = jnp.zeros((tb, 2 * PEER_PICKS), F32)
    for r in range(SUBLANES):
        shi, slo = _split2(sum_scr[r])
        a2 = a2 + _dot(shi, place_ref[r]) + _dot(slo, place_ref[r])
    ghi, glo = _split2(gate_buf[cur])
    g2 = _dot(ghi, dup_ref[...]) + _dot(glo, dup_ref[...])
    hbit2 = _dot(hbit_buf[cur].astype(BF16), dup_ref[...])
    parity = (lax.broadcasted_iota(I32, a2.shape, 1) % 2).astype(F32)
    coef = g2 * (0.5 * a2 * (1.0 + lax.erf(a2 * math.sqrt(0.5))))
    coefh_ref[...] = jnp.where(hbit2 == parity, coef, 0.0).astype(BF16)


def _peer_v_consts():
    col = jnp.arange(2 * PEER_PICKS * SUBLANES)
    ab, p, r = col // (PEER_PICKS * SUBLANES), (col // 16) % (PEER_PICKS // 2), col % 16
    src = jnp.arange(2 * PEER_PICKS)
    expand = ((src[:, None] // 2 == (2 * p + ab)[None, :]) & (src[:, None] % 2 == (r % 2)[None, :])).astype(BF16)
    kk = jnp.arange(PEER_PICKS * SUBLANES)
    diag = ((kk[None, :] % 16) // 2 == (jnp.arange(16) % SUBLANES)[:, None]).astype(F32)
    return expand, diag


def _peer_v_kernel(row_hbm, coefh_ref, x_ref, tab_ref, expand_ref, diag_ref, g_ref, b_ref, o_ref, lrow_scr, y_scr,
                   idx_smem, sem):
    tb = x_ref.shape[0]
    kdim = PEER_PICKS * SUBLANES
    lrow_scr[...] = _dot(coefh_ref[...], expand_ref[...])
    diag = diag_ref[...]
    sub = lax.broadcasted_iota(I32, (SUBLANES, LANES), 0)

    def tokens(t0, idx):
        by_row = None
        for u in range(PEER_UNROLL):
            t = t0 + u
            even = jnp.broadcast_to(lrow_scr[pl.ds(t, 1), 0:kdim], (SUBLANES, kdim))
            odd = jnp.broadcast_to(lrow_scr[pl.ds(t, 1), kdim:2 * kdim], (SUBLANES, kdim))
            lhs = (jnp.concatenate([even, odd], axis=0) * diag).astype(BF16)
            tiles = []
            for p in range(PEER_PICKS // 2):
                wa = _table_tile(tab_ref, idx[u, 2 * p])
                wb = _table_tile(tab_ref, idx[u, 2 * p + 1])
                tiles.append(jnp.concatenate([wa, wb], axis=1))
            res = _dot(lhs, jnp.concatenate(tiles, axis=0))
            y = res[0:SUBLANES, 0:LANES] + res[SUBLANES:, LANES:]
            if u % SUBLANES == 0:
                by_row = [jnp.zeros((SUBLANES, LANES), F32)] * SUBLANES
            by_row = [jnp.where(sub == u % SUBLANES, jnp.broadcast_to(y[r:r + 1], (SUBLANES, LANES)), by_row[r])
                      for r in range(SUBLANES)]
            if u % SUBLANES == SUBLANES - 1:
                first = pl.multiple_of(t0 + u - (SUBLANES - 1), SUBLANES)
                for r in range(SUBLANES):
                    y_scr[r, pl.ds(first, SUBLANES), :] = by_row[r]

    _for_each_token_batch(row_hbm, idx_smem, sem, tb, tokens)
    z = [DEEPNORM_ALPHA * x_ref[:, r * LANES:(r + 1) * LANES] + y_scr[r] for r in range(SUBLANES)]
    cnt = SUBLANES * LANES
    mu = sum(jnp.sum(zr, axis=1, keepdims=True) for zr in z) / cnt
    zc = [zr - mu for zr in z]
    var = sum(jnp.sum(c * c, axis=1, keepdims=True) for c in zc) / cnt
    rstd = lax.rsqrt(var + LN_EPS)
    for r in range(SUBLANES):
        o_ref[:, r * LANES:(r + 1) * LANES] = zc[r] * rstd * g_ref[r:r + 1, :] + b_ref[r:r + 1, :]


def _peer_ffn(h2, scores, expert_u, expert_v, ln_g, ln_b):
    n, d = h2.shape
    rows8 = d // LANES
    assert rows8 == SUBLANES and PEER_PICKS == LANES and expert_u.shape[0] == 2 * PEER_HALF_EXPERTS
    x3 = h2.reshape(n, rows8, LANES)
    u_pk = _pack_table(expert_u)
    v_pk = _pack_table(expert_v)
    hbm = pl.BlockSpec(memory_space=pl.ANY)
    staging = [pltpu.SMEM((2, PEER_UNROLL, PEER_PICKS), I32), pltpu.SemaphoreType.DMA((2,))]
    picks =pl.BlockSpec((PEER_BLOCK, PEER_PICKS), lambda i: (i, 0))
    picks2 = pl.BlockSpec((PEER_BLOCK, 2 * PEER_PICKS), lambda i: (i, 0))
    tok = pl.BlockSpec((PEER_BLOCK, rows8, LANES), lambda i: (i, 0, 0))
    flat = pl.BlockSpec((PEER_BLOCK, d), lambda i: (i, 0))
    table = pl.BlockSpec(u_pk.shape, lambda i: (0, 0), pipeline_mode=pl.Buffered(1))
    u_consts = _peer_u_consts()
    nblocks = n // PEER_BLOCK
    sblk = (1,) + scores.shape[1:]
    coefh, row = pl.pallas_call(
        _peer_u_kernel,
        grid=(nblocks,),
        in_specs=[pl.BlockSpec(sblk, lambda i: (0, 0, 0, 0)),
                  pl.BlockSpec(sblk, lambda i: (jnp.minimum(i + 1, nblocks - 1), 0, 0, 0)),
                  tok, table] + [_full(c.shape) for c in u_consts],
        out_specs=[picks2, picks],
        out_shape=[jax.ShapeDtypeStruct((n, 2 * PEER_PICKS), BF16), jax.ShapeDtypeStruct((n, PEER_PICKS), I32)],
        scratch_shapes=[pltpu.VMEM((SUBLANES, PEER_BLOCK, LANES), F32),
                        pltpu.VMEM((PEER_PICKS, PEER_BLOCK), I32), pltpu.VMEM((PEER_PICKS, PEER_BLOCK), F32),
                        pltpu.VMEM((2, PEER_BLOCK, PEER_PICKS), I32), pltpu.VMEM((2, PEER_BLOCK, PEER_PICKS), F32),
                        pltpu.VMEM((2, PEER_BLOCK, PEER_PICKS), F32)] + staging,
        compiler_params=_params("arbitrary"),
        name="peer_u",
    )(scores, scores, x3, u_pk, *u_consts)
    v_consts = _peer_v_consts()
    out = pl.pallas_call(
        _peer_v_kernel,
        grid=(n // PEER_BLOCK,),
        in_specs=[hbm, picks2, flat, table] + [_full(c.shape) for c in v_consts]
        + [_full((rows8, LANES)), _full((rows8, LANES))],
        out_specs=flat,
        out_shape=jax.ShapeDtypeStruct((n, d), F32),
        scratch_shapes=[pltpu.VMEM((PEER_BLOCK, 2 * PEER_PICKS * SUBLANES), F32),
                        pltpu.VMEM((rows8, PEER_BLOCK, LANES), F32)] + staging,
        compiler_params=_params("arbitrary"),
        name="peer_v",
    )(row, coefh, h2, v_pk, *v_consts, ln_g.reshape(rows8, LANES), ln_b.reshape(rows8, LANES))
    return out


def kernel(x, mem, ln_in_g, ln_in_b, w_in, conv_w, conv_b, mlstm_i_bias, mlstm_f_bias, mlstm_norm_g, rel_bias, w_out, ln1_g, ln1_b, xattn_w_q, xattn_w_kv, xattn_w_o, ln2_g, ln2_b, peer_w_query, peer_sub_keys, peer_u, peer_v, ln3_g, ln3_b):
    batch, seq, d = x.shape
    n = batch * seq
    assert w_in.shape[0] == DEPTH
    h, qkv, mqk, mv, mo, gates = _ln_inproj(x.reshape(n, d), ln_in_g, ln_in_b, w_in[0])
    for l in range(DEPTH):
        if l > 0:
            raise NotImplementedError("input projection of deeper layers")
        att = _attention(qkv, rel_bias[l], batch, seq)
        hm = _mlstm(mqk, mv, mo, gates, conv_w[l], conv_b[l], mlstm_i_bias[l], mlstm_f_bias[l], mlstm_norm_g[l],
                    batch, seq)
        h = _outproj(att.reshape(n, ATT_WIDTH), hm.reshape(n, MLSTM_WIDTH), h, w_out[l], ln1_g[l], ln1_b[l])
        k2, v2 = _kvproj(mem.reshape(-1, d), xattn_w_kv[l])
        m = mem.shape[1]
        h = _xattn(h.reshape(batch, seq, d), k2.reshape(batch, m, d), v2.reshape(batch, m, d),
                   xattn_w_q[l], xattn_w_o[l], ln2_g[l], ln2_b[l]).reshape(n, d)
        scores = _peer_scores(h, peer_w_query[l], peer_sub_keys[l])
        h = _peer_ffn(h, scores, peer_u[l], peer_v[l], ln3_g[l], ln3_b[l])
    return h.reshape(batch, seq, d)
```

```python
import math

import jax
import jax.numpy as jnp
from jax import lax
from jax.experimental import pallas as pl
from jax.experimental.pallas import tpu as pltpu

F32, BF16, I32, U32 = jnp.float32, jnp.bfloat16, jnp.int32, jnp.uint32

DEPTH = 1
LN_EPS = 1e-5
DEEPNORM_ALPHA = (2.0 * DEPTH) ** 0.25
CHUNK = 64
ATT_HEADS, ATT_HEAD_DIM, ATT_LEFT_CHUNKS, MAX_REL = 8, 64, 8, 128
ATT_WIDTH = ATT_HEADS * ATT_HEAD_DIM
MLSTM_HEADS, MLSTM_HEAD_DIM, CONV_WIDTH = 4, 128, 4
MLSTM_WIDTH = MLSTM_HEADS * MLSTM_HEAD_DIM
XATT_HEADS = 4
PEER_HEADS, N_KEYS, PEER_TOPK = 8, 128, 16
PEER_PICKS = PEER_HEADS * PEER_TOPK
PEER_HALF_EXPERTS = N_KEYS * N_KEYS // 2
U_TILE_PICKS = 16
PEER_UNROLL = 16

LANES = 128
SUBLANES = 8
VMEM_LIMIT_BYTES = 56 * 1024 * 1024

NEG_BIG = -1e30

ROW_BLOCK = 512
ATT_BLOCK = 256
ATT_KEY_BLOCKS = 3
MLSTM_BLOCK = 256
MLSTM_SEQS = 1
PEER_BLOCK = 128
PACK_BLOCK = 256


def _params(*semantics):
    return pltpu.CompilerParams(dimension_semantics=semantics, vmem_limit_bytes=VMEM_LIMIT_BYTES)


def _layer_norm(x, g, b):
    mu = jnp.mean(x, -1, keepdims=True)
    xc = x - mu
    var = jnp.mean(xc * xc, -1, keepdims=True)
    return xc * lax.rsqrt(var + LN_EPS) * g + b


def _split2(x):
    hi = x.astype(BF16)
    lo = (x - hi.astype(F32)).astype(BF16)
    return hi, lo


def _split3(x):
    hi = x.astype(BF16)
    r = x - hi.astype(F32)
    mid = r.astype(BF16)
    lo = (r - mid.astype(F32)).astype(BF16)
    return hi, mid, lo


def _dot(a, b):
    return jnp.dot(a, b, preferred_element_type=F32)


def _dot_nt(a, b):
    return lax.dot_general(a, b, (((1,), (1,)), ((), ())), preferred_element_type=F32)


def _dot_tn(a, b):
    return lax.dot_general(a, b, (((0,), (0,)), ((), ())), preferred_element_type=F32)


def _full(shape):
    return pl.BlockSpec(shape, lambda *_: (0,) * len(shape))


def _ln_inproj_kernel(x_ref, g_ref, b_ref, wqkv_ref, wmqk_ref, wmv_ref, wmo_ref, wghi_ref, wglo_ref,
                      h_ref, qkv_ref, mqk_ref, mv_ref, mo_ref, gate_ref):
    h = _layer_norm(x_ref[...], g_ref[...], b_ref[...])
    h_ref[...] = h
    hb, hlo = _split2(h)
    qkv_ref[...] = _dot(hb, wqkv_ref[...]).astype(BF16)
    mqk_ref[...] = _dot(hb, wmqk_ref[...]).astype(BF16)
    mv_ref[...] = _dot(hb, wmv_ref[...]).astype(BF16)
    mo_ref[...] = _dot(hb, wmo_ref[...]).astype(BF16)
    gate_ref[...] = _dot(hb, wghi_ref[...]) + _dot(hlo, wghi_ref[...]) + _dot(hb, wglo_ref[...])


def _ln_inproj(x2, g, b, w_in):
    n, d = x2.shape
    a3 = 3 * ATT_WIDTH
    wqkv = w_in[:, :a3].astype(BF16)
    wmqk = w_in[:, a3:a3 + 2 * MLSTM_WIDTH].astype(BF16)
    wmv = w_in[:, a3 + 2 * MLSTM_WIDTH:a3 + 3 * MLSTM_WIDTH].astype(BF16)
    wmo = w_in[:, a3 + 3 * MLSTM_WIDTH:a3 + 4 * MLSTM_WIDTH].astype(BF16)
    wg = jnp.pad(w_in[:, a3 + 4 * MLSTM_WIDTH:], ((0, 0), (0, LANES - 2 * MLSTM_HEADS)))
    wghi = wg.astype(BF16)
    wglo = (wg - wghi.astype(F32)).astype(BF16)
    rows = lambda w: pl.BlockSpec((ROW_BLOCK, w), lambda i: (i, 0))
    return pl.pallas_call(
        _ln_inproj_kernel,
        grid=(n // ROW_BLOCK,),
        in_specs=[rows(d), _full((1, d)), _full((1, d)), _full(wqkv.shape), _full(wmqk.shape),
                  _full(wmv.shape), _full(wmo.shape), _full(wghi.shape), _full(wglo.shape)],
        out_specs=[rows(d), rows(a3), rows(2 * MLSTM_WIDTH), rows(MLSTM_WIDTH), rows(MLSTM_WIDTH), rows(LANES)],
        out_shape=[jax.ShapeDtypeStruct((n, d), F32), jax.ShapeDtypeStruct((n, a3), BF16),
                   jax.ShapeDtypeStruct((n, 2 * MLSTM_WIDTH), BF16), jax.ShapeDtypeStruct((n, MLSTM_WIDTH), BF16),
                   jax.ShapeDtypeStruct((n, MLSTM_WIDTH), BF16), jax.ShapeDtypeStruct((n, LANES), F32)],
        compiler_params=_params("arbitrary"),
        name="ln_inproj",
    )(x2, g.reshape(1, d), b.reshape(1, d), wqkv, wmqk, wmv, wmo, wghi, wglo)


def _attn_kernel(q_ref, k0_ref, k1_ref, k2_ref, v0_ref, v1_ref, v2_ref, bias_ref, o_ref):
    i = pl.program_id(1)
    nkeys = ATT_KEY_BLOCKS * ATT_BLOCK
    q = q_ref[0] * (ATT_HEAD_DIM ** -0.5)
    kcat = jnp.concatenate([k0_ref[0], k1_ref[0], k2_ref[0]], axis=0)
    vcat = jnp.concatenate([v0_ref[0], v1_ref[0], v2_ref[0]], axis=0)
    col = lax.broadcasted_iota(I32, (1, nkeys), 1)
    in_seq = col >= (ATT_KEY_BLOCKS - 1 - i) * ATT_BLOCK
    lane = lax.broadcasted_iota(I32, (1, LANES), 1)
    low = lane < ATT_HEAD_DIM
    outs = []
    for pair in range(ATT_HEADS // 2):
        sl = slice(pair * LANES, (pair + 1) * LANES)
        qp, kp, vp = q[:, sl], kcat[:, sl], vcat[:, sl]
        halves = []
        for half in range(2):
            keep = low if half == 0 else jnp.logical_not(low)
            qh = jnp.where(keep, qp, jnp.zeros_like(qp))
            s = _dot_nt(qh, kp) + bias_ref[2 * pair + half]
            s = jnp.where(in_seq, s, NEG_BIG)
            m = jnp.max(s, -1, keepdims=True)
            p = jnp.exp(s - m)
            l = jnp.sum(p, -1, keepdims=True)
            halves.append(_dot(p.astype(BF16), vp) / l)
        outs.append(jnp.where(low, halves[0], halves[1]))
    o_ref[0] = jnp.concatenate(outs, axis=-1).astype(BF16)


def _attn_bias(rel_bias):
    nq, nk = ATT_BLOCK, ATT_KEY_BLOCKS * ATT_BLOCK
    r = jnp.arange(nq)[:, None]
    j = jnp.arange(nk)[None, :]
    allowed = (j // CHUNK >= r // CHUNK) & (j // CHUNK <= r // CHUNK + ATT_LEFT_CHUNKS)
    period = nq + nk - 1
    i = jnp.arange(period)
    d = jnp.where(i < nk, i, i - period)
    rel = d - (ATT_KEY_BLOCKS - 1) * ATT_BLOCK
    line = rel_bias[:, jnp.clip(rel, -MAX_REL, MAX_REL) + MAX_REL].astype(F32)
    rep = jnp.tile(line, (1, nq + 1))[:, :nq * (period - 1)]
    tab = rep.reshape(-1, nq, period - 1)[:, :, :nk]
    return jnp.where(allowed[None], tab, NEG_BIG)


def _attention(qkv, rel_bias, batch, seq):
    assert ATT_LEFT_CHUNKS * CHUNK == (ATT_KEY_BLOCKS - 1) * ATT_BLOCK
    qkv3 = qkv.reshape(batch, seq, 3 * ATT_WIDTH)
    bias = _attn_bias(rel_bias)
    blk = (1, ATT_BLOCK, ATT_WIDTH)

    def kv_spec(col, j):
        return pl.BlockSpec(blk, lambda b, i: (b, jnp.maximum(i - (ATT_KEY_BLOCKS - 1) + j, 0), col))

    return pl.pallas_call(
        _attn_kernel,
        grid=(batch, seq // ATT_BLOCK),
        in_specs=[pl.BlockSpec(blk, lambda b, i: (b, i, 0))]
        + [kv_spec(1, j) for j in range(ATT_KEY_BLOCKS)] + [kv_spec(2, j) for j in range(ATT_KEY_BLOCKS)]
        + [_full(bias.shape)],
        out_specs=pl.BlockSpec(blk, lambda b, i: (b, i, 0)),
        out_shape=jax.ShapeDtypeStruct((batch, seq, ATT_WIDTH), BF16),
        compiler_params=_params("arbitrary", "arbitrary"),
        name="attn",
    )(qkv3, qkv3, qkv3, qkv3, qkv3, qkv3, qkv3, bias)


def _mlstm_kernel(mqk_ref, mv_ref, mo_ref, gate_ref, convw_ref, convb_ref, gbias_ref, ng_ref, out_ref,
                  xpad, q_scr, k_scr, gate_scr, logf_scr, c_st, n_st, m_st):
    j = pl.program_id(1)
    rows = MLSTM_BLOCK
    d = MLSTM_HEAD_DIM

    @pl.when(j == 0)
    def _():
        xpad[:, 0:SUBLANES, :] = jnp.zeros((MLSTM_SEQS, SUBLANES, 2 * MLSTM_WIDTH), F32)
        c_st[...] = jnp.zeros_like(c_st)
        n_st[...] = jnp.zeros_like(n_st)
        m_st[...] = jnp.zeros_like(m_st)

    for bb in range(MLSTM_SEQS):
        xpad[bb, SUBLANES:SUBLANES + rows, :] = mqk_ref[bb].astype(F32)
        acc = jnp.broadcast_to(convb_ref[...], (rows, 2 * MLSTM_WIDTH))
        for t in range(CONV_WIDTH):
            acc = acc + convw_ref[t:t + 1, :] * xpad[bb, pl.ds(SUBLANES - (CONV_WIDTH - 1) + t, rows), :]
        xpad[bb, 0:SUBLANES, :] = xpad[bb, rows:rows + SUBLANES, :]
        qk = acc * jax.nn.sigmoid(acc)
        q_scr[bb] = qk[:, :MLSTM_WIDTH].astype(BF16)
        k_scr[bb] = qk[:, MLSTM_WIDTH:] * (d ** -0.5)

        gates = gate_ref[bb] + gbias_ref[...]
        gate_scr[bb] = gates
        logf_scr[bb] = jax.nn.log_sigmoid(gates)

    ri = lax.broadcasted_iota(I32, (CHUNK, CHUNK), 0)
    ci = lax.broadcasted_iota(I32, (CHUNK, CHUNK), 1)
    causal = ci <= ri
    tri = causal.astype(BF16)

    def chunk_body(c, carry):
        for bb in range(MLSTM_SEQS):
            _mlstm_chunk(bb, pl.multiple_of(c * CHUNK, CHUNK), causal, tri, mv_ref, mo_ref, ng_ref, out_ref,
                         q_scr, k_scr, gate_scr, logf_scr, c_st, n_st, m_st)
        return carry

    lax.fori_loop(0, rows // CHUNK, chunk_body, 0)


def _mlstm_chunk(bb, r0, causal, tri, mv_ref, mo_ref, ng_ref, out_ref, q_scr, k_scr, gate_scr, logf_scr,
                 c_st, n_st, m_st):
    d = MLSTM_HEAD_DIM
    g = gate_scr[bb, pl.ds(r0, CHUNK), :]
    lf = logf_scr[bb, pl.ds(r0, CHUNK), :]
    b3 = _dot(tri, jnp.concatenate(_split3(lf), axis=1))
    bcol = b3[:, :LANES] + b3[:, LANES:2 * LANES] + b3[:, 2 * LANES:]
    g_t = g.T
    b_t = bcol.T
    m_all = m_st[bb]
    n_all = n_st[bb]
    for h in range(MLSTM_HEADS):
        hs = slice(h * d, (h + 1) * d)
        ic = g[:, h:h + 1]
        bc = bcol[:, MLSTM_HEADS + h:MLSTM_HEADS + h + 1]
        ir = g_t[h:h + 1, :]
        br = b_t[MLSTM_HEADS + h:MLSTM_HEADS + h + 1, :]
        m_prev = m_all[h:h + 1, 0:1]
        n_prev = n_all[h:h + 1, :]
        c_prev = c_st[bb, h]
        qh = q_scr[bb, pl.ds(r0, CHUNK), hs]
        kh = k_scr[bb, pl.ds(r0, CHUNK), hs]
        vh = mv_ref[bb, pl.ds(r0, CHUNK), hs]

        log_d = jnp.where(causal, bc - br + ir, NEG_BIG)
        inter = bc + m_prev
        m_t = jnp.maximum(inter, jnp.max(log_d, -1, keepdims=True))
        d_mat = jnp.exp(log_d - m_t)
        w_inter = jnp.exp(inter - m_t)
        qk_d = _dot_nt(qh, kh.astype(BF16)) * d_mat
        lhs = jnp.concatenate([(w_inter * qh.astype(F32)).astype(BF16), qk_d.astype(BF16)], axis=1)
        num = _dot(lhs, jnp.concatenate([c_prev.astype(BF16), vh], axis=0))
        den = (w_inter * jnp.sum(qh.astype(F32) * n_prev, -1, keepdims=True)
               + jnp.sum(qk_d, -1, keepdims=True))
        hh = num / jnp.maximum(jnp.abs(den), jnp.exp(-m_t))

        b_last = bc[CHUNK - 1:CHUNK, :]
        log_in = b_last - bc + ic
        m_new = jnp.maximum(b_last + m_prev, jnp.max(log_in, 0, keepdims=True))
        w_prev = jnp.exp(b_last + m_prev - m_new)
        kw = kh * jnp.exp(log_in - m_new)
        c_st[bb, h] = w_prev * c_prev + _dot_tn(kw.astype(BF16), vh)
        n_st[bb, h:h + 1, :] = w_prev * n_prev + jnp.sum(kw, 0, keepdims=True)
        m_st[bb, h:h + 1, :] = jnp.broadcast_to(m_new, (1, LANES))

        mu = jnp.mean(hh, -1, keepdims=True)
        hc = hh - mu
        var = jnp.mean(hc * hc, -1, keepdims=True)
        hn = hc * lax.rsqrt(var + LN_EPS) * ng_ref[:, hs]
        og = jax.nn.sigmoid(mo_ref[bb, pl.ds(r0, CHUNK), hs].astype(F32))
        out_ref[bb, pl.ds(r0, CHUNK), hs] = (og * hn).astype(BF16)


def _mlstm(mqk, mv, mo, gates, conv_w, conv_b, i_bias, f_bias, norm_g, batch, seq):
    w2 = 2 * MLSTM_WIDTH
    gbias = jnp.pad(jnp.concatenate([i_bias, f_bias]).astype(F32), (0, LANES - 2 * MLSTM_HEADS)).reshape(1, LANES)
    ns = MLSTM_SEQS
    assert batch % ns == 0
    blk = lambda w: pl.BlockSpec((ns, MLSTM_BLOCK, w), lambda b, i: (b, i, 0))
    return pl.pallas_call(
        _mlstm_kernel,
        grid=(batch // ns, seq // MLSTM_BLOCK),
        in_specs=[blk(w2), blk(MLSTM_WIDTH), blk(MLSTM_WIDTH), blk(LANES),
                  _full((CONV_WIDTH, w2)), _full((1, w2)), _full((1, LANES)), _full((1, MLSTM_WIDTH))],
        out_specs=blk(MLSTM_WIDTH),
        out_shape=jax.ShapeDtypeStruct((batch, seq, MLSTM_WIDTH), BF16),
        scratch_shapes=[pltpu.VMEM((ns, MLSTM_BLOCK + SUBLANES, w2), F32),
                        pltpu.VMEM((ns, MLSTM_BLOCK, MLSTM_WIDTH), BF16),
                        pltpu.VMEM((ns, MLSTM_BLOCK, MLSTM_WIDTH), F32),
                        pltpu.VMEM((ns, MLSTM_BLOCK, LANES), F32),
                        pltpu.VMEM((ns, MLSTM_BLOCK, LANES), F32),
                        pltpu.VMEM((ns, MLSTM_HEADS, MLSTM_HEAD_DIM, MLSTM_HEAD_DIM), F32),
                        pltpu.VMEM((ns, SUBLANES, MLSTM_HEAD_DIM), F32),
                        pltpu.VMEM((ns, SUBLANES, LANES), F32)],
        compiler_params=_params("arbitrary", "arbitrary"),
        name="mlstm",
    )(mqk.reshape(batch, seq, w2), mv.reshape(batch, seq, MLSTM_WIDTH), mo.reshape(batch, seq, MLSTM_WIDTH),
      gates.reshape(batch, seq, LANES), conv_w.astype(F32), conv_b.reshape(1, w2).astype(F32), gbias,
      norm_g.reshape(1, MLSTM_WIDTH).astype(F32))


def _outproj_kernel(att_ref, hm_ref, h_ref, wa_ref, wm_ref, g_ref, b_ref, o_ref):
    y = _dot(att_ref[...], wa_ref[...]) + _dot(hm_ref[...], wm_ref[...])
    o_ref[...] = _layer_norm(DEEPNORM_ALPHA * h_ref[...] + y, g_ref[...], b_ref[...])


def _outproj(att, hm, h, w_out, g, b):
    n, d = h.shape
    wa = w_out[:ATT_WIDTH].astype(BF16)
    wm = w_out[ATT_WIDTH:].astype(BF16)
    rows = lambda w: pl.BlockSpec((ROW_BLOCK, w), lambda i: (i, 0))
    return pl.pallas_call(
        _outproj_kernel,
        grid=(n // ROW_BLOCK,),
        in_specs=[rows(ATT_WIDTH), rows(MLSTM_WIDTH), rows(d), _full(wa.shape), _full(wm.shape),
                  _full((1, d)), _full((1, d))],
        out_specs=rows(d),
        out_shape=jax.ShapeDtypeStruct((n, d), F32),
        compiler_params=_params("arbitrary"),
        name="outproj",
    )(att, hm, h, wa, wm, g.reshape(1, d), b.reshape(1, d))


def _kvproj_kernel(mem_ref, w_ref, k_ref, v_ref):
    kv = _dot(mem_ref[...].astype(BF16), w_ref[...])
    d = k_ref.shape[-1]
    k_ref[...] = kv[:, :d].astype(BF16)
    v_ref[...] = kv[:, d:].astype(BF16)


def _kvproj(mem2, w_kv):
    n, d = mem2.shape
    w = w_kv.astype(BF16)
    blk = min(ROW_BLOCK, n)
    rows = pl.BlockSpec((blk, d), lambda i: (i, 0))
    return pl.pallas_call(
        _kvproj_kernel,
        grid=(n // blk,),
        in_specs=[rows, _full(w.shape)],
        out_specs=[rows, rows],
        out_shape=[jax.ShapeDtypeStruct((n, d), BF16)] * 2,
        compiler_params=_params("arbitrary"),
        name="kvproj",
    )(mem2, w)


def _xattn_kernel(h_ref, k_ref, v_ref, wq_ref, wo_ref, g_ref, b_ref, o_ref):
    h = h_ref[0]
    d = h.shape[-1]
    dh = d // XATT_HEADS
    q = (_dot(h.astype(BF16), wq_ref[...]) * (dh ** -0.5)).astype(BF16)
    outs = []
    for hd in range(XATT_HEADS):
        sl = slice(hd * dh, (hd + 1) * dh)
        s = _dot_nt(q[:, sl], k_ref[0, :, sl])
        m = jnp.max(s, -1, keepdims=True)
        p = jnp.exp(s - m)
        l = jnp.sum(p, -1, keepdims=True)
        outs.append((_dot(p.astype(BF16), v_ref[0, :, sl]) / l).astype(BF16))
    y = _dot(jnp.concatenate(outs, axis=-1), wo_ref[...])
    o_ref[0] = _layer_norm(DEEPNORM_ALPHA * h + y, g_ref[...], b_ref[...])


def _xattn(h3, k3, v3, w_q, w_o, g, b):
    batch, seq, d = h3.shape
    m = k3.shape[1]
    wq = w_q.astype(BF16)
    wo = w_o.astype(BF16)
    blk = pl.BlockSpec((1, ROW_BLOCK, d), lambda bb, i: (bb, i, 0))
    mem = pl.BlockSpec((1, m, d), lambda bb, i: (bb, 0, 0))
    return pl.pallas_call(
        _xattn_kernel,
        grid=(batch, seq // ROW_BLOCK),
        in_specs=[blk, mem, mem, _full(wq.shape), _full(wo.shape), _full((1, d)), _full((1, d))],
        out_specs=blk,
        out_shape=jax.ShapeDtypeStruct((batch, seq, d), F32),
        compiler_params=_params("arbitrary", "arbitrary"),
        name="xattn",
    )(h3, k3, v3, wq, wo, g.reshape(1, d), b.reshape(1, d))


def _topk_rows(s, k):
    groups = s.shape[0] // SUBLANES
    t = s.shape[1]
    sg = [s[g * SUBLANES:(g + 1) * SUBLANES] for g in range(groups)]
    sub = lax.broadcasted_iota(I32, (SUBLANES, t), 0)
    vals, ids = [], []
    for _ in range(k):
        m8, a8 = sg[0], jnp.zeros((SUBLANES, t), I32)
        for g in range(1, groups):
            upd = sg[g] > m8
            m8 = jnp.where(upd, sg[g], m8)
            a8 = jnp.where(upd, g, a8)
        m = jnp.max(m8, axis=0, keepdims=True)
        kid = jnp.min(jnp.where(m8 == m, sub * groups + a8, SUBLANES * groups), axis=0, keepdims=True)
        vals.append(m)
        ids.append(kid)
        gone = jnp.where(sub * groups + a8 == kid, a8, -1)
        sg = [jnp.where(gone == g, -jnp.inf, sg[g]) for g in range(groups)]
    return vals, ids


def _stack_rows(rows_list):
    k = len(rows_list)
    t = rows_list[0].shape[-1]
    iota = lax.broadcasted_iota(I32, (k, t), 0)
    out = jnp.broadcast_to(rows_list[0], (k, t))
    for r in range(1, k):
        out = jnp.where(iota == r, jnp.broadcast_to(rows_list[r], (k, t)), out)
    return out


def _peer_scores_kernel(h_ref, wq_ref, khi_ref, klo_ref, s_ref):
    q = _dot(h_ref[...].astype(BF16), wq_ref[...])
    for b in range(s_ref.shape[0]):
        for hp in range(2 * PEER_HEADS):
            p = hp % 2
            qhi, qlo = _split2(q[b * PEER_BLOCK:(b + 1) * PEER_BLOCK, hp * N_KEYS:(hp + 1) * N_KEYS])
            s_ref[b, hp] = _dot_nt(khi_ref[p], qhi) + _dot_nt(khi_ref[p], qlo) + _dot_nt(klo_ref[p], qhi)


def _select_experts(sc0, sc1):
    t = sc0.shape[1]
    kk = PEER_TOPK
    half8 = kk // 2
    sub = lax.broadcasted_iota(I32, (half8, t), 0)
    tops = []
    for s in (sc0, sc1):
        vals, idxs = _topk_rows(s, kk)
        tops.append((_stack_rows(vals), _stack_rows(idxs)))
    (s0, i0), (s1, i1) = tops
    e0 = i0 * N_KEYS
    cand, cexp, cflat = [], [], []
    for b in range(half8):
        cand.append(s0[:half8] + s1[b:b + 1])
        cexp.append(e0[:half8] + i1[b:b + 1])
        cflat.append(sub * kk + b)
    cand.append(s0[half8:] + s1[0:1])
    cexp.append(e0[half8:] + i1[0:1])
    cflat.append((sub + half8) * kk)
    cand.append(s0[0:1] + s1[half8:])
    cexp.append(e0[0:1] + i1[half8:])
    cflat.append(sub + half8)
    cand = jnp.concatenate(cand, axis=0)
    cexp = jnp.concatenate(cexp, axis=0)
    cflat = jnp.concatenate(cflat, axis=0)
    best_s, best_e = [], []
    for _ in range(kk):
        m = jnp.max(cand, axis=0, keepdims=True)
        jsel = jnp.min(jnp.where(cand == m, cflat, kk * kk), axis=0, keepdims=True)
        hit = cflat == jsel
        best_s.append(m)
        best_e.append(jnp.max(jnp.where(hit, cexp, 0), axis=0, keepdims=True))
        cand = jnp.where(hit, -jnp.inf, cand)
    bs = _stack_rows(best_s)
    ex = jnp.exp(bs - bs[0:1])
    return _stack_rows(best_e), ex / jnp.sum(ex, axis=0, keepdims=True)


def _peer_scores(h2, w_query, sub_keys):
    n, d = h2.shape
    wq = w_query.astype(BF16)
    keys = sub_keys.reshape(2, SUBLANES, N_KEYS // SUBLANES, -1).swapaxes(1, 2).reshape(sub_keys.shape)
    khi = keys.astype(BF16)
    klo = (keys - khi.astype(F32)).astype(BF16)
    per_step = ROW_BLOCK // PEER_BLOCK
    blk = (per_step, 2 * PEER_HEADS, N_KEYS, PEER_BLOCK)
    return pl.pallas_call(
        _peer_scores_kernel,
        grid=(n // ROW_BLOCK,),
        in_specs=[pl.BlockSpec((ROW_BLOCK, d), lambda i: (i, 0)), _full(wq.shape), _full(khi.shape), _full(klo.shape)],
        out_specs=pl.BlockSpec(blk, lambda i: (i, 0, 0, 0)),
        out_shape=jax.ShapeDtypeStruct((n // PEER_BLOCK,) + blk[1:], F32),
        compiler_params=_params("arbitrary"),
        name="peer_scores",
    )(h2, wq, khi, klo)


def _pack_kernel(lo_ref, hi_ref, o_ref):
    rows = lo_ref.shape[0]
    lo = pltpu.bitcast(lo_ref[...].astype(BF16).astype(F32), U32) >> 16
    hi = pltpu.bitcast(hi_ref[...].astype(BF16).astype(F32), U32) & jnp.uint32(0xFFFF0000)
    word = lo | hi
    for s in range(word.shape[1] // LANES):
        o_ref[pl.ds(s, rows, stride=SUBLANES), :] = word[:, s * LANES:(s + 1) * LANES]


def _pack_table(tab):
    e, d = tab.shape
    assert d == SUBLANES * LANES
    half_blocks = e // 2 // PACK_BLOCK
    return pl.pallas_call(
        _pack_kernel,
        grid=(half_blocks,),
        in_specs=[pl.BlockSpec((PACK_BLOCK, d), lambda i: (i, 0)),
                  pl.BlockSpec((PACK_BLOCK, d), lambda i: (i + half_blocks, 0))],
        out_specs=pl.BlockSpec((PACK_BLOCK * SUBLANES, LANES), lambda i: (i, 0)),
        out_shape=jax.ShapeDtypeStruct((e // 2 * SUBLANES, LANES), U32),
        compiler_params=_params("arbitrary"),
        name="pack_table",
    )(tab, tab)


def _table_tile(tab_ref, row8):
    return pltpu.bitcast(tab_ref[pl.ds(pl.multiple_of(row8, SUBLANES), SUBLANES), :], BF16)


def _peer_u_consts():
    k = jnp.arange(2 * LANES)
    col = jnp.arange(2 * PEER_PICKS)
    row = jnp.arange(PEER_PICKS)
    fold = ((k[None, :] // 16 == jnp.arange(2 * U_TILE_PICKS)[:, None] // 2)
            & (k[None, :] % 2 == jnp.arange(2 * U_TILE_PICKS)[:, None] % 2)).astype(BF16)
    q = jnp.arange(LANES)
    r = jnp.arange(SUBLANES)[:, None, None]
    rowq = 8 * ((q % 8) // 2)[None, :, None] + r
    colq = 2 * (32 * (q // 8)[None, :, None] + 2 * (rowq // 2) + (q % 2)[None, :, None]) + rowq % 2
    place = ((col[None, None, :] == colq) & (q < 32)[None, :, None]).astype(BF16)
    dup = ((col[None, :] // 2) == jnp.arange(PEER_PICKS)[:, None]).astype(BF16)
    return fold, place, dup


def _index_copy(row_hbm, idx_smem, sem, batch, slot):
    return pltpu.make_async_copy(row_hbm.at[pl.ds(batch * PEER_UNROLL, PEER_UNROLL)], idx_smem.at[slot], sem.at[slot])


def _for_each_token_batch(row_hbm, idx_smem, sem, block_tokens, body):
    step = pl.program_id(0)
    per_step = block_tokens // PEER_UNROLL
    assert per_step % 2 == 0
    total = pl.num_programs(0) * per_step

    @pl.when(step == 0)
    def _():
        _index_copy(row_hbm, idx_smem, sem, 0, 0).start()

    def pair(k, carry):
        for slot in range(2):
            local = 2 * k + slot
            batch = step * per_step + local
            _index_copy(row_hbm, idx_smem, sem, batch, slot).wait()

            @pl.when(batch + 1 < total)
            def _():
                _index_copy(row_hbm, idx_smem, sem, batch + 1, 1 - slot).start()

            body(local * PEER_UNROLL, idx_smem.at[slot])
        return carry

    lax.fori_loop(0, per_step // 2, pair, 0)


def _peer_u_kernel(sfirst_ref, snext_ref, x_ref, tab_ref, fold_ref, place_ref, dup_ref,
                   coefh_ref, rows_ref, sum_scr, e_scr, g_scr, rows_buf, hbit_buf, gate_buf, idx_smem, sem):
    step = pl.program_id(0)
    cur = step % 2
    tb = x_ref.shape[0]
    per_step = tb // PEER_UNROLL
    assert per_step == PEER_HEADS and per_step % 2 == 0
    ntiles = PEER_PICKS // (2 * U_TILE_PICKS)
    lane = lax.broadcasted_iota(I32, (SUBLANES, LANES), 1)
    sub = lax.broadcasted_iota(I32, (SUBLANES, LANES), 0)

    def select_head(score_ref, head):
        ids, gates = _select_experts(score_ref[0, 2 * head], score_ref[0, 2 * head + 1])
        r0 = pl.multiple_of(head * PEER_TOPK, PEER_TOPK)
        e_scr[pl.ds(r0, PEER_TOPK), :] = ids
        g_scr[pl.ds(r0, PEER_TOPK), :] = gates

    def publish(buf):
        e = e_scr[...]
        rows_buf[buf] = ((e & (PEER_HALF_EXPERTS - 1)) * SUBLANES).T
        hbit_buf[buf] = (e >> (PEER_HALF_EXPERTS.bit_length() - 1)).astype(F32).T
        gate_buf[buf] = g_scr[...].T

    def index_copy(buf, batch, slot):
        return pltpu.make_async_copy(rows_buf.at[buf, pl.ds(batch * PEER_UNROLL, PEER_UNROLL)], idx_smem.at[slot],
                                     sem.at[slot])

    @pl.when(step == 0)
    def _():
        for head in range(PEER_HEADS):
            select_head(sfirst_ref, head)
        publish(0)
        index_copy(0, 0, 0).start()

    def tokens(t0, idx):
        by_row = None
        for u in range(PEER_UNROLL):
            t = t0 + u
            x = x_ref[t]
            xbits = pltpu.bitcast(x.astype(BF16).astype(F32), U32)
            xw = pltpu.bitcast(xbits | (xbits >> 16), BF16)
            sums = jnp.zeros((SUBLANES, LANES), F32)
            for j in range(ntiles):
                cols = []
                for ab in range(2):
                    prods = []
                    for mm in range(U_TILE_PICKS):
                        c = j * 2 * U_TILE_PICKS + 2 * mm + ab
                        prods.append(_table_tile(tab_ref, idx[u, c]) * xw)
                    cols.append(jnp.concatenate(prods, axis=0))
                res = _dot(fold_ref[...], jnp.concatenate(cols, axis=1))
                for rb in range(2 * U_TILE_PICKS // SUBLANES):
                    for ab in range(2):
                        part = res[rb * SUBLANES:(rb + 1) * SUBLANES, ab * LANES:(ab + 1) * LANES]
                        q = j * 8 + rb * 2 + ab
                        sums = jnp.where(lane == q, jnp.sum(part, axis=1, keepdims=True), sums)
            if u % SUBLANES == 0:
                by_row = [jnp.zeros((SUBLANES, LANES), F32)] * SUBLANES
            by_row = [jnp.where(sub == u % SUBLANES, jnp.broadcast_to(sums[r:r + 1], (SUBLANES, LANES)), by_row[r])
                      for r in range(SUBLANES)]
            if u % SUBLANES == SUBLANES - 1:
                first = pl.multiple_of(t0 + u - (SUBLANES - 1), SUBLANES)
                for r in range(SUBLANES):
                    sum_scr[r, pl.ds(first, SUBLANES), :] = by_row[r]

    def pair(k, carry):
        for slot in range(2):
            batch = 2 * k + slot
            index_copy(cur, batch, slot).wait()

            @pl.when(batch + 1 < per_step)
            def _():
                index_copy(cur, batch + 1, 1 - slot).start()

            tokens(batch * PEER_UNROLL, idx_smem.at[slot])
            select_head(snext_ref, batch)
        return carry

    lax.fori_loop(0, per_step // 2, pair, 0)
    publish(1 - cur)

    @pl.when(step + 1 < pl.num_programs(0))
    def _():
        index_copy(1 - cur, 0, 0).start()

    rows_ref[...] = rows_buf[cur]
    a2 = jnp.zeros((tb, 2 * PEER_PICKS), F32)
    for r in range(SUBLANES):
        shi, slo = _split2(sum_scr[r])
        a2 = a2 + _dot(shi, place_ref[r]) + _dot(slo, place_ref[r])
    ghi, glo = _split2(gate_buf[cur])
    g2 = _dot(ghi, dup_ref[...]) + _dot(glo, dup_ref[...])
    hbit2 = _dot(hbit_buf[cur].astype(BF16), dup_ref[...])
    parity = (lax.broadcasted_iota(I32, a2.shape, 1) % 2).astype(F32)
    coef = g2 * (0.5 * a2 * (1.0 + lax.erf(a2 * math.sqrt(0.5))))
    coefh_ref[...] = jnp.where(hbit2 == parity, coef, 0.0).astype(BF16)


def _peer_v_consts():
    col = jnp.arange(2 * PEER_PICKS * SUBLANES)
    ab, p, r = col // (PEER_PICKS * SUBLANES), (col // 16) % (PEER_PICKS // 2), col % 16
    src = jnp.arange(2 * PEER_PICKS)
    expand = ((src[:, None] // 2 == (2 * p + ab)[None, :]) & (src[:, None] % 2 == (r % 2)[None, :])).astype(BF16)
    kk = jnp.arange(PEER_PICKS * SUBLANES)
    diag = ((kk[None, :] % 16) // 2 == (jnp.arange(16) % SUBLANES)[:, None]).astype(F32)
    return expand, diag


def _peer_v_kernel(row_hbm, coefh_ref, x_ref, tab_ref, expand_ref, diag_ref, g_ref, b_ref, o_ref, lrow_scr, y_scr,
                   idx_smem, sem):
    tb = x_ref.shape[0]
    kdim = PEER_PICKS * SUBLANES
    lrow_scr[...] = _dot(coefh_ref[...], expand_ref[...])
    diag = diag_ref[...]
    sub = lax.broadcasted_iota(I32, (SUBLANES, LANES), 0)

    def tokens(t0, idx):
        by_row = None
        for u in range(PEER_UNROLL):
            t = t0 + u
            even = jnp.broadcast_to(lrow_scr[pl.ds(t, 1), 0:kdim], (SUBLANES, kdim))
            odd = jnp.broadcast_to(lrow_scr[pl.ds(t, 1), kdim:2 * kdim], (SUBLANES, kdim))
            lhs = (jnp.concatenate([even, odd], axis=0) * diag).astype(BF16)
            tiles = []
            for p in range(PEER_PICKS // 2):
                wa = _table_tile(tab_ref, idx[u, 2 * p])
                wb = _table_tile(tab_ref, idx[u, 2 * p + 1])
                tiles.append(jnp.concatenate([wa, wb], axis=1))
            res = _dot(lhs, jnp.concatenate(tiles, axis=0))
            y = res[0:SUBLANES, 0:LANES] + res[SUBLANES:, LANES:]
            if u % SUBLANES == 0:
                by_row = [jnp.zeros((SUBLANES, LANES), F32)] * SUBLANES
            by_row = [jnp.where(sub == u % SUBLANES, jnp.broadcast_to(y[r:r + 1], (SUBLANES, LANES)), by_row[r])
                      for r in range(SUBLANES)]
            if u % SUBLANES == SUBLANES - 1:
                first = pl.multiple_of(t0 + u - (SUBLANES - 1), SUBLANES)
                for r in range(SUBLANES):
                    y_scr[r, pl.ds(first, SUBLANES), :] = by_row[r]

    _for_each_token_batch(row_hbm, idx_smem, sem, tb, tokens)
    z = [DEEPNORM_ALPHA * x_ref[:, r * LANES:(r + 1) * LANES] + y_scr[r] for r in range(SUBLANES)]
    cnt = SUBLANES * LANES
    mu = sum(jnp.sum(zr, axis=1, keepdims=True) for zr in z) / cnt
    zc = [zr - mu for zr in z]
    var = sum(jnp.sum(c * c, axis=1, keepdims=True) for c in zc) / cnt
    rstd = lax.rsqrt(var + LN_EPS)
    for r in range(SUBLANES):
        o_ref[:, r * LANES:(r + 1) * LANES] = zc[r] * rstd * g_ref[r:r + 1, :] + b_ref[r:r + 1, :]


def _peer_ffn(h2, scores, expert_u, expert_v, ln_g, ln_b):
    n, d = h2.shape
    rows8 = d // LANES
    assert rows8 == SUBLANES and PEER_PICKS == LANES and expert_u.shape[0] == 2 * PEER_HALF_EXPERTS
    x3 = h2.reshape(n, rows8, LANES)
    u_pk = _pack_table(expert_u)
    v_pk = _pack_table(expert_v)
    hbm = pl.BlockSpec(memory_space=pl.ANY)
    staging = [pltpu.SMEM((2, PEER_UNROLL, PEER_PICKS), I32), pltpu.SemaphoreType.DMA((2,))]
    picks = pl.BlockSpec((PEER_BLOCK, PEER_PICKS), lambda i: (i, 0))
    picks2 = pl.BlockSpec((PEER_BLOCK, 2 * PEER_PICKS), lambda i: (i, 0))
    tok = pl.BlockSpec((PEER_BLOCK, rows8, LANES), lambda i: (i, 0, 0))
    flat = pl.BlockSpec((PEER_BLOCK, d), lambda i: (i, 0))
    table = pl.BlockSpec(u_pk.shape, lambda i: (0, 0), pipeline_mode=pl.Buffered(1))
    u_consts = _peer_u_consts()
    nblocks = n // PEER_BLOCK
    sblk = (1,) + scores.shape[1:]
    coefh, row = pl.pallas_call(
        _peer_u_kernel,
        grid=(nblocks,),
        in_specs=[pl.BlockSpec(sblk, lambda i: (0, 0, 0, 0)),
                  pl.BlockSpec(sblk, lambda i: (jnp.minimum(i + 1, nblocks - 1), 0, 0, 0)),
                  tok, table] + [_full(c.shape) for c in u_consts],
        out_specs=[picks2, picks],
        out_shape=[jax.ShapeDtypeStruct((n, 2 * PEER_PICKS), BF16), jax.ShapeDtypeStruct((n, PEER_PICKS), I32)],
        scratch_shapes=[pltpu.VMEM((SUBLANES, PEER_BLOCK, LANES), F32),
                        pltpu.VMEM((PEER_PICKS, PEER_BLOCK), I32), pltpu.VMEM((PEER_PICKS, PEER_BLOCK), F32),
                        pltpu.VMEM((2, PEER_BLOCK, PEER_PICKS), I32), pltpu.VMEM((2, PEER_BLOCK, PEER_PICKS), F32),
                        pltpu.VMEM((2, PEER_BLOCK, PEER_PICKS), F32)] + staging,
        compiler_params=_params("arbitrary"),
        name="peer_u",
    )(scores, scores, x3, u_pk, *u_consts)
    v_consts = _peer_v_consts()
    out = pl.pallas_call(
        _peer_v_kernel,
        grid=(n // PEER_BLOCK,),
        in_specs=[hbm, picks2, flat, table] + [_full(c.shape) for c in v_consts]
        + [_full((rows8, LANES)), _full((rows8, LANES))],
        out_specs=flat,
        out_shape=jax.ShapeDtypeStruct((n, d), F32),
        scratch_shapes=[pltpu.VMEM((PEER_BLOCK, 2 * PEER_PICKS * SUBLANES), F32),
                        pltpu.VMEM((rows8, PEER_BLOCK, LANES), F32)] + staging,
        compiler_params=_params("arbitrary"),
        name="peer_v",
    )(row, coefh, h2, v_pk, *v_consts, ln_g.reshape(rows8, LANES), ln_b.reshape(rows8, LANES))
    return out


def kernel(x, mem, ln_in_g, ln_in_b, w_in, conv_w, conv_b, mlstm_i_bias, mlstm_f_bias, mlstm_norm_g, rel_bias, w_out, ln1_g, ln1_b, xattn_w_q, xattn_w_kv, xattn_w_o, ln2_g, ln2_b, peer_w_query, peer_sub_keys, peer_u, peer_v, ln3_g, ln3_b):
    batch, seq, d = x.shape
    n = batch * seq
    assert w_in.shape[0] == DEPTH
    h, qkv, mqk, mv, mo, gates = _ln_inproj(x.reshape(n, d), ln_in_g, ln_in_b, w_in[0])
    for l in range(DEPTH):
        if l > 0:
            raise NotImplementedError("input projection of deeper layers")
        att = _attention(qkv, rel_bias[l], batch, seq)
        hm = _mlstm(mqk, mv, mo, gates, conv_w[l], conv_b[l], mlstm_i_bias[l], mlstm_f_bias[l], mlstm_norm_g[l],
                    batch, seq)
        h = _outproj(att.reshape(n, ATT_WIDTH), hm.reshape(n, MLSTM_WIDTH), h, w_out[l], ln1_g[l], ln1_b[l])
        k2, v2 = _kvproj(mem.reshape(-1, d), xattn_w_kv[l])
        m = mem.shape[1]
        h = _xattn(h.reshape(batch, seq, d), k2.reshape(batch, m, d), v2.reshape(batch, m, d),
                   xattn_w_q[l], xattn_w_o[l], ln2_g[l], ln2_b[l]).reshape(n, d)
        scores = _peer_scores(h, peer_w_query[l], peer_sub_keys[l])
        h = _peer_ffn(h, scores, peer_u[l], peer_v[l], ln3_g[l], ln3_b[l])
    return h.reshape(batch, seq, d)
```

```python
import math

import jax
import jax.numpy as jnp
from jax import lax
from jax.experimental import pallas as pl
from jax.experimental.pallas import tpu as pltpu

F32, BF16, I32, U32 = jnp.float32, jnp.bfloat16, jnp.int32, jnp.uint32

DEPTH = 1
LN_EPS = 1e-5
DEEPNORM_ALPHA = (2.0 * DEPTH) ** 0.25
CHUNK = 64
ATT_HEADS, ATT_HEAD_DIM, ATT_LEFT_CHUNKS, MAX_REL = 8, 64, 8, 128
ATT_WIDTH = ATT_HEADS * ATT_HEAD_DIM
MLSTM_HEADS, MLSTM_HEAD_DIM, CONV_WIDTH = 4, 128, 4
MLSTM_WIDTH = MLSTM_HEADS * MLSTM_HEAD_DIM
XATT_HEADS = 4
PEER_HEADS, N_KEYS, PEER_TOPK = 8, 128, 16
PEER_PICKS = PEER_HEADS * PEER_TOPK
PEER_HALF_EXPERTS = N_KEYS * N_KEYS // 2
U_TILE_PICKS = 16
PEER_UNROLL = 16

LANES = 128
SUBLANES = 8
VMEM_LIMIT_BYTES = 56 * 1024 * 1024

NEG_BIG = -1e30

ROW_BLOCK = 512
ATT_BLOCK = 256
ATT_KEY_BLOCKS = 3
MLSTM_BLOCK = 256
MLSTM_SEQS = 1
PEER_BLOCK = 128
PACK_BLOCK = 256


def _params(*semantics):
    return pltpu.CompilerParams(dimension_semantics=semantics, vmem_limit_bytes=VMEM_LIMIT_BYTES)


def _layer_norm(x, g, b):
    mu = jnp.mean(x, -1, keepdims=True)
    xc = x - mu
    var = jnp.mean(xc * xc, -1, keepdims=True)
    return xc * lax.rsqrt(var + LN_EPS) * g + b


def _split2(x):
    hi = x.astype(BF16)
    lo = (x - hi.astype(F32)).astype(BF16)
    return hi, lo


def _split3(x):
    hi = x.astype(BF16)
    r = x - hi.astype(F32)
    mid = r.astype(BF16)
    lo = (r - mid.astype(F32)).astype(BF16)
    return hi, mid, lo


def _dot(a, b):
    return jnp.dot(a, b, preferred_element_type=F32)


def _dot_nt(a, b):
    return lax.dot_general(a, b, (((1,), (1,)), ((), ())), preferred_element_type=F32)


def _dot_tn(a, b):
    return lax.dot_general(a, b, (((0,), (0,)), ((), ())), preferred_element_type=F32)


def _full(shape):
    return pl.BlockSpec(shape, lambda *_: (0,) * len(shape))


def _ln_inproj_kernel(x_ref, g_ref, b_ref, wqkv_ref, wmqk_ref, wmv_ref, wmo_ref, wghi_ref, wglo_ref,
                      h_ref, qkv_ref, mqk_ref, mv_ref, mo_ref, gate_ref):
    h = _layer_norm(x_ref[...], g_ref[...], b_ref[...])
    h_ref[...] = h
    hb, hlo = _split2(h)
    qkv_ref[...] = _dot(hb, wqkv_ref[...]).astype(BF16)
    mqk_ref[...] = _dot(hb, wmqk_ref[...]).astype(BF16)
    mv_ref[...] = _dot(hb, wmv_ref[...]).astype(BF16)
    mo_ref[...] = _dot(hb, wmo_ref[...]).astype(BF16)
    gate_ref[...] = _dot(hb, wghi_ref[...]) + _dot(hlo, wghi_ref[...]) + _dot(hb, wglo_ref[...])


def _ln_inproj(x2, g, b, w_in):
    n, d = x2.shape
    a3 = 3 * ATT_WIDTH
    wqkv = w_in[:, :a3].astype(BF16)
    wmqk = w_in[:, a3:a3 + 2 * MLSTM_WIDTH].astype(BF16)
    wmv = w_in[:, a3 + 2 * MLSTM_WIDTH:a3 + 3 * MLSTM_WIDTH].astype(BF16)
    wmo = w_in[:, a3 + 3 * MLSTM_WIDTH:a3 + 4 * MLSTM_WIDTH].astype(BF16)
    wg = jnp.pad(w_in[:, a3 + 4 * MLSTM_WIDTH:], ((0, 0), (0, LANES - 2 * MLSTM_HEADS)))
    wghi = wg.astype(BF16)
    wglo = (wg - wghi.astype(F32)).astype(BF16)
    rows = lambda w: pl.BlockSpec((ROW_BLOCK, w), lambda i: (i, 0))
    return pl.pallas_call(
        _ln_inproj_kernel,
        grid=(n // ROW_BLOCK,),
        in_specs=[rows(d), _full((1, d)), _full((1, d)), _full(wqkv.shape), _full(wmqk.shape),
                  _full(wmv.shape), _full(wmo.shape), _full(wghi.shape), _full(wglo.shape)],
        out_specs=[rows(d), rows(a3), rows(2 * MLSTM_WIDTH), rows(MLSTM_WIDTH), rows(MLSTM_WIDTH), rows(LANES)],
        out_shape=[jax.ShapeDtypeStruct((n, d), F32), jax.ShapeDtypeStruct((n, a3), BF16),
                   jax.ShapeDtypeStruct((n, 2 * MLSTM_WIDTH), BF16), jax.ShapeDtypeStruct((n, MLSTM_WIDTH), BF16),
                   jax.ShapeDtypeStruct((n, MLSTM_WIDTH), BF16), jax.ShapeDtypeStruct((n, LANES), F32)],
        compiler_params=_params("arbitrary"),
        name="ln_inproj",
    )(x2, g.reshape(1, d), b.reshape(1, d), wqkv, wmqk, wmv, wmo, wghi, wglo)


def _attn_kernel(q_ref, k0_ref, k1_ref, k2_ref, v0_ref, v1_ref, v2_ref, bias_ref, o_ref):
    i = pl.program_id(1)
    nkeys = ATT_KEY_BLOCKS * ATT_BLOCK
    q = q_ref[0] * (ATT_HEAD_DIM ** -0.5)
    kcat = jnp.concatenate([k0_ref[0], k1_ref[0], k2_ref[0]], axis=0)
    vcat = jnp.concatenate([v0_ref[0], v1_ref[0], v2_ref[0]], axis=0)
    col = lax.broadcasted_iota(I32, (1, nkeys), 1)
    in_seq = col >= (ATT_KEY_BLOCKS - 1 - i) * ATT_BLOCK
    lane = lax.broadcasted_iota(I32, (1, LANES), 1)
    low = lane < ATT_HEAD_DIM
    outs = []
    for pair in range(ATT_HEADS // 2):
        sl = slice(pair * LANES, (pair + 1) * LANES)
        qp, kp, vp = q[:, sl], kcat[:, sl], vcat[:, sl]
        halves = []
        for half in range(2):
            keep = low if half == 0 else jnp.logical_not(low)
            qh = jnp.where(keep, qp, jnp.zeros_like(qp))
            s = _dot_nt(qh, kp) + bias_ref[2 * pair + half]
            s = jnp.where(in_seq, s, NEG_BIG)
            m = jnp.max(s, -1, keepdims=True)
            p = jnp.exp(s - m)
            l = jnp.sum(p, -1, keepdims=True)
            halves.append(_dot(p.astype(BF16), vp) / l)
        outs.append(jnp.where(low, halves[0], halves[1]))
    o_ref[0] = jnp.concatenate(outs, axis=-1).astype(BF16)


def _attn_bias(rel_bias):
    nq, nk = ATT_BLOCK, ATT_KEY_BLOCKS * ATT_BLOCK
    r = jnp.arange(nq)[:, None]
    j = jnp.arange(nk)[None, :]
    allowed = (j // CHUNK >= r // CHUNK) & (j // CHUNK <= r // CHUNK + ATT_LEFT_CHUNKS)
    period = nq + nk - 1
    i = jnp.arange(period)
    d = jnp.where(i < nk, i, i - period)
    rel = d - (ATT_KEY_BLOCKS - 1) * ATT_BLOCK
    line = rel_bias[:, jnp.clip(rel, -MAX_REL, MAX_REL) + MAX_REL].astype(F32)
    rep = jnp.tile(line, (1, nq + 1))[:, :nq * (period - 1)]
    tab = rep.reshape(-1, nq, period - 1)[:, :, :nk]
    return jnp.where(allowed[None], tab, NEG_BIG)


def _attention(qkv, rel_bias, batch, seq):
    assert ATT_LEFT_CHUNKS * CHUNK == (ATT_KEY_BLOCKS - 1) * ATT_BLOCK
    qkv3 = qkv.reshape(batch, seq, 3 * ATT_WIDTH)
    bias = _attn_bias(rel_bias)
    blk = (1, ATT_BLOCK, ATT_WIDTH)

    def kv_spec(col, j):
        return pl.BlockSpec(blk, lambda b, i: (b, jnp.maximum(i - (ATT_KEY_BLOCKS - 1) + j, 0), col))

    return pl.pallas_call(
        _attn_kernel,
        grid=(batch, seq // ATT_BLOCK),
        in_specs=[pl.BlockSpec(blk, lambda b, i: (b, i, 0))]
        + [kv_spec(1, j) for j in range(ATT_KEY_BLOCKS)] + [kv_spec(2, j) for j in range(ATT_KEY_BLOCKS)]
        + [_full(bias.shape)],
        out_specs=pl.BlockSpec(blk, lambda b, i: (b, i, 0)),
        out_shape=jax.ShapeDtypeStruct((batch, seq, ATT_WIDTH), BF16),
        compiler_params=_params("arbitrary", "arbitrary"),
        name="attn",
    )(qkv3, qkv3, qkv3, qkv3, qkv3, qkv3, qkv3, bias)


def _mlstm_kernel(mqk_ref, mv_ref, mo_ref, gate_ref, convw_ref, convb_ref, gbias_ref, ng_ref, out_ref,
                  xpad, q_scr, k_scr, gate_scr, logf_scr, c_st, n_st, m_st):
    j = pl.program_id(1)
    rows = MLSTM_BLOCK
    d = MLSTM_HEAD_DIM

    @pl.when(j == 0)
    def _():
        xpad[:, 0:SUBLANES, :] = jnp.zeros((MLSTM_SEQS, SUBLANES, 2 * MLSTM_WIDTH), F32)
        c_st[...] = jnp.zeros_like(c_st)
        n_st[...] = jnp.zeros_like(n_st)
        m_st[...] = jnp.zeros_like(m_st)

    for bb in range(MLSTM_SEQS):
        xpad[bb, SUBLANES:SUBLANES + rows, :] = mqk_ref[bb].astype(F32)
        acc = jnp.broadcast_to(convb_ref[...], (rows, 2 * MLSTM_WIDTH))
        for t in range(CONV_WIDTH):
            acc = acc + convw_ref[t:t + 1, :] * xpad[bb, pl.ds(SUBLANES - (CONV_WIDTH - 1) + t, rows), :]
        xpad[bb, 0:SUBLANES, :] = xpad[bb, rows:rows + SUBLANES, :]
        qk = acc * jax.nn.sigmoid(acc)
        q_scr[bb] = qk[:, :MLSTM_WIDTH].astype(BF16)
        k_scr[bb] = qk[:, MLSTM_WIDTH:] * (d ** -0.5)

        gates = gate_ref[bb] + gbias_ref[...]
        gate_scr[bb] = gates
        logf_scr[bb] = jax.nn.log_sigmoid(gates)

    ri = lax.broadcasted_iota(I32, (CHUNK, CHUNK), 0)
    ci = lax.broadcasted_iota(I32, (CHUNK, CHUNK), 1)
    causal = ci <= ri
    tri = causal.astype(BF16)

    def chunk_body(c, carry):
        for bb in range(MLSTM_SEQS):
            _mlstm_chunk(bb, pl.multiple_of(c * CHUNK, CHUNK), causal, tri, mv_ref, mo_ref, ng_ref, out_ref,
                         q_scr, k_scr, gate_scr, logf_scr, c_st, n_st, m_st)
        return carry

    lax.fori_loop(0, rows // CHUNK, chunk_body, 0)


def _mlstm_chunk(bb, r0, causal, tri, mv_ref, mo_ref, ng_ref, out_ref, q_scr, k_scr, gate_scr, logf_scr,
                 c_st, n_st, m_st):
    d = MLSTM_HEAD_DIM
    g = gate_scr[bb, pl.ds(r0, CHUNK), :]
    lf = logf_scr[bb, pl.ds(r0, CHUNK), :]
    b3 = _dot(tri, jnp.concatenate(_split3(lf), axis=1))
    bcol = b3[:, :LANES] + b3[:, LANES:2 * LANES] + b3[:, 2 * LANES:]
    g_t = g.T
    b_t = bcol.T
    m_all = m_st[bb]
    n_all = n_st[bb]
    for h in range(MLSTM_HEADS):
        hs = slice(h * d, (h + 1) * d)
        ic = g[:, h:h + 1]
        bc = bcol[:, MLSTM_HEADS + h:MLSTM_HEADS + h + 1]
        ir = g_t[h:h + 1, :]
        br = b_t[MLSTM_HEADS + h:MLSTM_HEADS + h + 1, :]
        m_prev = m_all[h:h + 1, 0:1]
        n_prev = n_all[h:h + 1, :]
        c_prev = c_st[bb, h]
        qh = q_scr[bb, pl.ds(r0, CHUNK), hs]
        kh = k_scr[bb, pl.ds(r0, CHUNK), hs]
        vh = mv_ref[bb, pl.ds(r0, CHUNK), hs]

        log_d = jnp.where(causal, bc - br + ir, NEG_BIG)
        inter = bc + m_prev
        m_t = jnp.maximum(inter, jnp.max(log_d, -1, keepdims=True))
        d_mat = jnp.exp(log_d - m_t)
        w_inter = jnp.exp(inter - m_t)
        qk_d = _dot_nt(qh, kh.astype(BF16)) * d_mat
        lhs = jnp.concatenate([(w_inter * qh.astype(F32)).astype(BF16), qk_d.astype(BF16)], axis=1)
        num = _dot(lhs, jnp.concatenate([c_prev.astype(BF16), vh], axis=0))
        den = (w_inter * jnp.sum(qh.astype(F32) * n_prev, -1, keepdims=True)
               + jnp.sum(qk_d, -1, keepdims=True))
        hh = num / jnp.maximum(jnp.abs(den), jnp.exp(-m_t))

        b_last = bc[CHUNK - 1:CHUNK, :]
        log_in = b_last - bc + ic
        m_new = jnp.maximum(b_last + m_prev, jnp.max(log_in, 0, keepdims=True))
        w_prev = jnp.exp(b_last + m_prev - m_new)
        kw = kh * jnp.exp(log_in - m_new)
        c_st[bb, h] = w_prev * c_prev + _dot_tn(kw.astype(BF16), vh)
        n_st[bb, h:h + 1, :] = w_prev * n_prev + jnp.sum(kw, 0, keepdims=True)
        m_st[bb, h:h + 1, :] = jnp.broadcast_to(m_new, (1, LANES))

        mu = jnp.mean(hh, -1, keepdims=True)
        hc = hh - mu
        var = jnp.mean(hc * hc, -1, keepdims=True)
        hn = hc * lax.rsqrt(var + LN_EPS) * ng_ref[:, hs]
        og = jax.nn.sigmoid(mo_ref[bb, pl.ds(r0, CHUNK), hs].astype(F32))
        out_ref[bb, pl.ds(r0, CHUNK), hs] = (og * hn).astype(BF16)


def _mlstm(mqk, mv, mo, gates, conv_w, conv_b, i_bias, f_bias, norm_g, batch, seq):
    w2 = 2 * MLSTM_WIDTH
    gbias = jnp.pad(jnp.concatenate([i_bias, f_bias]).astype(F32), (0, LANES - 2 * MLSTM_HEADS)).reshape(1, LANES)
    ns = MLSTM_SEQS
    assert batch % ns == 0
    blk = lambda w: pl.BlockSpec((ns, MLSTM_BLOCK, w), lambda b, i: (b, i, 0))
    return pl.pallas_call(
        _mlstm_kernel,
        grid=(batch // ns, seq // MLSTM_BLOCK),
        in_specs=[blk(w2), blk(MLSTM_WIDTH), blk(MLSTM_WIDTH), blk(LANES),
                  _full((CONV_WIDTH, w2)), _full((1, w2)), _full((1, LANES)), _full((1, MLSTM_WIDTH))],
        out_specs=blk(MLSTM_WIDTH),
        out_shape=jax.ShapeDtypeStruct((batch, seq, MLSTM_WIDTH), BF16),
        scratch_shapes=[pltpu.VMEM((ns, MLSTM_BLOCK + SUBLANES, w2), F32),
                        pltpu.VMEM((ns, MLSTM_BLOCK, MLSTM_WIDTH), BF16),
                        pltpu.VMEM((ns, MLSTM_BLOCK, MLSTM_WIDTH), F32),
                        pltpu.VMEM((ns, MLSTM_BLOCK, LANES), F32),
                        pltpu.VMEM((ns, MLSTM_BLOCK, LANES), F32),
                        pltpu.VMEM((ns, MLSTM_HEADS, MLSTM_HEAD_DIM, MLSTM_HEAD_DIM), F32),
                        pltpu.VMEM((ns, SUBLANES, MLSTM_HEAD_DIM), F32),
                        pltpu.VMEM((ns, SUBLANES, LANES), F32)],
        compiler_params=_params("arbitrary", "arbitrary"),
        name="mlstm",
    )(mqk.reshape(batch, seq, w2), mv.reshape(batch, seq, MLSTM_WIDTH), mo.reshape(batch, seq, MLSTM_WIDTH),
      gates.reshape(batch, seq, LANES), conv_w.astype(F32), conv_b.reshape(1, w2).astype(F32), gbias,
      norm_g.reshape(1, MLSTM_WIDTH).astype(F32))


def _outproj_kernel(att_ref, hm_ref, h_ref, wa_ref, wm_ref, g_ref, b_ref, o_ref):
    y = _dot(att_ref[...], wa_ref[...]) + _dot(hm_ref[...], wm_ref[...])
    o_ref[...] = _layer_norm(DEEPNORM_ALPHA * h_ref[...] + y, g_ref[...], b_ref[...])


def _outproj(att, hm, h, w_out, g, b):
    n, d = h.shape
    wa = w_out[:ATT_WIDTH].astype(BF16)
    wm = w_out[ATT_WIDTH:].astype(BF16)
    rows = lambda w: pl.BlockSpec((ROW_BLOCK, w), lambda i: (i, 0))
    return pl.pallas_call(
        _outproj_kernel,
        grid=(n // ROW_BLOCK,),
        in_specs=[rows(ATT_WIDTH), rows(MLSTM_WIDTH), rows(d), _full(wa.shape), _full(wm.shape),
                  _full((1, d)), _full((1, d))],
        out_specs=rows(d),
        out_shape=jax.ShapeDtypeStruct((n, d), F32),
        compiler_params=_params("arbitrary"),
        name="outproj",
    )(att, hm, h, wa, wm, g.reshape(1, d), b.reshape(1, d))


def _kvproj_kernel(mem_ref, w_ref, k_ref, v_ref):
    kv = _dot(mem_ref[...].astype(BF16), w_ref[...])
    d = k_ref.shape[-1]
    k_ref[...] = kv[:, :d].astype(BF16)
    v_ref[...] = kv[:, d:].astype(BF16)


def _kvproj(mem2, w_kv):
    n, d = mem2.shape
    w = w_kv.astype(BF16)
    blk = min(ROW_BLOCK, n)
    rows = pl.BlockSpec((blk, d), lambda i: (i, 0))
    return pl.pallas_call(
        _kvproj_kernel,
        grid=(n // blk,),
        in_specs=[rows, _full(w.shape)],
        out_specs=[rows, rows],
        out_shape=[jax.ShapeDtypeStruct((n, d), BF16)] * 2,
        compiler_params=_params("arbitrary"),
        name="kvproj",
    )(mem2, w)


def _xattn_kernel(h_ref, k_ref, v_ref, wq_ref, wo_ref, g_ref, b_ref, o_ref):
    h = h_ref[0]
    d = h.shape[-1]
    dh = d // XATT_HEADS
    q = (_dot(h.astype(BF16), wq_ref[...]) * (dh ** -0.5)).astype(BF16)
    outs = []
    for hd in range(XATT_HEADS):
        sl = slice(hd * dh, (hd + 1) * dh)
        s = _dot_nt(q[:, sl], k_ref[0, :, sl])
        m = jnp.max(s, -1, keepdims=True)
        p = jnp.exp(s - m)
        l = jnp.sum(p, -1, keepdims=True)
        outs.append((_dot(p.astype(BF16), v_ref[0, :, sl]) / l).astype(BF16))
    y = _dot(jnp.concatenate(outs, axis=-1), wo_ref[...])
    o_ref[0] = _layer_norm(DEEPNORM_ALPHA * h + y, g_ref[...], b_ref[...])


def _xattn(h3, k3, v3, w_q, w_o, g, b):
    batch, seq, d = h3.shape
    m = k3.shape[1]
    wq = w_q.astype(BF16)
    wo = w_o.astype(BF16)
    blk = pl.BlockSpec((1, ROW_BLOCK, d), lambda bb, i: (bb, i, 0))
    mem = pl.BlockSpec((1, m, d), lambda bb, i: (bb, 0, 0))
    return pl.pallas_call(
        _xattn_kernel,
        grid=(batch, seq // ROW_BLOCK),
        in_specs=[blk, mem, mem, _full(wq.shape), _full(wo.shape), _full((1, d)), _full((1, d))],
        out_specs=blk,
        out_shape=jax.ShapeDtypeStruct((batch, seq, d), F32),
        compiler_params=_params("arbitrary", "arbitrary"),
        name="xattn",
    )(h3, k3, v3, wq, wo, g.reshape(1, d), b.reshape(1, d))


def _topk_rows(s, k):
    groups = s.shape[0] // SUBLANES
    t = s.shape[1]
    sg = [s[g * SUBLANES:(g + 1) * SUBLANES] for g in range(groups)]
    sub = lax.broadcasted_iota(I32, (SUBLANES, t), 0)
    vals, ids = [], []
    for _ in range(k):
        m8, a8 = sg[0], jnp.zeros((SUBLANES, t), I32)
        for g in range(1, groups):
            upd = sg[g] > m8
            m8 = jnp.where(upd, sg[g], m8)
            a8 = jnp.where(upd, g, a8)
        m = jnp.max(m8, axis=0, keepdims=True)
        kid = jnp.min(jnp.where(m8 == m, sub * groups + a8, SUBLANES * groups), axis=0, keepdims=True)
        vals.append(m)
        ids.append(kid)
        gone = jnp.where(sub * groups + a8 == kid, a8, -1)
        sg = [jnp.where(gone == g, -jnp.inf, sg[g]) for g in range(groups)]
    return vals, ids


def _stack_rows(rows_list):
    k = len(rows_list)
    t = rows_list[0].shape[-1]
    iota = lax.broadcasted_iota(I32, (k, t), 0)
    out = jnp.broadcast_to(rows_list[0], (k, t))
    for r in range(1, k):
        out = jnp.where(iota == r, jnp.broadcast_to(rows_list[r], (k, t)), out)
    return out


def _peer_scores_kernel(h_ref, wq_ref, khi_ref, klo_ref, s_ref):
    q = _dot(h_ref[...].astype(BF16), wq_ref[...])
    for b in range(s_ref.shape[0]):
        for hp in range(2 * PEER_HEADS):
            p = hp % 2
            qhi, qlo = _split2(q[b * PEER_BLOCK:(b + 1) * PEER_BLOCK, hp * N_KEYS:(hp + 1) * N_KEYS])
            keys3 = jnp.concatenate([khi_ref[p], khi_ref[p], klo_ref[p]], axis=1)
            s_ref[b, hp] = _dot_nt(keys3, jnp.concatenate([qhi, qlo, qhi], axis=1))


def _select_experts(sc0, sc1):
    t = sc0.shape[1]
    kk = PEER_TOPK
    half8 = kk // 2
    sub = lax.broadcasted_iota(I32, (half8, t), 0)
    tops = []
    for s in (sc0, sc1):
        vals, idxs = _topk_rows(s, kk)
        tops.append((_stack_rows(vals), _stack_rows(idxs)))
    (s0, i0), (s1, i1) = tops
    e0 = i0 * N_KEYS
    cand, cexp, cflat = [], [], []
    for b in range(half8):
        cand.append(s0[:half8] + s1[b:b + 1])
        cexp.append(e0[:half8] + i1[b:b + 1])
        cflat.append(sub * kk + b)
    cand.append(s0[half8:] + s1[0:1])
    cexp.append(e0[half8:] + i1[0:1])
    cflat.append((sub + half8) * kk)
    cand.append(s0[0:1] + s1[half8:])
    cexp.append(e0[0:1] + i1[half8:])
    cflat.append(sub + half8)
    cand = jnp.concatenate(cand, axis=0)
    cexp = jnp.concatenate(cexp, axis=0)
    cflat = jnp.concatenate(cflat, axis=0)
    best_s, best_e = [], []
    for _ in range(kk):
        m = jnp.max(cand, axis=0, keepdims=True)
        jsel = jnp.min(jnp.where(cand == m, cflat, kk * kk), axis=0, keepdims=True)
        hit = cflat == jsel
        best_s.append(m)
        best_e.append(jnp.max(jnp.where(hit, cexp, 0), axis=0, keepdims=True))
        cand = jnp.where(hit, -jnp.inf, cand)
    bs = _stack_rows(best_s)
    ex = jnp.exp(bs - bs[0:1])
    return _stack_rows(best_e), ex / jnp.sum(ex, axis=0, keepdims=True)


def _peer_scores(h2, w_query, sub_keys):
    n, d = h2.shape
    wq = w_query.astype(BF16)
    keys = sub_keys.reshape(2, SUBLANES, N_KEYS // SUBLANES, -1).swapaxes(1, 2).reshape(sub_keys.shape)
    khi = keys.astype(BF16)
    klo = (keys - khi.astype(F32)).astype(BF16)
    per_step = ROW_BLOCK // PEER_BLOCK
    blk = (per_step, 2 * PEER_HEADS, N_KEYS, PEER_BLOCK)
    return pl.pallas_call(
        _peer_scores_kernel,
        grid=(n // ROW_BLOCK,),
        in_specs=[pl.BlockSpec((ROW_BLOCK, d), lambda i: (i, 0)), _full(wq.shape), _full(khi.shape), _full(klo.shape)],
        out_specs=pl.BlockSpec(blk, lambda i: (i, 0, 0, 0)),
        out_shape=jax.ShapeDtypeStruct((n // PEER_BLOCK,) + blk[1:], F32),
        compiler_params=_params("arbitrary"),
        name="peer_scores",
    )(h2, wq, khi, klo)


def _pack_kernel(lo_ref, hi_ref, o_ref):
    rows = lo_ref.shape[0]
    lo = pltpu.bitcast(lo_ref[...].astype(BF16).astype(F32), U32) >> 16
    hi = pltpu.bitcast(hi_ref[...].astype(BF16).astype(F32), U32) & jnp.uint32(0xFFFF0000)
    word = lo | hi
    for s in range(word.shape[1] // LANES):
        o_ref[pl.ds(s, rows, stride=SUBLANES), :] = word[:, s * LANES:(s + 1) * LANES]


def _pack_table(tab):
    e, d = tab.shape
    assert d == SUBLANES * LANES
    half_blocks = e // 2 // PACK_BLOCK
    return pl.pallas_call(
        _pack_kernel,
        grid=(half_blocks,),
        in_specs=[pl.BlockSpec((PACK_BLOCK, d), lambda i: (i, 0)),
                  pl.BlockSpec((PACK_BLOCK, d), lambda i: (i + half_blocks, 0))],
        out_specs=pl.BlockSpec((PACK_BLOCK * SUBLANES, LANES), lambda i: (i, 0)),
        out_shape=jax.ShapeDtypeStruct((e // 2 * SUBLANES, LANES), U32),
        compiler_params=_params("arbitrary"),
        name="pack_table",
    )(tab, tab)


def _table_tile(tab_ref, row8):
    return pltpu.bitcast(tab_ref[pl.ds(pl.multiple_of(row8, SUBLANES), SUBLANES), :], BF16)


def _peer_u_consts():
    k = jnp.arange(2 * LANES)
    col = jnp.arange(2 * PEER_PICKS)
    row = jnp.arange(PEER_PICKS)
    fold = ((k[None, :] // 16 == jnp.arange(2 * U_TILE_PICKS)[:, None] // 2)
            & (k[None, :] % 2 == jnp.arange(2 * U_TILE_PICKS)[:, None] % 2)).astype(BF16)
    q = jnp.arange(LANES)
    r = jnp.arange(SUBLANES)[:, None, None]
    rowq = 8 * ((q % 8) // 2)[None, :, None] + r
    colq = 2 * (32 * (q // 8)[None, :, None] + 2 * (rowq // 2) + (q % 2)[None, :, None]) + rowq % 2
    place = ((col[None, None, :] == colq) & (q < 32)[None, :, None]).astype(BF16)
    dup = ((col[None, :] // 2) == jnp.arange(PEER_PICKS)[:, None]).astype(BF16)
    return fold, place, dup


def _index_copy(row_hbm, idx_smem, sem, batch, slot):
    return pltpu.make_async_copy(row_hbm.at[pl.ds(batch * PEER_UNROLL, PEER_UNROLL)], idx_smem.at[slot], sem.at[slot])


def _for_each_token_batch(row_hbm, idx_smem, sem, block_tokens, body):
    step = pl.program_id(0)
    per_step = block_tokens // PEER_UNROLL
    assert per_step % 2 == 0
    total = pl.num_programs(0) * per_step

    @pl.when(step == 0)
    def _():
        _index_copy(row_hbm, idx_smem, sem, 0, 0).start()

    def pair(k, carry):
        for slot in range(2):
            local = 2 * k + slot
            batch = step * per_step + local
            _index_copy(row_hbm, idx_smem, sem, batch, slot).wait()

            @pl.when(batch + 1 < total)
            def _():
                _index_copy(row_hbm, idx_smem, sem, batch + 1, 1 - slot).start()

            body(local * PEER_UNROLL, idx_smem.at[slot])
        return carry

    lax.fori_loop(0, per_step // 2, pair, 0)


def _peer_u_kernel(sfirst_ref, snext_ref, x_ref, tab_ref, fold_ref, place_ref, dup_ref,
                   coefh_ref, rows_ref, sum_scr, e_scr, g_scr, rows_buf, hbit_buf, gate_buf, idx_smem, sem):
    step = pl.program_id(0)
    cur = step % 2
    tb = x_ref.shape[0]
    per_step = tb // PEER_UNROLL
    assert per_step == PEER_HEADS and per_step % 2 == 0
    ntiles = PEER_PICKS // (2 * U_TILE_PICKS)
    lane = lax.broadcasted_iota(I32, (SUBLANES, LANES), 1)
    sub = lax.broadcasted_iota(I32, (SUBLANES, LANES), 0)

    def select_head(score_ref, head):
        ids, gates = _select_experts(score_ref[0, 2 * head], score_ref[0, 2 * head + 1])
        r0 = pl.multiple_of(head * PEER_TOPK, PEER_TOPK)
        e_scr[pl.ds(r0, PEER_TOPK), :] = ids
        g_scr[pl.ds(r0, PEER_TOPK), :] = gates

    def publish(buf):
        e = e_scr[...]
        rows_buf[buf] = ((e & (PEER_HALF_EXPERTS - 1)) * SUBLANES).T
        hbit_buf[buf] = (e >> (PEER_HALF_EXPERTS.bit_length() - 1)).astype(F32).T
        gate_buf[buf] = g_scr[...].T

    def index_copy(buf, batch, slot):
        return pltpu.make_async_copy(rows_buf.at[buf, pl.ds(batch * PEER_UNROLL, PEER_UNROLL)], idx_smem.at[slot],
                                     sem.at[slot])

    @pl.when(step == 0)
    def _():
        for head in range(PEER_HEADS):
            select_head(sfirst_ref, head)
        publish(0)
        index_copy(0, 0, 0).start()

    def tokens(t0, idx):
        by_row = None
        for u in range(PEER_UNROLL):
            t = t0 + u
            x = x_ref[t]
            xbits = pltpu.bitcast(x.astype(BF16).astype(F32), U32)
            xw = pltpu.bitcast(xbits | (xbits >> 16), BF16)
            sums = jnp.zeros((SUBLANES, LANES), F32)
            for j in range(ntiles):
                cols = []
                for ab in range(2):
                    prods = []
                    for mm in range(U_TILE_PICKS):
                        c = j * 2 * U_TILE_PICKS + 2 * mm + ab
                        prods.append(_table_tile(tab_ref, idx[u, c]) * xw)
                    cols.append(jnp.concatenate(prods, axis=0))
                res = _dot(fold_ref[...], jnp.concatenate(cols, axis=1))
                for rb in range(2 * U_TILE_PICKS // SUBLANES):
                    for ab in range(2):
                        part = res[rb * SUBLANES:(rb + 1) * SUBLANES, ab * LANES:(ab + 1) * LANES]
                        q = j * 8 + rb * 2 + ab
                        sums = jnp.where(lane == q, jnp.sum(part, axis=1, keepdims=True), sums)
            if u % SUBLANES == 0:
                by_row = [jnp.zeros((SUBLANES, LANES), F32)] * SUBLANES
            by_row = [jnp.where(sub == u % SUBLANES, jnp.broadcast_to(sums[r:r + 1], (SUBLANES, LANES)), by_row[r])
                      for r in range(SUBLANES)]
            if u % SUBLANES == SUBLANES - 1:
                first = pl.multiple_of(t0 + u - (SUBLANES - 1), SUBLANES)
                for r in range(SUBLANES):
                    sum_scr[r, pl.ds(first, SUBLANES), :] = by_row[r]

    def pair(k, carry):
        for slot in range(2):
            batch = 2 * k + slot
            index_copy(cur, batch, slot).wait()

            @pl.when(batch + 1 < per_step)
            def _():
                index_copy(cur, batch + 1, 1 - slot).start()

            tokens(batch * PEER_UNROLL, idx_smem.at[slot])
            select_head(snext_ref, batch)
        return carry

    lax.fori_loop(0, per_step // 2, pair, 0)
    publish(1 - cur)

    @pl.when(step + 1 < pl.num_programs(0))
    def _():
        index_copy(1 - cur, 0, 0).start()

    rows_ref[...] = rows_buf[cur]
    a2 = jnp.zeros((tb, 2 * PEER_PICKS), F32)
    for r in range(SUBLANES):
        shi, slo = _split2(sum_scr[r])
        a2 = a2 + _dot(shi, place_ref[r]) + _dot(slo, place_ref[r])
    ghi, glo = _split2(gate_buf[cur])
    g2 = _dot(ghi, dup_ref[...]) + _dot(glo, dup_ref[...])
    hbit2 = _dot(hbit_buf[cur].astype(BF16), dup_ref[...])
    parity = (lax.broadcasted_iota(I32, a2.shape, 1) % 2).astype(F32)
    coef = g2 * (0.5 * a2 * (1.0 + lax.erf(a2 * math.sqrt(0.5))))
    coefh_ref[...] = jnp.where(hbit2 == parity, coef, 0.0).astype(BF16)


def _peer_v_consts():
    col = jnp.arange(2 * PEER_PICKS * SUBLANES)
    ab, p, r = col // (PEER_PICKS * SUBLANES), (col // 16) % (PEER_PICKS // 2), col % 16
    src = jnp.arange(2 * PEER_PICKS)
    expand = ((src[:, None] // 2 == (2 * p + ab)[None, :]) & (src[:, None] % 2 == (r % 2)[None, :])).astype(BF16)
    kk = jnp.arange(PEER_PICKS * SUBLANES)
    diag = ((kk[None, :] % 16) // 2 == (jnp.arange(16) % SUBLANES)[:, None]).astype(F32)
    return expand, diag


def _peer_v_kernel(row_hbm, coefh_ref, x_ref, tab_ref, expand_ref, diag_ref, g_ref, b_ref, o_ref, lrow_scr, y_scr,
                   idx_smem, sem):
    tb = x_ref.shape[0]
    kdim = PEER_PICKS * SUBLANES
    lrow_scr[...] = _dot(coefh_ref[...], expand_ref[...])
    diag = diag_ref[...]
    sub = lax.broadcasted_iota(I32, (SUBLANES, LANES), 0)

    def tokens(t0, idx):
        by_row = None
        for u in range(PEER_UNROLL):
            t = t0 + u
            even = jnp.broadcast_to(lrow_scr[pl.ds(t, 1), 0:kdim], (SUBLANES, kdim))
            odd = jnp.broadcast_to(lrow_scr[pl.ds(t, 1), kdim:2 * kdim], (SUBLANES, kdim))
            lhs = (jnp.concatenate([even, odd], axis=0) * diag).astype(BF16)
            tiles = []
            for p in range(PEER_PICKS // 2):
                wa = _table_tile(tab_ref, idx[u, 2 * p])
                wb = _table_tile(tab_ref, idx[u, 2 * p + 1])
                tiles.append(jnp.concatenate([wa, wb], axis=1))
            res = _dot(lhs, jnp.concatenate(tiles, axis=0))
            y = res[0:SUBLANES, 0:LANES] + res[SUBLANES:, LANES:]
            if u % SUBLANES == 0:
                by_row = [jnp.zeros((SUBLANES, LANES), F32)] * SUBLANES
            by_row = [jnp.where(sub == u % SUBLANES, jnp.broadcast_to(y[r:r + 1], (SUBLANES, LANES)), by_row[r])
                      for r in range(SUBLANES)]
            if u % SUBLANES == SUBLANES - 1:
                first = pl.multiple_of(t0 + u - (SUBLANES - 1), SUBLANES)
                for r in range(SUBLANES):
                    y_scr[r, pl.ds(first, SUBLANES), :] = by_row[r]

    _for_each_token_batch(row_hbm, idx_smem, sem, tb, tokens)
    z = [DEEPNORM_ALPHA * x_ref[:, r * LANES:(r + 1) * LANES] + y_scr[r] for r in range(SUBLANES)]
    cnt = SUBLANES * LANES
    mu = sum(jnp.sum(zr, axis=1, keepdims=True) for zr in z) / cnt
    zc = [zr - mu for zr in z]
    var = sum(jnp.sum(c * c, axis=1, keepdims=True) for c in zc) / cnt
    rstd = lax.rsqrt(var + LN_EPS)
    for r in range(SUBLANES):
        o_ref[:, r * LANES:(r + 1) * LANES] = zc[r] * rstd * g_ref[r:r + 1, :] + b_ref[r:r + 1, :]


def _peer_ffn(h2, scores, expert_u, expert_v, ln_g, ln_b):
    n, d = h2.shape
    rows8 = d // LANES
    assert rows8 == SUBLANES and PEER_PICKS == LANES and expert_u.shape[0] == 2 * PEER_HALF_EXPERTS
    x3 = h2.reshape(n, rows8, LANES)
    u_pk = _pack_table(expert_u)
    v_pk = _pack_table(expert_v)
    hbm = pl.BlockSpec(memory_space=pl.ANY)
    staging = [pltpu.SMEM((2, PEER_UNROLL, PEER_PICKS), I32), pltpu.SemaphoreType.DMA((2,))]
    picks = pl.BlockSpec((PEER_BLOCK, PEER_PICKS), lambda i: (i, 0))
    picks2 = pl.BlockSpec((PEER_BLOCK, 2 * PEER_PICKS), lambda i: (i, 0))
    tok = pl.BlockSpec((PEER_BLOCK, rows8, LANES), lambda i: (i, 0, 0))
    flat = pl.BlockSpec((PEER_BLOCK, d), lambda i: (i, 0))
    table = pl.BlockSpec(u_pk.shape, lambda i: (0, 0), pipeline_mode=pl.Buffered(1))
    u_consts = _peer_u_consts()
    nblocks = n // PEER_BLOCK
    sblk = (1,) + scores.shape[1:]
    coefh, row = pl.pallas_call(
        _peer_u_kernel,
        grid=(nblocks,),
        in_specs=[pl.BlockSpec(sblk, lambda i: (0, 0, 0, 0)),
                  pl.BlockSpec(sblk, lambda i: (jnp.minimum(i + 1, nblocks - 1), 0, 0, 0)),
                  tok, table] + [_full(c.shape) for c in u_consts],
        out_specs=[picks2, picks],
        out_shape=[jax.ShapeDtypeStruct((n, 2 * PEER_PICKS), BF16), jax.ShapeDtypeStruct((n, PEER_PICKS), I32)],
        scratch_shapes=[pltpu.VMEM((SUBLANES, PEER_BLOCK, LANES), F32),
                        pltpu.VMEM((PEER_PICKS, PEER_BLOCK), I32), pltpu.VMEM((PEER_PICKS, PEER_BLOCK), F32),
                        pltpu.VMEM((2, PEER_BLOCK, PEER_PICKS), I32), pltpu.VMEM((2, PEER_BLOCK, PEER_PICKS), F32),
                        pltpu.VMEM((2, PEER_BLOCK, PEER_PICKS), F32)] + staging,
        compiler_params=_params("arbitrary"),
        name="peer_u",
    )(scores, scores, x3, u_pk, *u_consts)
    v_consts = _peer_v_consts()
    out = pl.pallas_call(
        _peer_v_kernel,
        grid=(n // PEER_BLOCK,),
        in_specs=[hbm, picks2, flat, table] + [_full(c.shape) for c in v_consts]
        + [_full((rows8, LANES)), _full((rows8, LANES))],
        out_specs=flat,
        out_shape=jax.ShapeDtypeStruct((n, d), F32),
        scratch_shapes=[pltpu.VMEM((PEER_BLOCK, 2 * PEER_PICKS * SUBLANES), F32),
                        pltpu.VMEM((rows8, PEER_BLOCK, LANES), F32)] + staging,
        compiler_params=_params("arbitrary"),
        name="peer_v",
    )(row, coefh, h2, v_pk, *v_consts, ln_g.reshape(rows8, LANES), ln_b.reshape(rows8, LANES))
    return out


def kernel(x, mem, ln_in_g, ln_in_b, w_in, conv_w, conv_b, mlstm_i_bias, mlstm_f_bias, mlstm_norm_g, rel_bias, w_out, ln1_g, ln1_b, xattn_w_q, xattn_w_kv, xattn_w_o, ln2_g, ln2_b, peer_w_query, peer_sub_keys, peer_u, peer_v, ln3_g, ln3_b):
    batch, seq, d = x.shape
    n = batch * seq
    assert w_in.shape[0] == DEPTH
    h, qkv, mqk, mv, mo, gates = _ln_inproj(x.reshape(n, d), ln_in_g, ln_in_b, w_in[0])
    for l in range(DEPTH):
        if l > 0:
            raise NotImplementedError("input projection of deeper layers")
        att = _attention(qkv, rel_bias[l], batch, seq)
        hm = _mlstm(mqk, mv, mo, gates, conv_w[l], conv_b[l], mlstm_i_bias[l], mlstm_f_bias[l], mlstm_norm_g[l],
                    batch, seq)
        h = _outproj(att.reshape(n, ATT_WIDTH), hm.reshape(n, MLSTM_WIDTH), h, w_out[l], ln1_g[l], ln1_b[l])
        k2, v2 = _kvproj(mem.reshape(-1, d), xattn_w_kv[l])
        m = mem.shape[1]
        h = _xattn(h.reshape(batch, seq, d), k2.reshape(batch, m, d), v2.reshape(batch, m, d),
                   xattn_w_q[l], xattn_w_o[l], ln2_g[l], ln2_b[l]).reshape(n, d)
        scores = _peer_scores(h, peer_w_query[l], peer_sub_keys[l])
        h = _peer_ffn(h, scores, peer_u[l], peer_v[l], ln3_g[l], ln3_b[l])
    return h.reshape(batch, seq, d)
```
